```python
import jax
import jax.numpy as jnp
from jax import lax
import numpy as np

D_MODEL = 2048
BATCH = 2
SEQ = 4096
DEPTH = 4

GRID_W = 64
CTX_LEN = 256
N_MIXERS = 4
EPS = 1e-6
ROPE_BASE = 10000.0
NEG_INF = -1e30

RET_HEADS = 8
RET_DK = D_MODEL // RET_HEADS
RET_DV = 2 * D_MODEL // RET_HEADS
RET_CHUNK = 128

MLA_HEADS = 16
MLA_Q_RANK = 512
MLA_KV_RANK = 512
MLA_NOPE = 128
MLA_ROPE = 64
MLA_V = 128
ATTN_Q_BLOCK = 128

SC_WIDTH = 3

GQA_HEADS = 32
GQA_KV_HEADS = 4
GQA_GROUP = GQA_HEADS // GQA_KV_HEADS
GQA_HEAD_DIM = 64
WINDOW = 128

N_EXPERTS = 32
TOP_K = 4
EXPERT_FF = 896
SWIGLU_LIMIT = 7.0
SWIGLU_ALPHA = 1.702
MOE_BLOCK = 128

N_RET = (DEPTH + N_MIXERS - 1) // N_MIXERS
N_MLA = (DEPTH + N_MIXERS - 2) // N_MIXERS
N_SC = (DEPTH + N_MIXERS - 3) // N_MIXERS
N_GQA = (DEPTH + N_MIXERS - 4) // N_MIXERS

kernel_name = 'hybrid_retention_mla_shortconv_swa_moe_dit'


def rms_norm(x, g):
    xf = x.astype(jnp.float32)
    y = xf * lax.rsqrt(jnp.mean(xf * xf, axis=-1, keepdims=True) + EPS)
    return (y * g.astype(jnp.float32)).astype(x.dtype)


def modulate(h, shift, scale):
    return h * (1 + scale) + shift


def rope_1d(x, positions):
    half = x.shape[-1] // 2
    inv = ROPE_BASE ** (-jnp.arange(half, dtype=jnp.float32) / half)
    ang = positions.astype(jnp.float32)[:, None] * inv
    c = jnp.cos(ang)[:, None].astype(x.dtype)
    s = jnp.sin(ang)[:, None].astype(x.dtype)
    x1, x2 = x[..., :half], x[..., half:]
    return jnp.concatenate([x1 * c - x2 * s, x2 * c + x1 * s], axis=-1)


def axial_rope_tables(rows, rot_dim):
    n_freq = rot_dim // 4
    row = jnp.repeat(jnp.arange(rows, dtype=jnp.float32), GRID_W)
    col = jnp.tile(jnp.arange(GRID_W, dtype=jnp.float32), rows)
    pos = jnp.stack([row, col], axis=-1)
    inv_freq = ROPE_BASE ** (-jnp.arange(n_freq, dtype=jnp.float32) / n_freq)
    ang = pos[:, :, None] * inv_freq
    return jnp.cos(ang), jnp.sin(ang)


def apply_axial_rope(x, cos, sin):
    n_freq = cos.shape[-1]
    xs = x.reshape(x.shape[:-1] + (2, 2, n_freq))
    x1, x2 = xs[..., 0, :], xs[..., 1, :]
    c = cos[:, None].astype(x.dtype)
    s = sin[:, None].astype(x.dtype)
    out = jnp.stack([x1 * c - x2 * s, x2 * c + x1 * s], axis=-2)
    return out.reshape(x.shape)


def retention_scan(q, k, v, log_gamma, state0):
    b, h, length, _ = q.shape
    dv = v.shape[-1]
    n_chunks = length // RET_CHUNK
    i = jnp.arange(RET_CHUNK, dtype=jnp.float32)
    diff = i[:, None] - i[None, :]
    lg = log_gamma[:, None, None]
    intra = jnp.where(diff >= 0, jnp.exp(jnp.maximum(diff, 0.0) * lg), 0.0)
    q_decay = jnp.exp((i + 1.0) * log_gamma[:, None])[:, :, None]
    k_decay = jnp.exp((RET_CHUNK - 1.0 - i) * log_gamma[:, None])[:, :, None]
    chunk_decay = jnp.exp(RET_CHUNK * log_gamma)[:, None, None]

    def to_chunks(t):
        t = t.astype(jnp.float32).reshape(b, h, n_chunks, RET_CHUNK, t.shape[-1])
        return jnp.moveaxis(t, 2, 0)

    def step(state, qkv):
        qc, kc, vc = qkv
        scores = jnp.einsum('bhid,bhjd->bhij', qc, kc) * intra
        out = (jnp.einsum('bhij,bhjv->bhiv', scores, vc)
               + jnp.einsum('bhid,bhdv->bhiv', qc * q_decay, state))
        state = state * chunk_decay + jnp.einsum('bhjd,bhjv->bhdv', kc * k_decay, vc)
        return state, out

    state, outs = lax.scan(step, state0, (to_chunks(q), to_chunks(k), to_chunks(v)))
    return jnp.moveaxis(outs, 0, 2).reshape(b, h, length, dv), state


def softmax_with_sink(s, sink):
    if sink is None:
        return jax.nn.softmax(s, axis=-1)
    col = jnp.broadcast_to(sink.astype(jnp.float32)[None, :, :, None, None], s.shape[:-1] + (1,))
    return jax.nn.softmax(jnp.concatenate([s, col], axis=-1), axis=-1)[..., :-1]


def joint_attend(q, k_loc, v_loc, k_ctx, v_ctx, scale, sink, valid):
    s_loc = jnp.einsum('bqhgd,bkhd->bhgqk', q, k_loc).astype(jnp.float32) * scale
    if valid is not None:
        s_loc = jnp.where(valid, s_loc, NEG_INF)
    s_ctx = jnp.einsum('bqhgd,bkhd->bhgqk', q, k_ctx).astype(jnp.float32) * scale
    p = softmax_with_sink(jnp.concatenate([s_loc, s_ctx], axis=-1), sink).astype(v_loc.dtype)
    n_loc = k_loc.shape[1]
    return (jnp.einsum('bhgqk,bkhd->bqhgd', p[..., :n_loc], v_loc)
            + jnp.einsum('bhgqk,bkhd->bqhgd', p[..., n_loc:], v_ctx))


def context_attention(q, k, v, scale, sink):
    s = jnp.einsum('bqhgd,bkhd->bhgqk', q, k).astype(jnp.float32) * scale
    p = softmax_with_sink(s, sink).astype(v.dtype)
    return jnp.einsum('bhgqk,bkhd->bqhgd', p, v)


def dense_latent_attention(q, k, v, k_ctx, v_ctx, scale):
    b, length = q.shape[:2]
    nb = length // ATTN_Q_BLOCK
    q_blocks = jnp.moveaxis(q.reshape((b, nb, ATTN_Q_BLOCK) + q.shape[2:]), 1, 0)
    out = lax.map(lambda qb: joint_attend(qb, k, v, k_ctx, v_ctx, scale, None, None), q_blocks)
    return jnp.moveaxis(out, 0, 1).reshape((b, length) + out.shape[3:])


def window_latent_attention(q, k, v, k_ctx, v_ctx, scale, sink):
    b, length = q.shape[:2]
    nb = length // WINDOW
    band = 3 * WINDOW
    pad = ((0, 0), (WINDOW, WINDOW), (0, 0), (0, 0))
    k_pad, v_pad = jnp.pad(k, pad), jnp.pad(v, pad)
    rel = jnp.arange(band)[None, :] - WINDOW - jnp.arange(WINDOW)[:, None]

    def one_block(blk):
        start = blk * WINDOW
        q_b = lax.dynamic_slice_in_dim(q, start, WINDOW, axis=1)
        k_b = lax.dynamic_slice_in_dim(k_pad, start, band, axis=1)
        v_b = lax.dynamic_slice_in_dim(v_pad, start, band, axis=1)
        key_pos = start - WINDOW + jnp.arange(band)
        valid = (jnp.abs(rel) <= WINDOW) & ((key_pos >= 0) & (key_pos < length))[None, :]
        return joint_attend(q_b, k_b, v_b, k_ctx, v_ctx, scale, sink, valid)

    out = lax.map(one_block, jnp.arange(nb))
    return jnp.moveaxis(out, 0, 1).reshape((b, length) + out.shape[3:])


def retention_mixer(h_lat, h_ctx, w_in, decay_logit, gn_gain, w_out, ctx_out):
    b, n_lat = h_lat.shape[:2]
    n_ctx = h_ctx.shape[1]
    qk_w = RET_HEADS * RET_DK
    v_w = RET_HEADS * RET_DV

    def project(h, positions):
        length = h.shape[1]
        q, k, v, gate = jnp.split(h @ w_in, [qk_w, 2 * qk_w, 2 * qk_w + v_w], axis=-1)
        q = rope_1d(q.reshape(b, length, RET_HEADS, RET_DK), positions)
        k = rope_1d(k.reshape(b, length, RET_HEADS, RET_DK), positions) * (RET_DK ** -0.5)
        v = v.reshape(b, length, RET_HEADS, RET_DV)
        heads_first = lambda t: jnp.transpose(t, (0, 2, 1, 3))
        return heads_first(q), heads_first(k), heads_first(v), gate

    q_c, k_c, v_c, g_c = project(h_ctx, jnp.arange(n_ctx))
    q_l, k_l, v_l, g_l = project(h_lat, n_ctx + jnp.arange(n_lat))
    log_gamma = jax.nn.log_sigmoid(decay_logit.astype(jnp.float32))
    zero = jnp.zeros((b, RET_HEADS, RET_DK, RET_DV), jnp.float32)
    rev = lambda t: jnp.flip(t, axis=2)
    o_cf, s_cf = retention_scan(q_c, k_c, v_c, log_gamma[0], zero)
    o_lf, _ = retention_scan(q_l, k_l, v_l, log_gamma[0], s_cf)
    o_cb, s_cb = retention_scan(rev(q_c), rev(k_c), rev(v_c), log_gamma[1], zero)
    o_lb, _ = retention_scan(rev(q_l), rev(k_l), rev(v_l), log_gamma[1], s_cb)

    def finish(o, gate):
        o = o * lax.rsqrt(jnp.mean(o * o, axis=-1, keepdims=True) + EPS)
        o = jnp.transpose(o, (0, 2, 1, 3)).reshape(o.shape[0], o.shape[2], v_w)
        o = (o * gn_gain.astype(jnp.float32)).astype(gate.dtype)
        return (jax.nn.silu(gate) * o) @ w_out

    y_lat = finish(o_lf + rev(o_lb), g_l)
    y_ctx = finish(o_cf + rev(o_cb), g_c) if ctx_out else None
    return y_lat, y_ctx


def mla_mixer(h_lat, h_ctx, rope_tables, w_in, q_a_norm, kv_a_norm, w_q_b, w_kv_b,
              q_norm, k_norm, w_out, ctx_out):
    scale = (MLA_NOPE + MLA_ROPE) ** -0.5

    def project(h, tables):
        b, length = h.shape[:2]
        q_lat, kv_lat, k_rope = jnp.split(h @ w_in, [MLA_Q_RANK, MLA_Q_RANK + MLA_KV_RANK], axis=-1)
        q = (rms_norm(q_lat, q_a_norm) @ w_q_b).reshape(b, length, MLA_HEADS, MLA_NOPE + MLA_ROPE)
        kv = (rms_norm(kv_lat, kv_a_norm) @ w_kv_b).reshape(b, length, MLA_HEADS, MLA_NOPE + MLA_V)
        k_nope, v = jnp.split(kv, [MLA_NOPE], axis=-1)
        k_rope = jnp.broadcast_to(k_rope[:, :, None, :], (b, length, MLA_HEADS, MLA_ROPE))
        q = rms_norm(q, q_norm)
        k = rms_norm(jnp.concatenate([k_nope, k_rope], axis=-1), k_norm)
        if tables is not None:
            q = jnp.concatenate([q[..., :MLA_NOPE], apply_axial_rope(q[..., MLA_NOPE:], *tables)], axis=-1)
            k = jnp.concatenate([k[..., :MLA_NOPE], apply_axial_rope(k[..., MLA_NOPE:], *tables)], axis=-1)
        return q[:, :, :, None, :], k, v

    q_c, k_c, v_c = project(h_ctx, None)
    q_l, k_l, v_l = project(h_lat, rope_tables)
    out_proj = lambda o: o.reshape(o.shape[0], o.shape[1], MLA_HEADS * MLA_V) @ w_out
    y_lat = out_proj(dense_latent_attention(q_l, k_l, v_l, k_c, v_c, scale))
    y_ctx = out_proj(context_attention(q_c, k_c, v_c, scale, None)) if ctx_out else None
    return y_lat, y_ctx


def shortconv_mixer(h_lat, h_ctx, w_in, conv_w, conv_b, w_out, ctx_out):
    def mix(h):
        b_gate, c_gate, u = jnp.split(h @ w_in, 3, axis=-1)
        z = c_gate * u
        z = lax.conv_general_dilated(
            z, conv_w[:, None, :].astype(z.dtype), window_strides=(1,),
            padding=((SC_WIDTH // 2, SC_WIDTH // 2),),
            dimension_numbers=('NWC', 'WIO', 'NWC'),
            feature_group_count=z.shape[-1]) + conv_b
        return (b_gate * z) @ w_out

    return mix(h_lat), (mix(h_ctx) if ctx_out else None)


def gqa_mixer(h_lat, h_ctx, rope_tables, w_in, b_in, q_norm, k_norm, sink, w_out, b_out, ctx_out):
    scale = GQA_HEAD_DIM ** -0.5
    sink = sink.reshape(GQA_KV_HEADS, GQA_GROUP)

    def project(h, tables):
        b, length = h.shape[:2]
        q, k, v = jnp.split(h @ w_in + b_in,
                            [GQA_HEADS * GQA_HEAD_DIM, (GQA_HEADS + GQA_KV_HEADS) * GQA_HEAD_DIM], axis=-1)
        q = rms_norm(q.reshape(b, length, GQA_HEADS, GQA_HEAD_DIM), q_norm)
        k = rms_norm(k.reshape(b, length, GQA_KV_HEADS, GQA_HEAD_DIM), k_norm)
        v = v.reshape(b, length, GQA_KV_HEADS, GQA_HEAD_DIM)
        if tables is not None:
            q = apply_axial_rope(q, *tables)
            k = apply_axial_rope(k, *tables)
        return q.reshape(b, length, GQA_KV_HEADS, GQA_GROUP, GQA_HEAD_DIM), k, v

    out_proj = lambda o: o.reshape(o.shape[0], o.shape[1], GQA_HEADS * GQA_HEAD_DIM) @ w_out + b_out
    q_c, k_c, v_c = project(h_ctx, None)
    q_l, k_l, v_l = project(h_lat, rope_tables)
    y_lat = out_proj(window_latent_attention(q_l, k_l, v_l, k_c, v_c, scale, sink))
    y_ctx = out_proj(context_attention(q_c, k_c, v_c, scale, sink)) if ctx_out else None
    return y_lat, y_ctx


def moe_ffn(x, router_w, router_b, w_gu, b_gu, w_down, b_down):
    t, d = x.shape
    logits = (x @ router_w + router_b).astype(jnp.float32)
    top_val, top_idx = lax.top_k(logits, TOP_K)
    gates = jax.nn.softmax(top_val, axis=-1)
    n_assign = t * TOP_K
    expert_id = top_idx.reshape(-1)
    token_id = jnp.arange(n_assign) // TOP_K
    order = jnp.argsort(expert_id)
    sorted_expert = expert_id[order]
    counts = jnp.bincount(expert_id, length=N_EXPERTS)
    padded = (counts + MOE_BLOCK - 1) // MOE_BLOCK * MOE_BLOCK
    start = jnp.cumsum(counts) - counts
    pstart = jnp.cumsum(padded) - padded
    dest = pstart[sorted_expert] + (jnp.arange(n_assign) - start[sorted_expert])
    n_rows = (-(-n_assign // MOE_BLOCK) + N_EXPERTS) * MOE_BLOCK
    n_blocks = n_rows // MOE_BLOCK
    tok_sorted = token_id[order]
    x_rows = jnp.zeros((n_rows, d), x.dtype).at[dest].set(x[tok_sorted])
    block_start = jnp.arange(n_blocks) * MOE_BLOCK
    block_expert = jnp.minimum(
        jnp.sum(block_start[:, None] >= (pstart + padded)[None, :], axis=1), N_EXPERTS - 1)

    def expert_block(args):
        xb, e = args
        h = xb @ w_gu[e] + b_gu[e]
        h_glu, h_lin = jnp.split(h, 2, axis=-1)
        h_glu = jnp.minimum(h_glu, SWIGLU_LIMIT)
        h_lin = jnp.clip(h_lin, -SWIGLU_LIMIT, SWIGLU_LIMIT)
        act = h_glu * jax.nn.sigmoid(SWIGLU_ALPHA * h_glu) * (h_lin + 1)
        return act @ w_down[e] + b_down[e]

    y_rows = lax.map(expert_block, (x_rows.reshape(n_blocks, MOE_BLOCK, d), block_expert))
    y_assign = y_rows.reshape(n_rows, d)[dest]
    w_sorted = gates.reshape(-1)[order].astype(x.dtype)
    return jax.ops.segment_sum(y_assign * w_sorted[:, None], tok_sorted, num_segments=t)


def setup_inputs(seed: int = 0) -> dict:
    key = jax.random.key(seed)
    keys = iter(jax.random.split(key, 48))

    def normal(shape, scale):
        return jax.random.normal(next(keys), shape, jnp.float32) * scale

    def gain(shape):
        return 1.0 + normal(shape, 0.05)

    d = D_MODEL
    ret_cols = 2 * RET_HEADS * RET_DK + 2 * RET_HEADS * RET_DV
    mla_cols = MLA_Q_RANK + MLA_KV_RANK + MLA_ROPE
    gqa_cols = (GQA_HEADS + 2 * GQA_KV_HEADS) * GQA_HEAD_DIM
    base_decay = jnp.log(2.0 ** (5.0 + jnp.arange(RET_HEADS, dtype=jnp.float32)) - 1.0)
    return {
        'x': normal((BATCH, SEQ, d), 1.0),
        'c': normal((BATCH, d), 1.0),
        'ctx': normal((BATCH, CTX_LEN, d), 1.0),
        'c_ctx': normal((d,), 1.0),
        'norm_mix': gain((DEPTH, d)),
        'norm_ffn': gain((DEPTH, d)),
        'mod_w': normal((DEPTH, d, 6 * d), 0.5 * d ** -0.5),
        'mod_b': normal((DEPTH, 6 * d), 0.02),
        'ret_w_in': normal((N_RET, d, ret_cols), d ** -0.5),
        'ret_decay': base_decay + normal((N_RET, 2, RET_HEADS), 0.1),
        'ret_gn': gain((N_RET, RET_HEADS * RET_DV)),
        'ret_w_out': normal((N_RET, RET_HEADS * RET_DV, d), (RET_HEADS * RET_DV) ** -0.5),
        'mla_w_in': normal((N_MLA, d, mla_cols), d ** -0.5),
        'mla_q_a_norm': gain((N_MLA, MLA_Q_RANK)),
        'mla_kv_a_norm': gain((N_MLA, MLA_KV_RANK)),
        'mla_w_q_b': normal((N_MLA, MLA_Q_RANK, MLA_HEADS * (MLA_NOPE + MLA_ROPE)), MLA_Q_RANK ** -0.5),
        'mla_w_kv_b': normal((N_MLA, MLA_KV_RANK, MLA_HEADS * (MLA_NOPE + MLA_V)), MLA_KV_RANK ** -0.5),
        'mla_q_norm': gain((N_MLA, MLA_NOPE + MLA_ROPE)),
        'mla_k_norm': gain((N_MLA, MLA_NOPE + MLA_ROPE)),
        'mla_w_out': normal((N_MLA, MLA_HEADS * MLA_V, d), (MLA_HEADS * MLA_V) ** -0.5),
        'sc_w_in': normal((N_SC, d, 3 * d), d ** -0.5),
        'sc_conv_w': normal((N_SC, SC_WIDTH, d), SC_WIDTH ** -0.5),
        'sc_conv_b': normal((N_SC, d), 0.02),
        'sc_w_out': normal((N_SC, d, d), d ** -0.5),
        'gqa_w_in': normal((N_GQA, d, gqa_cols), d ** -0.5),
        'gqa_b_in': normal((N_GQA, gqa_cols), 0.02),
        'gqa_q_norm': gain((N_GQA, GQA_HEAD_DIM)),
        'gqa_k_norm': gain((N_GQA, GQA_HEAD_DIM)),
        'gqa_sink': normal((N_GQA, GQA_HEADS), 0.5),
        'gqa_w_out': normal((N_GQA, GQA_HEADS * GQA_HEAD_DIM, d), (GQA_HEADS * GQA_HEAD_DIM) ** -0.5),
        'gqa_b_out': normal((N_GQA, d), 0.02),
        'moe_router_w': normal((DEPTH, d, N_EXPERTS), d ** -0.5),
        'moe_router_b': normal((DEPTH, N_EXPERTS), 0.01),
        'moe_w_gu': normal((DEPTH, N_EXPERTS, d, 2 * EXPERT_FF), d ** -0.5),
        'moe_b_gu': normal((DEPTH, N_EXPERTS, 2 * EXPERT_FF), 0.02),
        'moe_w_down': normal((DEPTH, N_EXPERTS, EXPERT_FF, d), EXPERT_FF ** -0.5),
        'moe_b_down': normal((DEPTH, N_EXPERTS, d), 0.02),
    }


def reference(x, c, ctx, c_ctx, norm_mix, norm_ffn, mod_w, mod_b,
              ret_w_in, ret_decay, ret_gn, ret_w_out,
              mla_w_in, mla_q_a_norm, mla_kv_a_norm, mla_w_q_b, mla_w_kv_b, mla_q_norm, mla_k_norm, mla_w_out,
              sc_w_in, sc_conv_w, sc_conv_b, sc_w_out,
              gqa_w_in, gqa_b_in, gqa_q_norm, gqa_k_norm, gqa_sink, gqa_w_out, gqa_b_out,
              moe_router_w, moe_router_b, moe_w_gu, moe_b_gu, moe_w_down, moe_b_down):
    b, n_lat, d = x.shape
    n_ctx = ctx.shape[1]
    rows = n_lat // GRID_W
    rope_mla = axial_rope_tables(rows, MLA_ROPE)
    rope_gqa = axial_rope_tables(rows, GQA_HEAD_DIM)
    silu_c = jax.nn.silu(c)
    silu_cc = jax.nn.silu(c_ctx)
    for i in range(DEPTH):
        ctx_out = i < DEPTH - 1
        kind, j = i % N_MIXERS, i // N_MIXERS
        mod_lat = jnp.split((silu_c @ mod_w[i] + mod_b[i])[:, None, :], 6, axis=-1)
        mod_ctx = jnp.split(silu_cc @ mod_w[i] + mod_b[i], 6, axis=-1)
        h_lat = modulate(rms_norm(x, norm_mix[i]), mod_lat[0], mod_lat[1])
        h_ctx = modulate(rms_norm(ctx, norm_mix[i]), mod_ctx[0], mod_ctx[1])
        if kind == 0:
            y_lat, y_ctx = retention_mixer(h_lat, h_ctx, ret_w_in[j], ret_decay[j], ret_gn[j],
                                           ret_w_out[j], ctx_out)
        elif kind == 1:
            y_lat, y_ctx = mla_mixer(h_lat, h_ctx, rope_mla, mla_w_in[j], mla_q_a_norm[j], mla_kv_a_norm[j],
                                     mla_w_q_b[j], mla_w_kv_b[j], mla_q_norm[j], mla_k_norm[j],
                                     mla_w_out[j], ctx_out)
        elif kind == 2:
            y_lat, y_ctx = shortconv_mixer(h_lat, h_ctx, sc_w_in[j], sc_conv_w[j], sc_conv_b[j],
                                           sc_w_out[j], ctx_out)
        else:
            y_lat, y_ctx = gqa_mixer(h_lat, h_ctx, rope_gqa, gqa_w_in[j], gqa_b_in[j], gqa_q_norm[j],
                                     gqa_k_norm[j], gqa_sink[j], gqa_w_out[j], gqa_b_out[j], ctx_out)
        x = x + mod_lat[2] * y_lat
        moe_args = (moe_router_w[i], moe_router_b[i], moe_w_gu[i], moe_b_gu[i], moe_w_down[i], moe_b_down[i])
        f_lat = modulate(rms_norm(x, norm_ffn[i]), mod_lat[3], mod_lat[4]).reshape(b * n_lat, d)
        if ctx_out:
            ctx = ctx + mod_ctx[2] * y_ctx
            f_ctx = modulate(rms_norm(ctx, norm_ffn[i]), mod_ctx[3], mod_ctx[4]).reshape(b * n_ctx, d)
            f = moe_ffn(jnp.concatenate([f_lat, f_ctx], axis=0), *moe_args)
            x = x + mod_lat[5] * f[:b * n_lat].reshape(b, n_lat, d)
            ctx = ctx + mod_ctx[5] * f[b * n_lat:].reshape(b, n_ctx, d)
        else:
            x = x + mod_lat[5] * moe_ffn(f_lat, *moe_args).reshape(b, n_lat, d)
    return x
```

```python
import functools

import jax
import jax.numpy as jnp
from jax import lax
from jax.experimental import pallas as pl
from jax.experimental.pallas import tpu as pltpu

F32 = jnp.float32
BF16 = jnp.bfloat16

EPS = 1e-6
ROPE_BASE = 10000.0
NEG_INF = -1e30
GRID_W = 64

RET_HEADS = 8
RET_CHUNK = 128
MLA_HEADS = 16
MLA_NOPE = 128
MLA_ROPE = 64
MLA_V = 128
MLA_RANK = 512
GQA_HEADS = 32
GQA_KV_HEADS = 4
GQA_GROUP = GQA_HEADS // GQA_KV_HEADS
GQA_HEAD_DIM = 64
WINDOW = 128
N_EXPERTS = 32
TOP_K = 4
SWIGLU_LIMIT = 7.0
SWIGLU_ALPHA = 1.702
N_MIXERS = 4

LANES = 128
VMEM_LIMIT = 56 * 1024 * 1024
MOE_TM = 256
MOD_ROWS = 8


def _params(n_axes, vmem=VMEM_LIMIT):
    return pltpu.CompilerParams(dimension_semantics=("arbitrary",) * n_axes,
                                vmem_limit_bytes=vmem)


def _row_tile(l_tot, cap):
    best = 16
    for t in range(16, cap + 1, 16):
        if l_tot % t == 0:
            best = t
    return best


def _is_ctx_rows(i, tm, tpb, lc):
    rows = lax.rem(i, tpb) * tm + lax.broadcasted_iota(jnp.int32, (tm, 1), 0)
    return rows < lc


def _pick_mod(is_ctx, cv_ref, lv_ref, r):
    return jnp.where(is_ctx, cv_ref[r:r + 1, :], lv_ref[0, r:r + 1, :])


def _sigmoid(x):
    return 1.0 / (1.0 + jnp.exp(-x))


def _rope3(x, c, s1, s2):
    return x * c + pltpu.roll(x, LANES - 16, axis=1) * s1 + pltpu.roll(x, 16, axis=1) * s2


def _mod_body(c_ref, w_ref, b_ref, o_ref):
    c = c_ref[...]
    s = (c * _sigmoid(c)).astype(BF16)
    o_ref[0] = jnp.dot(s, w_ref[0].astype(BF16), preferred_element_type=F32) + b_ref[0]


def _modulation(cvec, mod_w, mod_b):
    depth, d, n = mod_w.shape
    tn = 2048
    return pl.pallas_call(
        _mod_body,
        grid=(depth, n // tn),
        in_specs=[pl.BlockSpec((MOD_ROWS, d), lambda l, j: (0, 0)),
                  pl.BlockSpec((1, d, tn), lambda l, j: (l, 0, j)),
                  pl.BlockSpec((1, 1, tn), lambda l, j: (l, 0, j))],
        out_specs=pl.BlockSpec((1, MOD_ROWS, tn), lambda l, j: (l, 0, j)),
        out_shape=jax.ShapeDtypeStruct((depth, MOD_ROWS, n), F32),
        compiler_params=_params(2),
        name="modulation",
    )(cvec, mod_w, mod_b.reshape(depth, 1, n))


def _norm_mod_body(x_ref, g_ref, cv_ref, lv_ref, o_ref, *, tm, tpb, lc, r_shift, r_scale):
    x = x_ref[...]
    y = x * lax.rsqrt(jnp.mean(x * x, axis=-1, keepdims=True) + EPS) * g_ref[...]
    is_ctx = _is_ctx_rows(pl.program_id(0), tm, tpb, lc)
    shift = _pick_mod(is_ctx, cv_ref, lv_ref, r_shift)
    scale = _pick_mod(is_ctx, cv_ref, lv_ref, r_scale)
    o_ref[...] = (y * (1.0 + scale) + shift).astype(BF16)


def _norm_mod(x, gain, cv, lv, *, l_tot, lc, r_shift, r_scale):
    m, d = x.shape
    tm = _row_tile(l_tot, 544)
    tpb = l_tot // tm
    body = functools.partial(_norm_mod_body, tm=tm, tpb=tpb, lc=lc, r_shift=r_shift, r_scale=r_scale)
    return pl.pallas_call(
        body,
        grid=(m // tm,),
        in_specs=[pl.BlockSpec((tm, d), lambda i: (i, 0)),
                  pl.BlockSpec((1, d), lambda i: (0, 0)),
                  pl.BlockSpec((MOD_ROWS, d), lambda i: (0, 0)),
                  pl.BlockSpec((1, MOD_ROWS, d), lambda i: (i // tpb, 0, 0))],
        out_specs=pl.BlockSpec((tm, d), lambda i: (i, 0)),
        out_shape=jax.ShapeDtypeStruct((m, d), BF16),
        compiler_params=_params(1),
        name="norm_mod",
    )(x, gain.reshape(1, d), cv, lv)


def _mm_body(*refs, n_extra, n_out, epilogue, prologue):
    a_ref, w_ref = refs[0], refs[1]
    extra = refs[2:2 + n_extra]
    outs = refs[2 + n_extra:2 + n_extra + n_out]
    scratch = refs[2 + n_extra + n_out:]
    a = a_ref[...] if prologue is None else prologue(a_ref, extra, scratch)
    acc = jnp.dot(a, w_ref[...].astype(BF16), preferred_element_type=F32)
    epilogue(acc, extra, outs)


def _matmul(a, w, extras, extra_specs, out_shapes, out_specs, epilogue, *, tm, tn, name,
            prologue=None, scratch_shapes=()):
    m, k = a.shape
    n = w.shape[1]
    body = functools.partial(_mm_body, n_extra=len(extras), n_out=len(out_shapes),
                             epilogue=epilogue, prologue=prologue)
    return pl.pallas_call(
        body,
        grid=(m // tm, n // tn),
        in_specs=[pl.BlockSpec((tm, k), lambda i, j: (i, 0)),
                  pl.BlockSpec((k, tn), lambda i, j: (0, j))] + list(extra_specs),
        out_specs=list(out_specs),
        out_shape=list(out_shapes),
        scratch_shapes=list(scratch_shapes),
        compiler_params=_params(2),
        name=name,
    )(a, w, *extras)


def _plain_epilogue(out_dtype):
    def epilogue(acc, extra, outs):
        outs[0][...] = acc.astype(out_dtype)
    return epilogue


def _bias_epilogue(out_dtype):
    def epilogue(acc, extra, outs):
        outs[0][...] = (acc + extra[0][...]).astype(out_dtype)
    return epilogue


def _residual_matmul(a, w, bias, x_res, cv, lv, *, l_tot, lc, r_gate, tm, tn, name,
                     prologue=None, pro_extras=(), pro_specs=(), scratch_shapes=()):
    m = a.shape[0]
    n = w.shape[1]
    tpb = l_tot // tm
    n_pro = len(pro_extras)

    def epilogue(acc, extra, outs):
        b_ref, x_ref, cv_ref, lv_ref = extra[n_pro:n_pro + 4]
        is_ctx = _is_ctx_rows(pl.program_id(0), tm, tpb, lc)
        gate = _pick_mod(is_ctx, cv_ref, lv_ref, r_gate)
        outs[0][...] = x_ref[...] + gate * (acc + b_ref[...])

    extras = list(pro_extras) + [bias.reshape(1, n), x_res, cv, lv]
    specs = list(pro_specs) + [
        pl.BlockSpec((1, tn), lambda i, j: (0, j)),
        pl.BlockSpec((tm, tn), lambda i, j: (i, j)),
        pl.BlockSpec((MOD_ROWS, tn), lambda i, j: (0, j)),
        pl.BlockSpec((1, MOD_ROWS, tn), lambda i, j: (i // tpb, 0, j))]
    return _matmul(a, w, extras, specs,
                   [jax.ShapeDtypeStruct((m, n), F32)],
                   [pl.BlockSpec((tm, tn), lambda i, j: (i, j))],
                   epilogue, tm=tm, tn=tn, name=name, prologue=prologue,
                   scratch_shapes=scratch_shapes)[0]


def _ret_in_proj(h, w_in, cos, sin, *, l_tot, d_model):
    m = h.shape[0]
    n = w_in.shape[1]
    tm = _row_tile(l_tot, 2176)
    tn = 512
    tpb = l_tot // tm
    dk = d_model // RET_HEADS
    half = dk // 2
    n_q_tiles = d_model // tn
    k_scale = dk ** -0.5

    def epilogue(acc, extra, outs):
        cos_ref, sin_ref = extra
        j = pl.program_id(1)

        @pl.when(j < 2 * n_q_tiles)
        def _():
            c = cos_ref[...]
            s = sin_ref[...]
            mult = jnp.where(j >= n_q_tiles, k_scale, 1.0)
            for hh in range(tn // dk):
                x1 = acc[:, hh * dk:hh * dk + half]
                x2 = acc[:, hh * dk + half:(hh + 1) * dk]
                outs[0][:, hh * dk:hh * dk + half] = ((x1 * c - x2 * s) * mult).astype(BF16)
                outs[0][:, hh * dk + half:(hh + 1) * dk] = ((x2 * c + x1 * s) * mult).astype(BF16)

        @pl.when(j >= 2 * n_q_tiles)
        def _():
            outs[0][...] = acc.astype(BF16)

    tab_spec = pl.BlockSpec((tm, half), lambda i, j: (lax.rem(i, tpb), 0))
    return _matmul(h, w_in, [cos, sin], [tab_spec, tab_spec],
                   [jax.ShapeDtypeStruct((m, n), BF16)],
                   [pl.BlockSpec((tm, tn), lambda i, j: (i, j))],
                   epilogue, tm=tm, tn=tn, name="ret_in_proj")[0]


def _ret_body(lg_ref, q_ref, k_ref, v_ref, gn_ref, o_ref, oacc_ref, s_ref, *, nc_ctx, nc_tot):
    h = pl.program_id(1)
    ch = RET_CHUNK
    dv = v_ref.shape[-1]
    ii = lax.broadcasted_iota(jnp.int32, (ch, ch), 0).astype(F32)
    jj = lax.broadcasted_iota(jnp.int32, (ch, ch), 1).astype(F32)
    ri = lax.broadcasted_iota(jnp.int32, (ch, 1), 0).astype(F32)
    for direction in (0, 1):
        lg = lg_ref[direction, h]
        if direction == 0:
            diff = ii - jj
            q_decay = jnp.exp((ri + 1.0) * lg)
            k_decay = jnp.exp((ch - 1.0 - ri) * lg)
        else:
            diff = jj - ii
            q_decay = jnp.exp((ch - ri) * lg)
            k_decay = jnp.exp(ri * lg)
        intra = jnp.where(diff >= 0, jnp.exp(jnp.maximum(diff, 0.0) * lg), 0.0)
        chunk_decay = jnp.exp(jnp.full((1, dv), float(ch), F32) * lg)
        s_ref[...] = jnp.zeros_like(s_ref)

        def step(t, carry, direction=direction, intra=intra, q_decay=q_decay,
                 k_decay=k_decay, chunk_decay=chunk_decay):
            if direction == 0:
                c = t
            else:
                c = jnp.where(t < nc_ctx, nc_ctx - 1 - t, nc_tot - 1 - (t - nc_ctx))
            rows = pl.ds(pl.multiple_of(c * ch, ch), ch)
            qc = q_ref[0, rows, :]
            kc = k_ref[0, rows, :]
            vc = v_ref[0, rows, :]
            scores = lax.dot_general(qc, kc, (((1,), (1,)), ((), ())),
                                     preferred_element_type=F32) * intra
            state = s_ref[...]
            out = (jnp.dot(scores.astype(BF16), vc, preferred_element_type=F32)
                   + jnp.dot((qc.astype(F32) * q_decay).astype(BF16), state.astype(BF16),
                             preferred_element_type=F32))
            kt = (kc.astype(F32) * k_decay).T.astype(BF16)
            s_ref[...] = state * chunk_decay + jnp.dot(kt, vc, preferred_element_type=F32)
            if direction == 0:
                oacc_ref[rows, :] = out
            else:
                o = oacc_ref[rows, :] + out
                o = o * lax.rsqrt(jnp.mean(o * o, axis=-1, keepdims=True) + EPS) * gn_ref[...]
                o_ref[0, rows, :] = o.astype(BF16)
            return carry

        lax.fori_loop(0, nc_tot, step, 0)


def _retention(proj, log_gamma, gn, *, b, l_tot, lc, d_model):
    dk = d_model // RET_HEADS
    dv = 2 * d_model // RET_HEADS
    proj3 = proj.reshape(b, l_tot, proj.shape[-1])
    k_blk0 = d_model // dk
    v_blk0 = 2 * d_model // dv
    body = functools.partial(_ret_body, nc_ctx=lc // RET_CHUNK, nc_tot=l_tot // RET_CHUNK)
    return pl.pallas_call(
        body,
        grid=(b, RET_HEADS),
        in_specs=[pl.BlockSpec(memory_space=pltpu.SMEM),
                  pl.BlockSpec((1, l_tot, dk), lambda bi, hi: (bi, 0, hi)),
                  pl.BlockSpec((1, l_tot, dk), lambda bi, hi: (bi, 0, k_blk0 + hi)),
                  pl.BlockSpec((1, l_tot, dv), lambda bi, hi: (bi, 0, v_blk0 + hi)),
                  pl.BlockSpec((1, dv), lambda bi, hi: (0, hi))],
        out_specs=pl.BlockSpec((1, l_tot, dv), lambda bi, hi: (bi, 0, hi)),
        out_shape=jax.ShapeDtypeStruct((b, l_tot, RET_HEADS * dv), BF16),
        scratch_shapes=[pltpu.VMEM((l_tot, dv), F32), pltpu.VMEM((dk, dv), F32)],
        compiler_params=_params(2),
        name="retention",
    )(log_gamma, proj3, proj3, proj3, gn.reshape(1, RET_HEADS * dv))


def _retention_layer(x, h, cv, lv, w_in, decay, gn, w_out, cos, sin, *, b, l_tot, lc):
    m, d = x.shape
    proj = _ret_in_proj(h, w_in, cos, sin, l_tot=l_tot, d_model=d)
    log_gamma = jax.nn.log_sigmoid(decay.astype(F32))
    o = _retention(proj, log_gamma, gn, b=b, l_tot=l_tot, lc=lc, d_model=d).reshape(m, 2 * d)
    tm = _row_tile(l_tot, 544)
    gate_blk = 2

    def prologue(a_ref, extra, scratch):
        @pl.when(pl.program_id(1) == 0)
        def _():
            g = extra[0][...].astype(F32)
            scratch[0][...] = ((g * _sigmoid(g)) * a_ref[...].astype(F32)).astype(BF16)
        return scratch[0][...]

    return _residual_matmul(
        o, w_out, jnp.zeros((d,), F32), x, cv, lv, l_tot=l_tot, lc=lc, r_gate=2,
        tm=tm, tn=512, name="ret_out_proj", prologue=prologue,
        pro_extras=[proj], pro_specs=[pl.BlockSpec((tm, 2 * d), lambda i, j: (i, gate_blk))],
        scratch_shapes=[pltpu.VMEM((tm, 2 * d), BF16)])


def _mla_in_proj(h, w_in_pad, qa_gain, kva_gain, *, l_tot):
    m = h.shape[0]
    n = w_in_pad.shape[1]
    tm = _row_tile(l_tot, 1088)
    r = MLA_RANK

    def epilogue(acc, extra, outs):
        qg_ref, kg_ref = extra
        ql = acc[:, :r]
        kl = acc[:, r:2 * r]
        outs[0][...] = (ql * lax.rsqrt(jnp.mean(ql * ql, axis=-1, keepdims=True) + EPS)
                        * qg_ref[...]).astype(BF16)
        outs[1][...] = (kl * lax.rsqrt(jnp.mean(kl * kl, axis=-1, keepdims=True) + EPS)
                        * kg_ref[...]).astype(BF16)
        outs[2][...] = acc[:, 2 * r:]

    gspec = pl.BlockSpec((1, r), lambda i, j: (0, 0))
    return _matmul(h, w_in_pad, [qa_gain.reshape(1, r), kva_gain.reshape(1, r)], [gspec, gspec],
                   [jax.ShapeDtypeStruct((m, r), BF16), jax.ShapeDtypeStruct((m, r), BF16),
                    jax.ShapeDtypeStruct((m, LANES), F32)],
                   [pl.BlockSpec((tm, r), lambda i, j: (i, 0)),
                    pl.BlockSpec((tm, r), lambda i, j: (i, 0)),
                    pl.BlockSpec((tm, LANES), lambda i, j: (i, 0))],
                   epilogue, tm=tm, tn=n, name="mla_in_proj")


def _mla_q_proj(qn, wq_pad, q_gain_pad, tabs, *, l_tot):
    m = qn.shape[0]
    n = wq_pad.shape[1]
    tm = _row_tile(l_tot, 1088)
    tn = 1024
    tpb = l_tot // tm
    hw = 2 * LANES
    width = MLA_NOPE + MLA_ROPE
    scale = width ** -0.5

    def epilogue(acc, extra, outs):
        g_ref, c_ref, s1_ref, s2_ref = extra
        c, s1, s2 = c_ref[...], s1_ref[...], s2_ref[...]
        for hh in range(tn // hw):
            xs = acc[:, hh * hw:(hh + 1) * hw]
            inv = lax.rsqrt(jnp.sum(xs * xs, axis=-1, keepdims=True) * (1.0 / width) + EPS)
            y = xs * inv * g_ref[...] * scale
            outs[0][:, hh * hw:hh * hw + LANES] = y[:, :LANES].astype(BF16)
            outs[0][:, hh * hw + LANES:(hh + 1) * hw] = _rope3(y[:, LANES:], c, s1, s2).astype(BF16)

    tspec = pl.BlockSpec((tm, LANES), lambda i, j: (lax.rem(i, tpb), 0))
    return _matmul(qn, wq_pad, [q_gain_pad] + list(tabs),
                   [pl.BlockSpec((1, hw), lambda i, j: (0, 0)), tspec, tspec, tspec],
                   [jax.ShapeDtypeStruct((m, n), BF16)],
                   [pl.BlockSpec((tm, tn), lambda i, j: (i, j))],
                   epilogue, tm=tm, tn=tn, name="mla_q_proj")[0]


def _mla_kv_proj(kvn, w_kv, kr, k_gain_pad, tabs, *, l_tot):
    m = kvn.shape[0]
    n = w_kv.shape[1]
    tm = _row_tile(l_tot, 1088)
    tn = 1024
    tpb = l_tot // tm
    hw = MLA_NOPE + MLA_V
    width = MLA_NOPE + MLA_ROPE

    def epilogue(acc, extra, outs):
        kr_ref, g_ref, c_ref, s1_ref, s2_ref = extra
        kr_raw = kr_ref[...]
        ss_r = jnp.sum(kr_raw * kr_raw, axis=-1, keepdims=True)
        kr_rot = _rope3(kr_raw * g_ref[:, LANES:], c_ref[...], s1_ref[...], s2_ref[...])
        g_nope = g_ref[:, :LANES]
        for hh in range(tn // hw):
            kn = acc[:, hh * hw:hh * hw + MLA_NOPE]
            inv = lax.rsqrt((jnp.sum(kn * kn, axis=-1, keepdims=True) + ss_r) * (1.0 / width) + EPS)
            outs[0][:, hh * hw:hh * hw + LANES] = (kn * inv * g_nope).astype(BF16)
            outs[0][:, hh * hw + LANES:(hh + 1) * hw] = (kr_rot * inv).astype(BF16)
            outs[1][:, hh * MLA_V:(hh + 1) * MLA_V] = acc[:, hh * hw + MLA_NOPE:(hh + 1) * hw].astype(BF16)

    tspec = pl.BlockSpec((tm, LANES), lambda i, j: (lax.rem(i, tpb), 0))
    return _matmul(kvn, w_kv, [kr, k_gain_pad] + list(tabs),
                   [pl.BlockSpec((tm, LANES), lambda i, j: (i, 0)),
                    pl.BlockSpec((1, 2 * LANES), lambda i, j: (0, 0)), tspec, tspec, tspec],
                   [jax.ShapeDtypeStruct((m, n), BF16), jax.ShapeDtypeStruct((m, n // 2), BF16)],
                   [pl.BlockSpec((tm, tn), lambda i, j: (i, j)),
                    pl.BlockSpec((tm, tn // 2), lambda i, j: (i, j))],
                   epilogue, tm=tm, tn=tn, name="mla_kv_proj")


def _softmax_pv(q, k, v):
    s = lax.dot_general(q, k, (((1,), (1,)), ((), ())), preferred_element_type=F32)
    p = jnp.exp(s - jnp.max(s, axis=-1, keepdims=True))
    den = jnp.sum(p, axis=-1, keepdims=True)
    return jnp.dot(p.astype(BF16), v, preferred_element_type=F32) / den


def _mla_attn_body(q_ref, k_ref, v_ref, o_ref, *, lc, tq, n_q):
    o_ref[0, 0:lc, :] = _softmax_pv(q_ref[0, 0:lc, :], k_ref[0, 0:lc, :], v_ref[0, 0:lc, :]).astype(BF16)

    def step(t, carry):
        rows = pl.ds(pl.multiple_of(lc + t * tq, tq), tq)
        o_ref[0, rows, :] = _softmax_pv(q_ref[0, rows, :], k_ref[0], v_ref[0]).astype(BF16)
        return carry

    lax.fori_loop(0, n_q, step, 0)


def _mla_attention(q, k, v, *, b, l_tot, lc):
    hw = 2 * LANES
    tq = 256
    assert lc % tq == 0 and (l_tot - lc) % tq == 0
    q3 = q.reshape(b, l_tot, MLA_HEADS * hw)
    k3 = k.reshape(b, l_tot, MLA_HEADS * hw)
    v3 = v.reshape(b, l_tot, MLA_HEADS * MLA_V)
    body = functools.partial(_mla_attn_body, lc=lc, tq=tq, n_q=(l_tot - lc) // tq)
    return pl.pallas_call(
        body,
        grid=(b, MLA_HEADS),
        in_specs=[pl.BlockSpec((1, l_tot, hw), lambda bi, hi: (bi, 0, hi)),
                  pl.BlockSpec((1, l_tot, hw), lambda bi, hi: (bi, 0, hi)),
                  pl.BlockSpec((1, l_tot, MLA_V), lambda bi, hi: (bi, 0, hi))],
        out_specs=pl.BlockSpec((1, l_tot, MLA_V), lambda bi, hi: (bi, 0, hi)),
        out_shape=jax.ShapeDtypeStruct((b, l_tot, MLA_HEADS * MLA_V), BF16),
        compiler_params=_params(2),
        name="mla_attention",
    )(q3, k3, v3)


def _mla_layer(x, h, cv, lv, w_in, qa_gain, kva_gain, w_q_b, w_kv_b, q_gain, k_gain, w_out, tabs,
               *, b, l_tot, lc):
    m, d = x.shape
    width = MLA_NOPE + MLA_ROPE
    hw = 2 * LANES
    w_in_pad = jnp.pad(w_in, ((0, 0), (0, 2 * MLA_RANK + LANES - w_in.shape[1])))
    wq_pad = jnp.pad(w_q_b.reshape(MLA_RANK, MLA_HEADS, width),
                     ((0, 0), (0, 0), (0, hw - width))).reshape(MLA_RANK, MLA_HEADS * hw)
    q_gain_pad = jnp.pad(q_gain, (0, hw - width)).reshape(1, hw)
    k_gain_pad = jnp.pad(k_gain, (0, hw - width)).reshape(1, hw)
    qn, kvn, kr = _mla_in_proj(h, w_in_pad, qa_gain, kva_gain, l_tot=l_tot)
    q = _mla_q_proj(qn, wq_pad, q_gain_pad, tabs, l_tot=l_tot)
    k, v = _mla_kv_proj(kvn, w_kv_b, kr, k_gain_pad, tabs, l_tot=l_tot)
    o = _mla_attention(q, k, v, b=b, l_tot=l_tot, lc=lc).reshape(m, MLA_HEADS * MLA_V)
    return _residual_matmul(o, w_out, jnp.zeros((d,), F32), x, cv, lv, l_tot=l_tot, lc=lc,
                            r_gate=2, tm=_row_tile(l_tot, 2176), tn=512, name="mla_out_proj")


def _conv_gate_body(bg_ref, cg_ref, u_ref, w_ref, cb_ref, o_ref, z_ref, *, l_tot, lc):
    pad = 8
    tn = o_ref.shape[-1]
    z_ref[0:pad, :] = jnp.zeros((pad, tn), F32)
    z_ref[pad + l_tot:, :] = jnp.zeros((pad, tn), F32)
    z_ref[pad:pad + l_tot, :] = cg_ref[0].astype(F32) * u_ref[0].astype(F32)
    rows = lax.broadcasted_iota(jnp.int32, (l_tot, 1), 0)
    z_prev = jnp.where(rows == lc, 0.0, z_ref[pad - 1:pad - 1 + l_tot, :])
    z_next = jnp.where(rows == lc - 1, 0.0, z_ref[pad + 1:pad + 1 + l_tot, :])
    z_mid = z_ref[pad:pad + l_tot, :]
    conv = z_prev * w_ref[0:1, :] + z_mid * w_ref[1:2, :] + z_next * w_ref[2:3, :] + cb_ref[...]
    o_ref[0] = (bg_ref[0].astype(F32) * conv).astype(BF16)


def _conv_gate(proj, conv_w, conv_b, *, b, l_tot, lc, d):
    tn = 256
    nb = d // tn
    proj3 = proj.reshape(b, l_tot, 3 * d)
    body = functools.partial(_conv_gate_body, l_tot=l_tot, lc=lc)
    return pl.pallas_call(
        body,
        grid=(b, nb),
        in_specs=[pl.BlockSpec((1, l_tot, tn), lambda bi, j: (bi, 0, j)),
                  pl.BlockSpec((1, l_tot, tn), lambda bi, j: (bi, 0, nb + j)),
                  pl.BlockSpec((1, l_tot, tn), lambda bi, j: (bi, 0, 2 * nb + j)),
                  pl.BlockSpec((3, tn), lambda bi, j: (0, j)),
                  pl.BlockSpec((1, tn), lambda bi, j: (0, j))],
        out_specs=pl.BlockSpec((1, l_tot, tn), lambda bi, j: (bi, 0, j)),
        out_shape=jax.ShapeDtypeStruct((b, l_tot, d), BF16),
        scratch_shapes=[pltpu.VMEM((l_tot + 16, tn), F32)],
        compiler_params=_params(2),
        name="conv_gate",
    )(proj3, proj3, proj3, conv_w, conv_b.reshape(1, d))


def _shortconv_layer(x, h, cv, lv, w_in, conv_w, conv_b, w_out, *, b, l_tot, lc):
    m, d = x.shape
    tm = _row_tile(l_tot, 2176)
    proj = _matmul(h, w_in, [], [], [jax.ShapeDtypeStruct((m, 3 * d), BF16)],
                   [pl.BlockSpec((tm, 512), lambda i, j: (i, j))],
                   _plain_epilogue(BF16), tm=tm, tn=512, name="sc_in_proj")[0]
    a = _conv_gate(proj, conv_w, conv_b, b=b, l_tot=l_tot, lc=lc, d=d).reshape(m, d)
    return _residual_matmul(a, w_out, jnp.zeros((d,), F32), x, cv, lv, l_tot=l_tot, lc=lc,
                            r_gate=2, tm=tm, tn=512, name="sc_out_proj")


def _split_hi_lo(x):
    hi = x.astype(BF16)
    lo = (x - hi.astype(F32)).astype(BF16)
    return hi, lo


def _head_rms_scale(x, seg_ref, segt_ref):
    hi, lo = _split_hi_lo(x * x)
    seg = seg_ref[...]
    ss = jnp.dot(hi, seg, preferred_element_type=F32) + jnp.dot(lo, seg, preferred_element_type=F32)
    inv = lax.rsqrt(ss * (1.0 / GQA_HEAD_DIM) + EPS)
    ihi, ilo = _split_hi_lo(inv)
    segt = segt_ref[...]
    return jnp.dot(ihi, segt, preferred_element_type=F32) + jnp.dot(ilo, segt, preferred_element_type=F32)


def _gqa_in_proj(h, w_in, b_in, qg_t, kg_t, seg, segt, tabs, *, l_tot):
    m = h.shape[0]
    n = w_in.shape[1]
    tm = _row_tile(l_tot, 1088)
    tn = 512
    tpb = l_tot // tm
    n_q_tiles = GQA_HEADS * GQA_HEAD_DIM // tn
    kv_w = GQA_KV_HEADS * GQA_HEAD_DIM
    assert n == n_q_tiles * tn + 2 * kv_w and 2 * kv_w == tn
    scale = GQA_HEAD_DIM ** -0.5

    def norm_rope(xs, g, extra, mult):
        seg_ref, segt_ref, c_ref, s1_ref, s2_ref = extra[3:8]
        y = xs * _head_rms_scale(xs, seg_ref, segt_ref) * g
        c, s1, s2 = c_ref[...], s1_ref[...], s2_ref[...]
        return [_rope3(y[:, a * LANES:(a + 1) * LANES], c, s1, s2) * mult
                for a in range(xs.shape[1] // LANES)]

    def dup_heads(slab, lo_mask):
        swapped = pltpu.roll(slab, GQA_HEAD_DIM, axis=1)
        return jnp.where(lo_mask, slab, swapped), jnp.where(lo_mask, swapped, slab)

    def epilogue(acc, extra, outs):
        b_ref, qg_ref, kg_ref = extra[0:3]
        j = pl.program_id(1)
        acc = acc + b_ref[...]

        @pl.when(j < n_q_tiles)
        def _():
            for a, slab in enumerate(norm_rope(acc, qg_ref[...], extra, scale)):
                outs[0][:, a * LANES:(a + 1) * LANES] = slab.astype(BF16)

        @pl.when(j == n_q_tiles)
        def _():
            lo_mask = lax.broadcasted_iota(jnp.int32, (tm, LANES), 1) < GQA_HEAD_DIM
            kpad = jnp.concatenate([acc[:, :kv_w], acc[:, :kv_w]], axis=1)
            k_slabs = norm_rope(kpad, kg_ref[...], extra, 1.0)[:kv_w // LANES]
            for a in range(kv_w // LANES):
                ka, kb = dup_heads(k_slabs[a], lo_mask)
                outs[1][:, (2 * a) * LANES:(2 * a + 1) * LANES] = ka.astype(BF16)
                outs[1][:, (2 * a + 1) * LANES:(2 * a + 2) * LANES] = kb.astype(BF16)
                va, vb = dup_heads(acc[:, kv_w + a * LANES:kv_w + (a + 1) * LANES], lo_mask)
                outs[2][:, (2 * a) * LANES:(2 * a + 1) * LANES] = va.astype(BF16)
                outs[2][:, (2 * a + 1) * LANES:(2 * a + 2) * LANES] = vb.astype(BF16)

    tspec = pl.BlockSpec((tm, LANES), lambda i, j: (lax.rem(i, tpb), 0))
    full = lambda shape: pl.BlockSpec(shape, lambda i, j: (0, 0))
    return _matmul(
        h, w_in, [b_in.reshape(1, n), qg_t, kg_t, seg, segt] + list(tabs),
        [pl.BlockSpec((1, tn), lambda i, j: (0, j)), full((1, tn)), full((1, tn)),
         full((tn, LANES)), full((LANES, tn)), tspec, tspec, tspec],
        [jax.ShapeDtypeStruct((m, n_q_tiles * tn), BF16),
         jax.ShapeDtypeStruct((m, GQA_KV_HEADS * LANES), BF16),
         jax.ShapeDtypeStruct((m, GQA_KV_HEADS * LANES), BF16)],
        [pl.BlockSpec((tm, tn), lambda i, j: (i, jnp.minimum(j, n_q_tiles - 1))),
         pl.BlockSpec((tm, GQA_KV_HEADS * LANES), lambda i, j: (i, 0)),
         pl.BlockSpec((tm, GQA_KV_HEADS * LANES), lambda i, j: (i, 0))],
        epilogue, tm=tm, tn=tn, name="gqa_in_proj")


def _gqa_attn_body(sink_ref, q_ref, k_ref, v_ref, o_ref, *, lc, nb):
    g = pl.program_id(1)
    r = WINDOW
    n_slab = GQA_GROUP // 2
    rows_all = GQA_GROUP * r
    row_head = lax.broadcasted_iota(jnp.int32, (rows_all, 1), 0) // r
    sink = jnp.zeros((rows_all, 1), F32)
    for hh in range(GQA_GROUP):
        sink = jnp.where(row_head == hh, sink_ref[g * GQA_GROUP + hh], sink)
    lo_mask = lax.broadcasted_iota(jnp.int32, (r, LANES), 1) < GQA_HEAD_DIM
    ii = lax.rem(lax.broadcasted_iota(jnp.int32, (rows_all, r), 0), r)
    jj = lax.broadcasted_iota(jnp.int32, (rows_all, r), 1)
    k_ctx = k_ref[0, 0:lc, :]
    v_ctx = v_ref[0, 0:lc, :]

    def stack_q(rows):
        parts = []
        for a in range(n_slab):
            qa = q_ref[0, rows, a * LANES:(a + 1) * LANES]
            zero = jnp.zeros_like(qa)
            parts += [jnp.where(lo_mask, qa, zero), jnp.where(lo_mask, zero, qa)]
        return jnp.concatenate(parts, axis=0)

    def attend(rows, blocks):
        qs = stack_q(rows)
        scores = []
        for kb, _, mask in blocks:
            s = lax.dot_general(qs, kb, (((1,), (1,)), ((), ())), preferred_element_type=F32)
            scores.append(s if mask is None else jnp.where(mask, s, NEG_INF))
        mx = sink
        for s in scores:
            mx = jnp.maximum(mx, jnp.max(s, axis=-1, keepdims=True))
        den = jnp.exp(sink - mx)
        acc = jnp.zeros((rows_all, LANES), F32)
        for s, (_, vb, _) in zip(scores, blocks):
            p = jnp.exp(s - mx)
            den = den + jnp.sum(p, axis=-1, keepdims=True)
            acc = acc + jnp.dot(p.astype(BF16), vb, preferred_element_type=F32)
        o = acc / den
        for a in range(n_slab):
            oa = jnp.where(lo_mask, o[(2 * a) * r:(2 * a + 1) * r], o[(2 * a + 1) * r:(2 * a + 2) * r])
            o_ref[0, rows, a * LANES:(a + 1) * LANES] = oa.astype(BF16)

    for cb in range(lc // r):
        attend(pl.ds(cb * r, r), [(k_ctx, v_ctx, None)])

    def step(t, carry):
        r_cur = pl.multiple_of(lc + t * r, r)
        r_prev = pl.multiple_of(lc + jnp.maximum(t - 1, 0) * r, r)
        r_next = pl.multiple_of(lc + jnp.minimum(t + 1, nb - 1) * r, r)
        blk = lambda r0: (k_ref[0, pl.ds(r0, r), :], v_ref[0, pl.ds(r0, r), :])
        m_prev = jnp.logical_and(jj >= ii, t > 0)
        m_next = jnp.logical_and(jj <= ii, t < nb - 1)
        attend(pl.ds(r_cur, r), [blk(r_prev) + (m_prev,), blk(r_cur) + (None,),
                                 blk(r_next) + (m_next,), (k_ctx, v_ctx, None)])
        return carry

    lax.fori_loop(0, nb, step, 0)


def _gqa_attention(q, k, v, sink, *, b, l_tot, lc):
    gw = GQA_GROUP * GQA_HEAD_DIM
    q3 = q.reshape(b, l_tot, GQA_HEADS * GQA_HEAD_DIM)
    k3 = k.reshape(b, l_tot, GQA_KV_HEADS * LANES)
    v3 = v.reshape(b, l_tot, GQA_KV_HEADS * LANES)
    body = functools.partial(_gqa_attn_body, lc=lc, nb=(l_tot - lc) // WINDOW)
    return pl.pallas_call(
        body,
        grid=(b, GQA_KV_HEADS),
        in_specs=[pl.BlockSpec(memory_space=pltpu.SMEM),
                  pl.BlockSpec((1, l_tot, gw), lambda bi, gi: (bi, 0, gi)),
                  pl.BlockSpec((1, l_tot, LANES), lambda bi, gi: (bi, 0, gi)),
                  pl.BlockSpec((1, l_tot, LANES), lambda bi, gi: (bi, 0, gi))],
        out_specs=pl.BlockSpec((1, l_tot, gw), lambda bi, gi: (bi, 0, gi)),
        out_shape=jax.ShapeDtypeStruct((b, l_tot, GQA_HEADS * GQA_HEAD_DIM), BF16),
        compiler_params=_params(2),
        name="gqa_attention",
    )(sink, q3, k3, v3)


def _gqa_layer(x, h, cv, lv, w_in, b_in, q_gain, k_gain, sink, w_out, b_out, tabs, *, b, l_tot, lc):
    m, d = x.shape
    tn = 512
    heads_per_tile = tn // GQA_HEAD_DIM
    col = jnp.arange(tn)
    seg = (col[:, None] // GQA_HEAD_DIM == jnp.arange(LANES)[None, :]).astype(BF16)
    qg_t = jnp.tile(q_gain, heads_per_tile).reshape(1, tn)
    kg_t = jnp.tile(k_gain, heads_per_tile).reshape(1, tn)
    q, k, v = _gqa_in_proj(h, w_in, b_in, qg_t, kg_t, seg, seg.T, tabs, l_tot=l_tot)
    o = _gqa_attention(q, k, v, sink.astype(F32), b=b, l_tot=l_tot, lc=lc).reshape(m, d)
    return _residual_matmul(o, w_out, b_out, x, cv, lv, l_tot=l_tot, lc=lc, r_gate=2,
                            tm=_row_tile(l_tot, 2176), tn=512, name="gqa_out_proj")


def _moe_gu_body(be_ref, nu_ref, x_ref, wg_ref, wl_ref, bg_ref, bl_ref, o_ref, wgb_ref, wlb_ref):
    blk = pl.program_id(0)
    prev = be_ref[jnp.maximum(blk - 1, 0)]

    @pl.when(jnp.logical_or(blk == 0, be_ref[blk] != prev))
    def _():
        wgb_ref[...] = wg_ref[0].astype(BF16)
        wlb_ref[...] = wl_ref[0].astype(BF16)

    @pl.when(blk < nu_ref[0])
    def _():
        xb = x_ref[...]
        h_glu = jnp.dot(xb, wgb_ref[...], preferred_element_type=F32) + bg_ref[0]
        h_lin = jnp.dot(xb, wlb_ref[...], preferred_element_type=F32) + bl_ref[0]
        h_glu = jnp.minimum(h_glu, SWIGLU_LIMIT)
        h_lin = jnp.clip(h_lin, -SWIGLU_LIMIT, SWIGLU_LIMIT)
        o_ref[...] = (h_glu * _sigmoid(SWIGLU_ALPHA * h_glu) * (h_lin + 1.0)).astype(BF16)

    @pl.when(blk >= nu_ref[0])
    def _():
        o_ref[...] = jnp.zeros_like(o_ref)


def _moe_down_body(be_ref, nu_ref, a_ref, w_ref, b_ref, g_ref, o_ref, wb_ref):
    blk = pl.program_id(0)
    prev = be_ref[jnp.maximum(blk - 1, 0)]

    @pl.when(jnp.logical_or(blk == 0, be_ref[blk] != prev))
    def _():
        wb_ref[...] = w_ref[0].astype(BF16)

    @pl.when(blk < nu_ref[0])
    def _():
        y = jnp.dot(a_ref[...], wb_ref[...], preferred_element_type=F32) + b_ref[0]
        o_ref[...] = y * g_ref[...]

    @pl.when(blk >= nu_ref[0])
    def _():
        o_ref[...] = jnp.zeros_like(o_ref)


def _moe_experts(x_rows, row_gate, block_expert, n_used, w_gu, b_gu, w_down, b_down):
    n_rows, d = x_rows.shape
    n_blocks = n_rows // MOE_TM
    ff = w_down.shape[1]
    n_e = w_gu.shape[0]
    act = pl.pallas_call(
        _moe_gu_body,
        grid_spec=pltpu.PrefetchScalarGridSpec(
            num_scalar_prefetch=2,
            grid=(n_blocks,),
            in_specs=[pl.BlockSpec((MOE_TM, d), lambda i, be, nu: (i, 0)),
                      pl.BlockSpec((1, d, ff), lambda i, be, nu: (be[i], 0, 0)),
                      pl.BlockSpec((1, d, ff), lambda i, be, nu: (be[i], 0, 1)),
                      pl.BlockSpec((1, 1, ff), lambda i, be, nu: (be[i], 0, 0)),
                      pl.BlockSpec((1, 1, ff), lambda i, be, nu: (be[i], 0, 1))],
            out_specs=pl.BlockSpec((MOE_TM, ff), lambda i, be, nu: (i, 0)),
            scratch_shapes=[pltpu.VMEM((d, ff), BF16), pltpu.VMEM((d, ff), BF16)]),
        out_shape=jax.ShapeDtypeStruct((n_rows, ff), BF16),
        compiler_params=_params(1),
        name="moe_gate_up",
    )(block_expert, n_used, x_rows, w_gu, w_gu, b_gu.reshape(n_e, 1, 2 * ff), b_gu.reshape(n_e, 1, 2 * ff))
    return pl.pallas_call(
        _moe_down_body,
        grid_spec=pltpu.PrefetchScalarGridSpec(
            num_scalar_prefetch=2,
            grid=(n_blocks,),
            in_specs=[pl.BlockSpec((MOE_TM, ff), lambda i, be, nu: (i, 0)),
                      pl.BlockSpec((1, ff, d), lambda i, be, nu: (be[i], 0, 0)),
                      pl.BlockSpec((1, 1, d), lambda i, be, nu: (be[i], 0, 0)),
                      pl.BlockSpec((MOE_TM, 1), lambda i, be, nu: (i, 0))],
            out_specs=pl.BlockSpec((MOE_TM, d), lambda i, be, nu: (i, 0)),
            scratch_shapes=[pltpu.VMEM((ff, d), BF16)]),
        out_shape=jax.ShapeDtypeStruct((n_rows, d), F32),
        compiler_params=_params(1),
        name="moe_down",
    )(block_expert, n_used, act, w_down, b_down.reshape(n_e, 1, d), row_gate)


def _moe_layer(x, f, cv, lv, valid, router_w, router_b, w_gu, b_gu, w_down, b_down, *, l_tot, lc):
    m, d = x.shape
    n_e = router_w.shape[1]
    tm = _row_tile(l_tot, 2176)
    rw = jnp.pad(router_w, ((0, 0), (0, LANES - n_e)))
    rb = jnp.pad(router_b, (0, LANES - n_e)).reshape(1, LANES)
    logits = _matmul(f, rw, [rb], [pl.BlockSpec((1, LANES), lambda i, j: (0, 0))],
                     [jax.ShapeDtypeStruct((m, LANES), F32)],
                     [pl.BlockSpec((tm, LANES), lambda i, j: (i, 0))],
                     _bias_epilogue(F32), tm=tm, tn=LANES, name="moe_router")[0][:, :n_e]
    top_val, top_idx = lax.top_k(logits, TOP_K)
    gates = jax.nn.softmax(top_val, axis=-1)
    n_assign = m * TOP_K
    expert = top_idx.reshape(n_assign)
    keep = jnp.repeat(valid, TOP_K)
    onehot = jnp.logical_and(expert[:, None] == jnp.arange(n_e)[None, :], keep[:, None]).astype(jnp.int32)
    counts = jnp.sum(onehot, axis=0)
    rank = jnp.sum((jnp.cumsum(onehot, axis=0) - onehot) * onehot, axis=1)
    padded = (counts + MOE_TM - 1) // MOE_TM * MOE_TM
    pend = jnp.cumsum(padded)
    pstart = pend - padded
    n_blocks = -(-n_assign // MOE_TM) + n_e
    n_rows = n_blocks * MOE_TM
    dest = jnp.where(keep, pstart[expert] + rank, n_rows)
    token = jnp.arange(n_assign, dtype=jnp.int32) // TOP_K
    src = jnp.zeros((n_rows,), jnp.int32).at[dest].set(token, mode="drop")
    row_gate = jnp.zeros((n_rows,), F32).at[dest].set(gates.reshape(n_assign), mode="drop")
    block_start = jnp.arange(n_blocks, dtype=jnp.int32) * MOE_TM
    block_expert = jnp.minimum(jnp.sum(block_start[:, None] >= pend[None, :], axis=1), n_e - 1).astype(jnp.int32)
    n_used = (pend[-1] // MOE_TM).astype(jnp.int32).reshape(1)
    x_rows = jnp.take(f, src, axis=0)
    y_rows = _moe_experts(x_rows, row_gate.reshape(n_rows, 1), block_expert, n_used,
                          w_gu, b_gu, w_down, b_down)
    y = jnp.sum(jnp.take(y_rows, jnp.minimum(dest, n_rows - 1).reshape(m, TOP_K), axis=0), axis=1)
    is_ctx = (jnp.arange(m) % l_tot < lc)[:, None]
    gate = jnp.where(is_ctx, cv[5][None, :], jnp.repeat(lv[:, 5, :], l_tot, axis=0))
    return x + gate * y


def _rope_1d_tables(l_tot, half):
    inv = ROPE_BASE ** (-jnp.arange(half, dtype=F32) / half)
    ang = jnp.arange(l_tot, dtype=F32)[:, None] * inv
    return jnp.cos(ang), jnp.sin(ang)


def _axial_tables(l_tot, lc):
    n_freq = GQA_HEAD_DIM // 4
    t = jnp.maximum(jnp.arange(l_tot) - lc, 0)
    pos = jnp.stack([(t // GRID_W).astype(F32), (t % GRID_W).astype(F32)], axis=-1)
    inv_freq = ROPE_BASE ** (-jnp.arange(n_freq, dtype=F32) / n_freq)
    ang = pos[:, :, None] * inv_freq
    lane = jnp.arange(LANES)
    axis = (lane % GQA_HEAD_DIM) // (2 * n_freq)
    freq = lane % n_freq
    first_half = (lane % (2 * n_freq)) < n_freq
    is_lat = (jnp.arange(l_tot) >= lc)[:, None]
    cos = jnp.where(is_lat, jnp.cos(ang)[:, axis, freq], 1.0)
    sin = jnp.where(is_lat, jnp.sin(ang)[:, axis, freq], 0.0)
    return cos, jnp.where(first_half, -sin, 0.0), jnp.where(first_half, 0.0, sin)


def kernel(x, c, ctx, c_ctx, norm_mix, norm_ffn, mod_w, mod_b, ret_w_in, ret_decay, ret_gn, ret_w_out, mla_w_in, mla_q_a_norm, mla_kv_a_norm, mla_w_q_b, mla_w_kv_b, mla_q_norm, mla_k_norm, mla_w_out, sc_w_in, sc_conv_w, sc_conv_b, sc_w_out, gqa_w_in, gqa_b_in, gqa_q_norm, gqa_k_norm, gqa_sink, gqa_w_out, gqa_b_out, moe_router_w, moe_router_b, moe_w_gu, moe_b_gu, moe_w_down, moe_b_down):
    b, n_lat, d = x.shape
    lc = ctx.shape[1]
    l_tot = lc + n_lat
    m = b * l_tot
    depth = mod_w.shape[0]
    assert b + 1 <= MOD_ROWS and lc % 256 == 0 and n_lat % 256 == 0

    xs = jnp.concatenate([ctx, x], axis=1).reshape(m, d)
    cvec = jnp.zeros((MOD_ROWS, d), F32).at[:b].set(c).at[b].set(c_ctx)
    mod = _modulation(cvec, mod_w, mod_b).reshape(depth, MOD_ROWS, 6, d)
    pad_rows = ((0, 0), (0, MOD_ROWS - 6), (0, 0))
    ret_cos, ret_sin = _rope_1d_tables(l_tot, d // RET_HEADS // 2)
    tabs = _axial_tables(l_tot, lc)
    all_rows = jnp.ones((m,), bool)
    lat_rows = jnp.arange(m) % l_tot >= lc

    for i in range(depth):
        kind, j = i % N_MIXERS, i // N_MIXERS
        cv = jnp.pad(mod[i, b], pad_rows[1:])
        lv = jnp.pad(mod[i, :b], pad_rows)
        lay = dict(b=b, l_tot=l_tot, lc=lc)
        h = _norm_mod(xs, norm_mix[i], cv, lv, l_tot=l_tot, lc=lc, r_shift=0, r_scale=1)
        if kind == 0:
            xs = _retention_layer(xs, h, cv, lv, ret_w_in[j], ret_decay[j], ret_gn[j], ret_w_out[j],
                                  ret_cos, ret_sin, **lay)
        elif kind == 1:
            xs = _mla_layer(xs, h, cv, lv, mla_w_in[j], mla_q_a_norm[j], mla_kv_a_norm[j], mla_w_q_b[j],
                            mla_w_kv_b[j], mla_q_norm[j], mla_k_norm[j], mla_w_out[j], tabs, **lay)
        elif kind == 2:
            xs = _shortconv_layer(xs, h, cv, lv, sc_w_in[j], sc_conv_w[j], sc_conv_b[j], sc_w_out[j], **lay)
        else:
            xs = _gqa_layer(xs, h, cv, lv, gqa_w_in[j], gqa_b_in[j], gqa_q_norm[j], gqa_k_norm[j],
                            gqa_sink[j], gqa_w_out[j], gqa_b_out[j], tabs, **lay)
        f = _norm_mod(xs, norm_ffn[i], cv, lv, l_tot=l_tot, lc=lc, r_shift=3, r_scale=4)
        valid = all_rows if i < depth - 1 else lat_rows
        xs = _moe_layer(xs, f, cv, lv, valid, moe_router_w[i], moe_router_b[i], moe_w_gu[i], moe_b_gu[i],
                        moe_w_down[i], moe_b_down[i], l_tot=l_tot, lc=lc)
    return xs.reshape(b, l_tot, d)[:, lc:, :]
```

```python
import functools

import jax
import jax.numpy as jnp
from jax import lax
from jax.experimental import pallas as pl
from jax.experimental.pallas import tpu as pltpu

F32 = jnp.float32
BF16 = jnp.bfloat16

EPS = 1e-6
ROPE_BASE = 10000.0
NEG_INF = -1e30
GRID_W = 64

RET_HEADS = 8
RET_CHUNK = 128
MLA_HEADS = 16
MLA_NOPE = 128
MLA_ROPE = 64
MLA_V = 128
MLA_RANK = 512
GQA_HEADS = 32
GQA_KV_HEADS = 4
GQA_GROUP = GQA_HEADS // GQA_KV_HEADS
GQA_HEAD_DIM = 64
WINDOW = 128
N_EXPERTS = 32
TOP_K = 4
SWIGLU_LIMIT = 7.0
SWIGLU_ALPHA = 1.702
N_MIXERS = 4

LANES = 128
VMEM_LIMIT = 56 * 1024 * 1024
MOE_TM = 256
MOD_ROWS = 8


def _params(n_axes, vmem=VMEM_LIMIT):
    return pltpu.CompilerParams(dimension_semantics=("arbitrary",) * n_axes,
                                vmem_limit_bytes=vmem)


def _row_tile(l_tot, cap):
    best = 16
    for t in range(16, cap + 1, 16):
        if l_tot % t == 0:
            best = t
    return best


def _is_ctx_rows(i, tm, tpb, lc):
    rows = lax.rem(i, tpb) * tm + lax.broadcasted_iota(jnp.int32, (tm, 1), 0)
    return rows < lc


def _pick_mod(is_ctx, cv_ref, lv_ref, r):
    return jnp.where(is_ctx, cv_ref[r:r + 1, :], lv_ref[0, r:r + 1, :])


def _sigmoid(x):
    return 1.0 / (1.0 + jnp.exp(-x))


def _rope3(x, c, s1, s2):
    return x * c + pltpu.roll(x, LANES - 16, axis=1) * s1 + pltpu.roll(x, 16, axis=1) * s2


def _mod_body(c_ref, w_ref, b_ref, o_ref):
    c = c_ref[...]
    s = (c * _sigmoid(c)).astype(BF16)
    o_ref[0] = jnp.dot(s, w_ref[0].astype(BF16), preferred_element_type=F32) + b_ref[0]


def _modulation(cvec, mod_w, mod_b):
    depth, d, n = mod_w.shape
    tn = 2048
    return pl.pallas_call(
        _mod_body,
        grid=(depth, n // tn),
        in_specs=[pl.BlockSpec((MOD_ROWS, d), lambda l, j: (0, 0)),
                  pl.BlockSpec((1, d, tn), lambda l, j: (l, 0, j)),
                  pl.BlockSpec((1, 1, tn), lambda l, j: (l, 0, j))],
        out_specs=pl.BlockSpec((1, MOD_ROWS, tn), lambda l, j: (l, 0, j)),
        out_shape=jax.ShapeDtypeStruct((depth, MOD_ROWS, n), F32),
        compiler_params=_params(2),
        name="modulation",
    )(cvec, mod_w, mod_b.reshape(depth, 1, n))


def _norm_mod_body(x_ref, g_ref, cv_ref, lv_ref, o_ref, *, tm, tpb, lc, r_shift, r_scale):
    x = x_ref[...]
    y = x * lax.rsqrt(jnp.mean(x * x, axis=-1, keepdims=True) + EPS) * g_ref[...]
    is_ctx = _is_ctx_rows(pl.program_id(0), tm, tpb, lc)
    shift = _pick_mod(is_ctx, cv_ref, lv_ref, r_shift)
    scale = _pick_mod(is_ctx, cv_ref, lv_ref, r_scale)
    o_ref[...] = (y * (1.0 + scale) + shift).astype(BF16)


def _norm_mod(x, gain, cv, lv, *, l_tot, lc, r_shift, r_scale):
    m, d = x.shape
    tm = _row_tile(l_tot, 544)
    tpb = l_tot // tm
    body = functools.partial(_norm_mod_body, tm=tm, tpb=tpb, lc=lc, r_shift=r_shift, r_scale=r_scale)
    return pl.pallas_call(
        body,
        grid=(m // tm,),
        in_specs=[pl.BlockSpec((tm, d), lambda i: (i, 0)),
                  pl.BlockSpec((1, d), lambda i: (0, 0)),
                  pl.BlockSpec((MOD_ROWS, d), lambda i: (0, 0)),
                  pl.BlockSpec((1, MOD_ROWS, d), lambda i: (i // tpb, 0, 0))],
        out_specs=pl.BlockSpec((tm, d), lambda i: (i, 0)),
        out_shape=jax.ShapeDtypeStruct((m, d), BF16),
        compiler_params=_params(1),
        name="norm_mod",
    )(x, gain.reshape(1, d), cv, lv)


def _mm_body(*refs, n_extra, n_out, epilogue, prologue):
    a_ref, w_ref = refs[0], refs[1]
    extra = refs[2:2 + n_extra]
    outs = refs[2 + n_extra:2 + n_extra + n_out]
    scratch = refs[2 + n_extra + n_out:]
    a = a_ref[...] if prologue is None else prologue(a_ref, extra, scratch)
    acc = jnp.dot(a, w_ref[...].astype(BF16), preferred_element_type=F32)
    epilogue(acc, extra, outs)


def _matmul(a, w, extras, extra_specs, out_shapes, out_specs, epilogue, *, tm, tn, name,
            prologue=None, scratch_shapes=()):
    m, k = a.shape
    n = w.shape[1]
    body = functools.partial(_mm_body, n_extra=len(extras), n_out=len(out_shapes),
                             epilogue=epilogue, prologue=prologue)
    return pl.pallas_call(
        body,
        grid=(m // tm, n // tn),
        in_specs=[pl.BlockSpec((tm, k), lambda i, j: (i, 0)),
                  pl.BlockSpec((k, tn), lambda i, j: (0, j))] + list(extra_specs),
        out_specs=list(out_specs),
        out_shape=list(out_shapes),
        scratch_shapes=list(scratch_shapes),
        compiler_params=_params(2),
        name=name,
    )(a, w, *extras)


def _plain_epilogue(out_dtype):
    def epilogue(acc, extra, outs):
        outs[0][...] = acc.astype(out_dtype)
    return epilogue


def _bias_epilogue(out_dtype):
    def epilogue(acc, extra, outs):
        outs[0][...] = (acc + extra[0][...]).astype(out_dtype)
    return epilogue


def _residual_matmul(a, w, bias, x_res, cv, lv, *, l_tot, lc, r_gate, tm, tn, name,
                     prologue=None, pro_extras=(), pro_specs=(), scratch_shapes=()):
    m = a.shape[0]
    n = w.shape[1]
    tpb = l_tot // tm
    n_pro = len(pro_extras)

    def epilogue(acc, extra, outs):
        b_ref, x_ref, cv_ref, lv_ref = extra[n_pro:n_pro + 4]
        is_ctx = _is_ctx_rows(pl.program_id(0), tm, tpb, lc)
        gate = _pick_mod(is_ctx, cv_ref, lv_ref, r_gate)
        outs[0][...] = x_ref[...] + gate * (acc + b_ref[...])

    extras = list(pro_extras) + [bias.reshape(1, n), x_res, cv, lv]
    specs = list(pro_specs) + [
        pl.BlockSpec((1, tn), lambda i, j: (0, j)),
        pl.BlockSpec((tm, tn), lambda i, j: (i, j)),
        pl.BlockSpec((MOD_ROWS, tn), lambda i, j: (0, j)),
        pl.BlockSpec((1, MOD_ROWS, tn), lambda i, j: (i // tpb, 0, j))]
    return _matmul(a, w, extras, specs,
                   [jax.ShapeDtypeStruct((m, n), F32)],
                   [pl.BlockSpec((tm, tn), lambda i, j: (i, j))],
                   epilogue, tm=tm, tn=tn, name=name, prologue=prologue,
                   scratch_shapes=scratch_shapes)[0]


def _ret_in_proj(h, w_in, cos, sin, *, l_tot, d_model):
    m = h.shape[0]
    n = w_in.shape[1]
    tm = _row_tile(l_tot, 2176)
    tn = 512
    tpb = l_tot // tm
    dk = d_model // RET_HEADS
    half = dk // 2
    n_q_tiles = d_model // tn
    k_scale = dk ** -0.5

    def epilogue(acc, extra, outs):
        cos_ref, sin_ref = extra
        j = pl.program_id(1)

        @pl.when(j < 2 * n_q_tiles)
        def _():
            c = cos_ref[...]
            s = sin_ref[...]
            mult = jnp.where(j >= n_q_tiles, k_scale, 1.0)
            for hh in range(tn // dk):
                x1 = acc[:, hh * dk:hh * dk + half]
                x2 = acc[:, hh * dk + half:(hh + 1) * dk]
                outs[0][:, hh * dk:hh * dk + half] = ((x1 * c - x2 * s) * mult).astype(BF16)
                outs[0][:, hh * dk + half:(hh + 1) * dk] = ((x2 * c + x1 * s) * mult).astype(BF16)

        @pl.when(j >= 2 * n_q_tiles)
        def _():
            outs[0][...] = acc.astype(BF16)

    tab_spec = pl.BlockSpec((tm, half), lambda i, j: (lax.rem(i, tpb), 0))
    return _matmul(h, w_in, [cos, sin], [tab_spec, tab_spec],
                   [jax.ShapeDtypeStruct((m, n), BF16)],
                   [pl.BlockSpec((tm, tn), lambda i, j: (i, j))],
                   epilogue, tm=tm, tn=tn, name="ret_in_proj")[0]


def _ret_body(lg_ref, q_ref, k_ref, v_ref, gn_ref, o_ref, oacc_ref, s_ref, *, nc_ctx, nc_tot):
    h = pl.program_id(1)
    ch = RET_CHUNK
    dv = v_ref.shape[-1]
    ii = lax.broadcasted_iota(jnp.int32, (ch, ch), 0).astype(F32)
    jj = lax.broadcasted_iota(jnp.int32, (ch, ch), 1).astype(F32)
    ri = lax.broadcasted_iota(jnp.int32, (ch, 1), 0).astype(F32)
    for direction in (0, 1):
        lg = lg_ref[direction, h]
        if direction == 0:
            diff = ii - jj
            q_decay = jnp.exp((ri + 1.0) * lg)
            k_decay = jnp.exp((ch - 1.0 - ri) * lg)
        else:
            diff = jj - ii
            q_decay = jnp.exp((ch - ri) * lg)
            k_decay = jnp.exp(ri * lg)
        intra = jnp.where(diff >= 0, jnp.exp(jnp.maximum(diff, 0.0) * lg), 0.0)
        chunk_decay = jnp.exp(jnp.full((1, dv), float(ch), F32) * lg)
        s_ref[...] = jnp.zeros_like(s_ref)

        def step(t, carry, direction=direction, intra=intra, q_decay=q_decay,
                 k_decay=k_decay, chunk_decay=chunk_decay):
            if direction == 0:
                c = t
            else:
                c = jnp.where(t < nc_ctx, nc_ctx - 1 - t, nc_tot - 1 - (t - nc_ctx))
            rows = pl.ds(pl.multiple_of(c * ch, ch), ch)
            qc = q_ref[0, rows, :]
            kc = k_ref[0, rows, :]
            vc = v_ref[0, rows, :]
            scores = lax.dot_general(qc, kc, (((1,), (1,)), ((), ())),
                                     preferred_element_type=F32) * intra
            state = s_ref[...]
            out = (jnp.dot(scores.astype(BF16), vc, preferred_element_type=F32)
                   + jnp.dot((qc.astype(F32) * q_decay).astype(BF16), state.astype(BF16),
                             preferred_element_type=F32))
            kt = (kc.astype(F32) * k_decay).T.astype(BF16)
            s_ref[...] = state * chunk_decay + jnp.dot(kt, vc, preferred_element_type=F32)
            if direction == 0:
                oacc_ref[rows, :] = out
            else:
                o = oacc_ref[rows, :] + out
                o = o * lax.rsqrt(jnp.mean(o * o, axis=-1, keepdims=True) + EPS) * gn_ref[...]
                o_ref[0, rows, :] = o.astype(BF16)
            return carry

        lax.fori_loop(0, nc_tot, step, 0)


def _retention(proj, log_gamma, gn, *, b, l_tot, lc, d_model):
    dk = d_model // RET_HEADS
    dv = 2 * d_model // RET_HEADS
    proj3 = proj.reshape(b, l_tot, proj.shape[-1])
    k_blk0 = d_model // dk
    v_blk0 = 2 * d_model // dv
    body = functools.partial(_ret_body, nc_ctx=lc // RET_CHUNK, nc_tot=l_tot // RET_CHUNK)
    return pl.pallas_call(
        body,
        grid=(b, RET_HEADS),
        in_specs=[pl.BlockSpec(memory_space=pltpu.SMEM),
                  pl.BlockSpec((1, l_tot, dk), lambda bi, hi: (bi, 0, hi)),
                  pl.BlockSpec((1, l_tot, dk), lambda bi, hi: (bi, 0, k_blk0 + hi)),
                  pl.BlockSpec((1, l_tot, dv), lambda bi, hi: (bi, 0, v_blk0 + hi)),
                  pl.BlockSpec((1, dv), lambda bi, hi: (0, hi))],
        out_specs=pl.BlockSpec((1, l_tot, dv), lambda bi, hi: (bi, 0, hi)),
        out_shape=jax.ShapeDtypeStruct((b, l_tot, RET_HEADS * dv), BF16),
        scratch_shapes=[pltpu.VMEM((l_tot, dv), F32), pltpu.VMEM((dk, dv), F32)],
        compiler_params=_params(2),
        name="retention",
    )(log_gamma, proj3, proj3, proj3, gn.reshape(1, RET_HEADS * dv))


def _retention_layer(x, h, cv, lv, w_in, decay, gn, w_out, cos, sin, *, b, l_tot, lc):
    m, d = x.shape
    proj = _ret_in_proj(h, w_in, cos, sin, l_tot=l_tot, d_model=d)
    log_gamma = jax.nn.log_sigmoid(decay.astype(F32))
    o = _retention(proj, log_gamma, gn, b=b, l_tot=l_tot, lc=lc, d_model=d).reshape(m, 2 * d)
    tm = _row_tile(l_tot, 544)
    gate_blk = 2

    def prologue(a_ref, extra, scratch):
        @pl.when(pl.program_id(1) == 0)
        def _():
            g = extra[0][...].astype(F32)
            scratch[0][...] = ((g * _sigmoid(g)) * a_ref[...].astype(F32)).astype(BF16)
        return scratch[0][...]

    return _residual_matmul(
        o, w_out, jnp.zeros((d,), F32), x, cv, lv, l_tot=l_tot, lc=lc, r_gate=2,
        tm=tm, tn=512, name="ret_out_proj", prologue=prologue,
        pro_extras=[proj], pro_specs=[pl.BlockSpec((tm, 2 * d), lambda i, j: (i, gate_blk))],
        scratch_shapes=[pltpu.VMEM((tm, 2 * d), BF16)])


def _mla_in_proj(h, w_in_pad, qa_gain, kva_gain, *, l_tot):
    m = h.shape[0]
    n = w_in_pad.shape[1]
    tm = _row_tile(l_tot, 1088)
    r = MLA_RANK

    def epilogue(acc, extra, outs):
        qg_ref, kg_ref = extra
        ql = acc[:, :r]
        kl = acc[:, r:2 * r]
        outs[0][...] = (ql * lax.rsqrt(jnp.mean(ql * ql, axis=-1, keepdims=True) + EPS)
                        * qg_ref[...]).astype(BF16)
        outs[1][...] = (kl * lax.rsqrt(jnp.mean(kl * kl, axis=-1, keepdims=True) + EPS)
                        * kg_ref[...]).astype(BF16)
        outs[2][...] = acc[:, 2 * r:]

    gspec = pl.BlockSpec((1, r), lambda i, j: (0, 0))
    return _matmul(h, w_in_pad, [qa_gain.reshape(1, r), kva_gain.reshape(1, r)], [gspec, gspec],
                   [jax.ShapeDtypeStruct((m, r), BF16), jax.ShapeDtypeStruct((m, r), BF16),
                    jax.ShapeDtypeStruct((m, LANES), F32)],
                   [pl.BlockSpec((tm, r), lambda i, j: (i, 0)),
                    pl.BlockSpec((tm, r), lambda i, j: (i, 0)),
                    pl.BlockSpec((tm, LANES), lambda i, j: (i, 0))],
                   epilogue, tm=tm, tn=n, name="mla_in_proj")


def _mla_q_proj(qn, wq_pad, q_gain_pad, tabs, *, l_tot):
    m = qn.shape[0]
    n = wq_pad.shape[1]
    tm = _row_tile(l_tot, 1088)
    tn = 1024
    tpb = l_tot // tm
    hw = 2 * LANES
    width = MLA_NOPE + MLA_ROPE
    scale = width ** -0.5

    def epilogue(acc, extra, outs):
        g_ref, c_ref, s1_ref, s2_ref = extra
        c, s1, s2 = c_ref[...], s1_ref[...], s2_ref[...]
        for hh in range(tn // hw):
            xs = acc[:, hh * hw:(hh + 1) * hw]
            inv = lax.rsqrt(jnp.sum(xs * xs, axis=-1, keepdims=True) * (1.0 / width) + EPS)
            y = xs * inv * g_ref[...] * scale
            outs[0][:, hh * hw:hh * hw + LANES] = y[:, :LANES].astype(BF16)
            outs[0][:, hh * hw + LANES:(hh + 1) * hw] = _rope3(y[:, LANES:], c, s1, s2).astype(BF16)

    tspec = pl.BlockSpec((tm, LANES), lambda i, j: (lax.rem(i, tpb), 0))
    return _matmul(qn, wq_pad, [q_gain_pad] + list(tabs),
                   [pl.BlockSpec((1, hw), lambda i, j: (0, 0)), tspec, tspec, tspec],
                   [jax.ShapeDtypeStruct((m, n), BF16)],
                   [pl.BlockSpec((tm, tn), lambda i, j: (i, j))],
                   epilogue, tm=tm, tn=tn, name="mla_q_proj")[0]


def _mla_kv_proj(kvn, w_kv, kr, k_gain_pad, tabs, *, l_tot):
    m = kvn.shape[0]
    n = w_kv.shape[1]
    tm = _row_tile(l_tot, 1088)
    tn = 1024
    tpb = l_tot // tm
    hw = MLA_NOPE + MLA_V
    width = MLA_NOPE + MLA_ROPE

    def epilogue(acc, extra, outs):
        kr_ref, g_ref, c_ref, s1_ref, s2_ref = extra
        kr_raw = kr_ref[...]
        ss_r = jnp.sum(kr_raw * kr_raw, axis=-1, keepdims=True)
        kr_rot = _rope3(kr_raw * g_ref[:, LANES:], c_ref[...], s1_ref[...], s2_ref[...])
        g_nope = g_ref[:, :LANES]
        for hh in range(tn // hw):
            kn = acc[:, hh * hw:hh * hw + MLA_NOPE]
            inv = lax.rsqrt((jnp.sum(kn * kn, axis=-1, keepdims=True) + ss_r) * (1.0 / width) + EPS)
            outs[0][:, hh * hw:hh * hw + LANES] = (kn * inv * g_nope).astype(BF16)
            outs[0][:, hh * hw + LANES:(hh + 1) * hw] = (kr_rot * inv).astype(BF16)
            outs[1][:, hh * MLA_V:(hh + 1) * MLA_V] = acc[:, hh * hw + MLA_NOPE:(hh + 1) * hw].astype(BF16)

    tspec = pl.BlockSpec((tm, LANES), lambda i, j: (lax.rem(i, tpb), 0))
    return _matmul(kvn, w_kv, [kr, k_gain_pad] + list(tabs),
                   [pl.BlockSpec((tm, LANES), lambda i, j: (i, 0)),
                    pl.BlockSpec((1, 2 * LANES), lambda i, j: (0, 0)), tspec, tspec, tspec],
                   [jax.ShapeDtypeStruct((m, n), BF16), jax.ShapeDtypeStruct((m, n // 2), BF16)],
                   [pl.BlockSpec((tm, tn), lambda i, j: (i, j)),
                    pl.BlockSpec((tm, tn // 2), lambda i, j: (i, j))],
                   epilogue, tm=tm, tn=tn, name="mla_kv_proj")


def _softmax_pv(q, k, v):
    s = lax.dot_general(q, k, (((1,), (1,)), ((), ())), preferred_element_type=F32)
    p = jnp.exp(s - jnp.max(s, axis=-1, keepdims=True))
    den = jnp.sum(p, axis=-1, keepdims=True)
    return jnp.dot(p.astype(BF16), v, preferred_element_type=F32) / den


def _mla_attn_body(q_ref, k_ref, v_ref, o_ref, *, lc, tq, n_q):
    o_ref[0, 0:lc, :] = _softmax_pv(q_ref[0, 0:lc, :], k_ref[0, 0:lc, :], v_ref[0, 0:lc, :]).astype(BF16)

    def step(t, carry):
        rows = pl.ds(pl.multiple_of(lc + t * tq, tq), tq)
        o_ref[0, rows, :] = _softmax_pv(q_ref[0, rows, :], k_ref[0], v_ref[0]).astype(BF16)
        return carry

    lax.fori_loop(0, n_q, step, 0)


def _mla_attention(q, k, v, *, b, l_tot, lc):
    hw = 2 * LANES
    tq = 256
    assert lc % tq == 0 and (l_tot - lc) % tq == 0
    q3 = q.reshape(b, l_tot, MLA_HEADS * hw)
    k3 = k.reshape(b, l_tot, MLA_HEADS * hw)
    v3 = v.reshape(b, l_tot, MLA_HEADS * MLA_V)
    body = functools.partial(_mla_attn_body, lc=lc, tq=tq, n_q=(l_tot - lc) // tq)
    return pl.pallas_call(
        body,
        grid=(b, MLA_HEADS),
        in_specs=[pl.BlockSpec((1, l_tot, hw), lambda bi, hi: (bi, 0, hi)),
                  pl.BlockSpec((1, l_tot, hw), lambda bi, hi: (bi, 0, hi)),
                  pl.BlockSpec((1, l_tot, MLA_V), lambda bi, hi: (bi, 0, hi))],
        out_specs=pl.BlockSpec((1, l_tot, MLA_V), lambda bi, hi: (bi, 0, hi)),
        out_shape=jax.ShapeDtypeStruct((b, l_tot, MLA_HEADS * MLA_V), BF16),
        compiler_params=_params(2),
        name="mla_attention",
    )(q3, k3, v3)


def _mla_layer(x, h, cv, lv, w_in, qa_gain, kva_gain, w_q_b, w_kv_b, q_gain, k_gain, w_out, tabs,
               *, b, l_tot, lc):
    m, d = x.shape
    width = MLA_NOPE + MLA_ROPE
    hw = 2 * LANES
    w_in_pad = jnp.pad(w_in, ((0, 0), (0, 2 * MLA_RANK + LANES - w_in.shape[1])))
    wq_pad = jnp.pad(w_q_b.reshape(MLA_RANK, MLA_HEADS, width),
                     ((0, 0), (0, 0), (0, hw - width))).reshape(MLA_RANK, MLA_HEADS * hw)
    q_gain_pad = jnp.pad(q_gain, (0, hw - width)).reshape(1, hw)
    k_gain_pad = jnp.pad(k_gain, (0, hw - width)).reshape(1, hw)
    qn, kvn, kr = _mla_in_proj(h, w_in_pad, qa_gain, kva_gain, l_tot=l_tot)
    q = _mla_q_proj(qn, wq_pad, q_gain_pad, tabs, l_tot=l_tot)
    k, v = _mla_kv_proj(kvn, w_kv_b, kr, k_gain_pad, tabs, l_tot=l_tot)
    o = _mla_attention(q, k, v, b=b, l_tot=l_tot, lc=lc).reshape(m, MLA_HEADS * MLA_V)
    return _residual_matmul(o, w_out, jnp.zeros((d,), F32), x, cv, lv, l_tot=l_tot, lc=lc,
                            r_gate=2, tm=_row_tile(l_tot, 2176), tn=512, name="mla_out_proj")


def _conv_gate_body(bg_ref, cg_ref, u_ref, w_ref, cb_ref, o_ref, z_ref, *, l_tot, lc):
    pad = 8
    tn = o_ref.shape[-1]
    z_ref[0:pad, :] = jnp.zeros((pad, tn), F32)
    z_ref[pad + l_tot:, :] = jnp.zeros((pad, tn), F32)
    z_ref[pad:pad + l_tot, :] = cg_ref[0].astype(F32) * u_ref[0].astype(F32)
    rows = lax.broadcasted_iota(jnp.int32, (l_tot, 1), 0)
    z_prev = jnp.where(rows == lc, 0.0, z_ref[pad - 1:pad - 1 + l_tot, :])
    z_next = jnp.where(rows == lc - 1, 0.0, z_ref[pad + 1:pad + 1 + l_tot, :])
    z_mid = z_ref[pad:pad + l_tot, :]
    conv = z_prev * w_ref[0:1, :] + z_mid * w_ref[1:2, :] + z_next * w_ref[2:3, :] + cb_ref[...]
    o_ref[0] = (bg_ref[0].astype(F32) * conv).astype(BF16)


def _conv_gate(proj, conv_w, conv_b, *, b, l_tot, lc, d):
    tn = 256
    nb = d // tn
    proj3 = proj.reshape(b, l_tot, 3 * d)
    body = functools.partial(_conv_gate_body, l_tot=l_tot, lc=lc)
    return pl.pallas_call(
        body,
        grid=(b, nb),
        in_specs=[pl.BlockSpec((1, l_tot, tn), lambda bi, j: (bi, 0, j)),
                  pl.BlockSpec((1, l_tot, tn), lambda bi, j: (bi, 0, nb + j)),
                  pl.BlockSpec((1, l_tot, tn), lambda bi, j: (bi, 0, 2 * nb + j)),
                  pl.BlockSpec((3, tn), lambda bi, j: (0, j)),
                  pl.BlockSpec((1, tn), lambda bi, j: (0, j))],
        out_specs=pl.BlockSpec((1, l_tot, tn), lambda bi, j: (bi, 0, j)),
        out_shape=jax.ShapeDtypeStruct((b, l_tot, d), BF16),
        scratch_shapes=[pltpu.VMEM((l_tot + 16, tn), F32)],
        compiler_params=_params(2),
        name="conv_gate",
    )(proj3, proj3, proj3, conv_w, conv_b.reshape(1, d))


def _shortconv_layer(x, h, cv, lv, w_in, conv_w, conv_b, w_out, *, b, l_tot, lc):
    m, d = x.shape
    tm = _row_tile(l_tot, 2176)
    proj = _matmul(h, w_in, [], [], [jax.ShapeDtypeStruct((m, 3 * d), BF16)],
                   [pl.BlockSpec((tm, 512), lambda i, j: (i, j))],
                   _plain_epilogue(BF16), tm=tm, tn=512, name="sc_in_proj")[0]
    a = _conv_gate(proj, conv_w, conv_b, b=b, l_tot=l_tot, lc=lc, d=d).reshape(m, d)
    return _residual_matmul(a, w_out, jnp.zeros((d,), F32), x, cv, lv, l_tot=l_tot, lc=lc,
                            r_gate=2, tm=tm, tn=512, name="sc_out_proj")


def _split_hi_lo(x):
    hi = x.astype(BF16)
    lo = (x - hi.astype(F32)).astype(BF16)
    return hi, lo


def _head_rms_scale(x, seg_ref, segt_ref):
    hi, lo = _split_hi_lo(x * x)
    seg = seg_ref[...]
    ss = jnp.dot(hi, seg, preferred_element_type=F32) + jnp.dot(lo, seg, preferred_element_type=F32)
    inv = lax.rsqrt(ss * (1.0 / GQA_HEAD_DIM) + EPS)
    ihi, ilo = _split_hi_lo(inv)
    segt = segt_ref[...]
    return jnp.dot(ihi, segt, preferred_element_type=F32) + jnp.dot(ilo, segt, preferred_element_type=F32)


def _gqa_in_proj(h, w_in, b_in, qg_t, kg_t, seg, segt, tabs, *, l_tot):
    m = h.shape[0]
    n = w_in.shape[1]
    tm = _row_tile(l_tot, 1088)
    tn = 512
    tpb = l_tot // tm
    n_q_tiles = GQA_HEADS * GQA_HEAD_DIM // tn
    kv_w = GQA_KV_HEADS * GQA_HEAD_DIM
    assert n == n_q_tiles * tn + 2 * kv_w and 2 * kv_w == tn
    scale = GQA_HEAD_DIM ** -0.5

    def norm_rope(xs, g, extra, mult):
        seg_ref, segt_ref, c_ref, s1_ref, s2_ref = extra[3:8]
        y = xs * _head_rms_scale(xs, seg_ref, segt_ref) * g
        c, s1, s2 = c_ref[...], s1_ref[...], s2_ref[...]
        return [_rope3(y[:, a * LANES:(a + 1) * LANES], c, s1, s2) * mult
                for a in range(xs.shape[1] // LANES)]

    def dup_heads(slab, lo_mask):
        swapped = pltpu.roll(slab, GQA_HEAD_DIM, axis=1)
        return jnp.where(lo_mask, slab, swapped), jnp.where(lo_mask, swapped, slab)

    def epilogue(acc, extra, outs):
        b_ref, qg_ref, kg_ref = extra[0:3]
        j = pl.program_id(1)
        acc = acc + b_ref[...]

        @pl.when(j < n_q_tiles)
        def _():
            for a, slab in enumerate(norm_rope(acc, qg_ref[...], extra, scale)):
                outs[0][:, a * LANES:(a + 1) * LANES] = slab.astype(BF16)

        @pl.when(j == n_q_tiles)
        def _():
            lo_mask = lax.broadcasted_iota(jnp.int32, (tm, LANES), 1) < GQA_HEAD_DIM
            kpad = jnp.concatenate([acc[:, :kv_w], acc[:, :kv_w]], axis=1)
            k_slabs = norm_rope(kpad, kg_ref[...], extra, 1.0)[:kv_w // LANES]
            for a in range(kv_w // LANES):
                ka, kb = dup_heads(k_slabs[a], lo_mask)
                outs[1][:, (2 * a) * LANES:(2 * a + 1) * LANES] = ka.astype(BF16)
                outs[1][:, (2 * a + 1) * LANES:(2 * a + 2) * LANES] = kb.astype(BF16)
                va, vb = dup_heads(acc[:, kv_w + a * LANES:kv_w + (a + 1) * LANES], lo_mask)
                outs[2][:, (2 * a) * LANES:(2 * a + 1) * LANES] = va.astype(BF16)
                outs[2][:, (2 * a + 1) * LANES:(2 * a + 2) * LANES] = vb.astype(BF16)

    tspec = pl.BlockSpec((tm, LANES), lambda i, j: (lax.rem(i, tpb), 0))
    full = lambda shape: pl.BlockSpec(shape, lambda i, j: (0, 0))
    return _matmul(
        h, w_in, [b_in.reshape(1, n), qg_t, kg_t, seg, segt] + list(tabs),
        [pl.BlockSpec((1, tn), lambda i, j: (0, j)), full((1, tn)), full((1, tn)),
         full((tn, LANES)), full((LANES, tn)), tspec, tspec, tspec],
        [jax.ShapeDtypeStruct((m, n_q_tiles * tn), BF16),
         jax.ShapeDtypeStruct((m, GQA_KV_HEADS * LANES), BF16),
         jax.ShapeDtypeStruct((m, GQA_KV_HEADS * LANES), BF16)],
        [pl.BlockSpec((tm, tn), lambda i, j: (i, jnp.minimum(j, n_q_tiles - 1))),
         pl.BlockSpec((tm, GQA_KV_HEADS * LANES), lambda i, j: (i, 0)),
         pl.BlockSpec((tm, GQA_KV_HEADS * LANES), lambda i, j: (i, 0))],
        epilogue, tm=tm, tn=tn, name="gqa_in_proj")


def _gqa_attn_body(sink_ref, q_ref, k_ref, v_ref, o_ref, *, lc, nb):
    g = pl.program_id(1)
    r = WINDOW
    n_slab = GQA_GROUP // 2
    rows_all = GQA_GROUP * r
    row_head = lax.broadcasted_iota(jnp.int32, (rows_all, 1), 0) // r
    sink = jnp.zeros((rows_all, 1), F32)
    for hh in range(GQA_GROUP):
        sink = jnp.where(row_head == hh, sink_ref[g * GQA_GROUP + hh], sink)
    lo_mask = lax.broadcasted_iota(jnp.int32, (r, LANES), 1) < GQA_HEAD_DIM
    ii = lax.rem(lax.broadcasted_iota(jnp.int32, (rows_all, r), 0), r)
    jj = lax.broadcasted_iota(jnp.int32, (rows_all, r), 1)
    k_ctx = k_ref[0, 0:lc, :]
    v_ctx = v_ref[0, 0:lc, :]

    def stack_q(rows):
        parts = []
        for a in range(n_slab):
            qa = q_ref[0, rows, a * LANES:(a + 1) * LANES]
            zero = jnp.zeros_like(qa)
            parts += [jnp.where(lo_mask, qa, zero), jnp.where(lo_mask, zero, qa)]
        return jnp.concatenate(parts, axis=0)

    def attend(rows, blocks):
        qs = stack_q(rows)
        scores = []
        for kb, _, mask in blocks:
            s = lax.dot_general(qs, kb, (((1,), (1,)), ((), ())), preferred_element_type=F32)
            scores.append(s if mask is None else jnp.where(mask, s, NEG_INF))
        mx = sink
        for s in scores:
            mx = jnp.maximum(mx, jnp.max(s, axis=-1, keepdims=True))
        den = jnp.exp(sink - mx)
        acc = jnp.zeros((rows_all, LANES), F32)
        for s, (_, vb, _) in zip(scores, blocks):
            p = jnp.exp(s - mx)
            den = den + jnp.sum(p, axis=-1, keepdims=True)
            acc = acc + jnp.dot(p.astype(BF16), vb, preferred_element_type=F32)
        o = acc / den
        for a in range(n_slab):
            oa = jnp.where(lo_mask, o[(2 * a) * r:(2 * a + 1) * r], o[(2 * a + 1) * r:(2 * a + 2) * r])
            o_ref[0, rows, a * LANES:(a + 1) * LANES] = oa.astype(BF16)

    for cb in range(lc // r):
        attend(pl.ds(cb * r, r), [(k_ctx, v_ctx, None)])

    def step(t, carry):
        r_cur = pl.multiple_of(lc + t * r, r)
        r_prev = pl.multiple_of(lc + jnp.maximum(t - 1, 0) * r, r)
        r_next = pl.multiple_of(lc + jnp.minimum(t + 1, nb - 1) * r, r)
        blk = lambda r0: (k_ref[0, pl.ds(r0, r), :], v_ref[0, pl.ds(r0, r), :])
        m_prev = jnp.logical_and(jj >= ii, t > 0)
        m_next = jnp.logical_and(jj <= ii, t < nb - 1)
        attend(pl.ds(r_cur, r), [blk(r_prev) + (m_prev,), blk(r_cur) + (None,),
                                 blk(r_next) + (m_next,), (k_ctx, v_ctx, None)])
        return carry

    lax.fori_loop(0, nb, step, 0)


def _gqa_attention(q, k, v, sink, *, b, l_tot, lc):
    gw = GQA_GROUP * GQA_HEAD_DIM
    q3 = q.reshape(b, l_tot, GQA_HEADS * GQA_HEAD_DIM)
    k3 = k.reshape(b, l_tot, GQA_KV_HEADS * LANES)
    v3 = v.reshape(b, l_tot, GQA_KV_HEADS * LANES)
    body = functools.partial(_gqa_attn_body, lc=lc, nb=(l_tot - lc) // WINDOW)
    return pl.pallas_call(
        body,
        grid=(b, GQA_KV_HEADS),
        in_specs=[pl.BlockSpec(memory_space=pltpu.SMEM),
                  pl.BlockSpec((1, l_tot, gw), lambda bi, gi: (bi, 0, gi)),
                  pl.BlockSpec((1, l_tot, LANES), lambda bi, gi: (bi, 0, gi)),
                  pl.BlockSpec((1, l_tot, LANES), lambda bi, gi: (bi, 0, gi))],
        out_specs=pl.BlockSpec((1, l_tot, gw), lambda bi, gi: (bi, 0, gi)),
        out_shape=jax.ShapeDtypeStruct((b, l_tot, GQA_HEADS * GQA_HEAD_DIM), BF16),
        compiler_params=_params(2),
        name="gqa_attention",
    )(sink, q3, k3, v3)


def _gqa_layer(x, h, cv, lv, w_in, b_in, q_gain, k_gain, sink, w_out, b_out, tabs, *, b, l_tot, lc):
    m, d = x.shape
    tn = 512
    heads_per_tile = tn // GQA_HEAD_DIM
    col = jnp.arange(tn)
    seg = (col[:, None] // GQA_HEAD_DIM == jnp.arange(LANES)[None, :]).astype(BF16)
    qg_t = jnp.tile(q_gain, heads_per_tile).reshape(1, tn)
    kg_t = jnp.tile(k_gain, heads_per_tile).reshape(1, tn)
    q, k, v = _gqa_in_proj(h, w_in, b_in, qg_t, kg_t, seg, seg.T, tabs, l_tot=l_tot)
    o = _gqa_attention(q, k, v, sink.astype(F32), b=b, l_tot=l_tot, lc=lc).reshape(m, d)
    return _residual_matmul(o, w_out, b_out, x, cv, lv, l_tot=l_tot, lc=lc, r_gate=2,
                            tm=_row_tile(l_tot, 2176), tn=512, name="gqa_out_proj")


def _pack_halves(yb):
    w = yb.shape[1] // 2
    lo = lax.bitcast_convert_type(yb[:, :w].astype(F32), jnp.uint32)
    hi = lax.bitcast_convert_type(yb[:, w:].astype(F32), jnp.uint32)
    return lax.shift_right_logical(lo, jnp.uint32(16)) | hi


def _unpack_halves(words):
    lo = lax.bitcast_convert_type(lax.shift_left(words, jnp.uint32(16)), F32).astype(BF16)
    hi = lax.bitcast_convert_type(words & jnp.uint32(0xFFFF0000), F32).astype(BF16)
    return lo, hi


def _ffn_route_body(x_ref, g_ref, cv_ref, lv_ref, rw_ref, rb_ref, fpk_ref, ri_ref, rg_ref, cnt_ref,
                    carry_ref, *, tm, tpb, lc, route_ctx, n_e):
    i = pl.program_id(0)

    @pl.when(i == 0)
    def _():
        carry_ref[...] = jnp.zeros_like(carry_ref)

    x = x_ref[...]
    y = x * lax.rsqrt(jnp.mean(x * x, axis=-1, keepdims=True) + EPS) * g_ref[...]
    is_ctx = _is_ctx_rows(i, tm, tpb, lc)
    y = y * (1.0 + _pick_mod(is_ctx, cv_ref, lv_ref, 4)) + _pick_mod(is_ctx, cv_ref, lv_ref, 3)
    yb = y.astype(BF16)
    fpk_ref[...] = _pack_halves(yb)

    lane = lax.broadcasted_iota(jnp.int32, (tm, LANES), 1)
    logits = jnp.dot(yb, rw_ref[...].astype(BF16), preferred_element_type=F32) + rb_ref[...]
    logits = jnp.where(lane < n_e, logits, NEG_INF)
    vals, idxs, hots = [], [], []
    for _ in range(TOP_K):
        mk = jnp.max(logits, axis=-1, keepdims=True)
        ik = jnp.min(jnp.where(logits == mk, lane, LANES), axis=-1, keepdims=True)
        hot = lane == ik
        vals.append(mk)
        idxs.append(ik)
        hots.append(hot)
        logits = jnp.where(hot, NEG_INF, logits)
    exps = [jnp.exp(v - vals[0]) for v in vals]
    den = exps[0] + exps[1] + exps[2] + exps[3]

    routed = jnp.ones_like(is_ctx) if route_ctx else jnp.logical_not(is_ctx)
    chosen = jnp.zeros((tm, LANES), F32)
    for hot in hots:
        chosen = chosen + jnp.where(jnp.logical_and(hot, routed), 1.0, 0.0)
    before = (lax.broadcasted_iota(jnp.int32, (tm, tm), 0)
              > lax.broadcasted_iota(jnp.int32, (tm, tm), 1)).astype(BF16)
    arrived = jnp.dot(before, chosen.astype(BF16), preferred_element_type=F32) + carry_ref[...]
    carry_ref[...] = carry_ref[...] + jnp.sum(chosen, axis=0, keepdims=True)
    cnt_ref[...] = carry_ref[...]

    ri = jnp.zeros((tm, LANES), jnp.int32)
    rg = jnp.zeros((tm, LANES), F32)
    for k in range(TOP_K):
        rank = jnp.sum(jnp.where(hots[k], arrived, 0.0), axis=-1, keepdims=True)
        ri = jnp.where(lane == k, idxs[k], ri)
        ri = jnp.where(lane == TOP_K + k, rank.astype(jnp.int32), ri)
        rg = jnp.where(lane == k, exps[k] / den, rg)
    ri_ref[...] = ri
    rg_ref[...] = rg


def _ffn_route(x, gain, cv, lv, router_w_pad, router_b_pad, layer, *, l_tot, lc, route_ctx, n_e):
    m, d = x.shape
    tm = _row_tile(l_tot, 544)
    tpb = l_tot // tm
    body = functools.partial(_ffn_route_body, tm=tm, tpb=tpb, lc=lc, route_ctx=route_ctx, n_e=n_e)
    return pl.pallas_call(
        body,
        grid=(m // tm,),
        in_specs=[pl.BlockSpec((tm, d), lambda i: (i, 0)),
                  pl.BlockSpec((1, d), lambda i: (0, 0)),
                  pl.BlockSpec((MOD_ROWS, d), lambda i: (0, 0)),
                  pl.BlockSpec((1, MOD_ROWS, d), lambda i: (i // tpb, 0, 0)),
                  pl.BlockSpec((None, d, LANES), lambda i: (layer, 0, 0)),
                  pl.BlockSpec((None, 1, LANES), lambda i: (layer, 0, 0))],
        out_specs=[pl.BlockSpec((tm, d // 2), lambda i: (i, 0)),
                   pl.BlockSpec((tm, LANES), lambda i: (i, 0)),
                   pl.BlockSpec((tm, LANES), lambda i: (i, 0)),
                   pl.BlockSpec((1, LANES), lambda i: (0, 0))],
        out_shape=[jax.ShapeDtypeStruct((m, d // 2), jnp.uint32),
                   jax.ShapeDtypeStruct((m, LANES), jnp.int32),
                   jax.ShapeDtypeStruct((m, LANES), F32),
                   jax.ShapeDtypeStruct((1, LANES), F32)],
        scratch_shapes=[pltpu.VMEM((1, LANES), F32)],
        compiler_params=_params(1),
        name="ffn_route",
    )(x, gain.reshape(1, d), cv, lv, router_w_pad, router_b_pad)


def _assignment_row(idx_ref, rank_ref, ps_ref, a):
    return ps_ref[idx_ref[a]] + rank_ref[a]


def _dispatch_body(idx_ref, rank_ref, ps_ref, f_ref, init_hbm, xr_hbm, sem, *, tm, tpb, ctx_tiles, route_ctx):
    del init_hbm
    i = pl.program_id(0)

    def scatter_rows():
        def start(r, carry):
            a0 = (i * tm + r) * TOP_K
            for k in range(TOP_K):
                dst = _assignment_row(idx_ref, rank_ref, ps_ref, a0 + k)
                pltpu.make_async_copy(f_ref.at[pl.ds(r, 1)], xr_hbm.at[pl.ds(dst, 1)], sem).start()
            return carry

        lax.fori_loop(0, tm, start, 0)

        def wait(r, carry):
            for k in range(TOP_K):
                pltpu.make_async_copy(f_ref.at[pl.ds(0, 1)], xr_hbm.at[pl.ds(0, 1)], sem).wait()
            return carry

        lax.fori_loop(0, tm, wait, 0)

    if route_ctx:
        scatter_rows()
    else:
        pl.when(lax.rem(i, tpb) >= ctx_tiles)(scatter_rows)


def _dispatch(fpk, idx_flat, rank_flat, pstart, n_rows, *, l_tot, lc, route_ctx):
    m, w = fpk.shape
    tm = 256
    assert lc % tm == 0 and l_tot % tm == 0
    body = functools.partial(_dispatch_body, tm=tm, tpb=l_tot // tm, ctx_tiles=lc // tm, route_ctx=route_ctx)
    return pl.pallas_call(
        body,
        grid_spec=pltpu.PrefetchScalarGridSpec(
            num_scalar_prefetch=3,
            grid=(m // tm,),
            in_specs=[pl.BlockSpec((tm, w), lambda i, a, r, p: (i, 0)),
                      pl.BlockSpec(memory_space=pl.ANY)],
            out_specs=pl.BlockSpec(memory_space=pl.ANY),
            scratch_shapes=[pltpu.SemaphoreType.DMA(())]),
        out_shape=jax.ShapeDtypeStruct((n_rows, w), jnp.uint32),
        input_output_aliases={4: 0},
        compiler_params=_params(1),
        name="moe_dispatch",
    )(idx_flat, rank_flat, pstart, fpk, jnp.zeros((n_rows, w), jnp.uint32))


def _moe_gu_body(be_ref, nu_ref, x_ref, wg_ref, wl_ref, bg_ref, bl_ref, o_ref, wgb_ref, wlb_ref):
    blk = pl.program_id(0)
    prev = be_ref[jnp.maximum(blk - 1, 0)]

    @pl.when(jnp.logical_or(blk == 0, be_ref[blk] != prev))
    def _():
        wgb_ref[...] = wg_ref[...].astype(BF16)
        wlb_ref[...] = wl_ref[...].astype(BF16)

    @pl.when(blk < nu_ref[0])
    def _():
        x_lo, x_hi = _unpack_halves(x_ref[...])
        w = x_lo.shape[1]
        h_glu = (jnp.dot(x_lo, wgb_ref[0:w, :], preferred_element_type=F32)
                 + jnp.dot(x_hi, wgb_ref[w:, :], preferred_element_type=F32) + bg_ref[...])
        h_lin = (jnp.dot(x_lo, wlb_ref[0:w, :], preferred_element_type=F32)
                 + jnp.dot(x_hi, wlb_ref[w:, :], preferred_element_type=F32) + bl_ref[...])
        h_glu = jnp.minimum(h_glu, SWIGLU_LIMIT)
        h_lin = jnp.clip(h_lin, -SWIGLU_LIMIT, SWIGLU_LIMIT)
        o_ref[...] = (h_glu * _sigmoid(SWIGLU_ALPHA * h_glu) * (h_lin + 1.0)).astype(BF16)

    @pl.when(blk >= nu_ref[0])
    def _():
        o_ref[...] = jnp.zeros_like(o_ref)


def _moe_down_body(be_ref, nu_ref, a_ref, w_ref, b_ref, o_ref, wb_ref):
    blk = pl.program_id(0)
    prev = be_ref[jnp.maximum(blk - 1, 0)]

    @pl.when(jnp.logical_or(blk == 0, be_ref[blk] != prev))
    def _():
        wb_ref[...] = w_ref[...].astype(BF16)

    @pl.when(blk < nu_ref[0])
    def _():
        o_ref[...] = jnp.dot(a_ref[...], wb_ref[...], preferred_element_type=F32) + b_ref[...]

    @pl.when(blk >= nu_ref[0])
    def _():
        o_ref[...] = jnp.zeros_like(o_ref)


def _moe_experts(x_rows, block_expert, n_used, w_gu, b_gu, w_down, b_down, layer):
    n_rows, w = x_rows.shape
    n_blocks = n_rows // MOE_TM
    depth, n_e, ff, d = w_down.shape
    b_gu4 = b_gu.reshape(depth, n_e, 1, 2 * ff)
    act = pl.pallas_call(
        _moe_gu_body,
        grid_spec=pltpu.PrefetchScalarGridSpec(
            num_scalar_prefetch=2,
            grid=(n_blocks,),
            in_specs=[pl.BlockSpec((MOE_TM, w), lambda i, be, nu: (i, 0)),
                      pl.BlockSpec((None, None, d, ff), lambda i, be, nu: (layer, be[i], 0, 0)),
                      pl.BlockSpec((None, None, d, ff), lambda i, be, nu: (layer, be[i], 0, 1)),
                      pl.BlockSpec((None, None, 1, ff), lambda i, be, nu: (layer, be[i], 0, 0)),
                      pl.BlockSpec((None, None, 1, ff), lambda i, be, nu: (layer, be[i], 0, 1))],
            out_specs=pl.BlockSpec((MOE_TM, ff), lambda i, be, nu: (i, 0)),
            scratch_shapes=[pltpu.VMEM((d, ff), BF16), pltpu.VMEM((d, ff), BF16)]),
        out_shape=jax.ShapeDtypeStruct((n_rows, ff), BF16),
        compiler_params=_params(1),
        name="moe_gate_up",
    )(block_expert, n_used, x_rows, w_gu, w_gu, b_gu4, b_gu4)
    return pl.pallas_call(
        _moe_down_body,
        grid_spec=pltpu.PrefetchScalarGridSpec(
            num_scalar_prefetch=2,
            grid=(n_blocks,),
            in_specs=[pl.BlockSpec((MOE_TM, ff), lambda i, be, nu: (i, 0)),
                      pl.BlockSpec((None, None, ff, d), lambda i, be, nu: (layer, be[i], 0, 0)),
                      pl.BlockSpec((None, None, 1, d), lambda i, be, nu: (layer, be[i], 0, 0))],
            out_specs=pl.BlockSpec((MOE_TM, d), lambda i, be, nu: (i, 0)),
            scratch_shapes=[pltpu.VMEM((ff, d), BF16)]),
        out_shape=jax.ShapeDtypeStruct((n_rows, d), F32),
        compiler_params=_params(1),
        name="moe_down",
    )(block_expert, n_used, act, w_down, b_down.reshape(depth, n_e, 1, d))


def _combine_body(idx_ref, rank_ref, ps_ref, x_ref, rg_ref, cv_ref, lv_ref, y_hbm, o_ref, buf, sem,
                  *, tt, n_tiles, tpb, ctx_tiles, lc, route_ctx):
    i = pl.program_id(0)

    def routed(tile):
        return (tile >= 0) if route_ctx else (lax.rem(tile, tpb) >= ctx_tiles)

    def row_copy(src, slot, k, r):
        return pltpu.make_async_copy(y_hbm.at[pl.ds(src, 1)], buf.at[slot, k, pl.ds(r, 1)], sem.at[slot])

    def issue(tile, slot):
        def start(r, carry):
            a0 = (tile * tt + r) * TOP_K
            for k in range(TOP_K):
                row_copy(_assignment_row(idx_ref, rank_ref, ps_ref, a0 + k), slot, k, r).start()
            return carry

        lax.fori_loop(0, tt, start, 0)

    @pl.when(jnp.logical_and(i == 0, routed(i)))
    def _():
        issue(i, 0)

    nxt = jnp.minimum(i + 1, n_tiles - 1)

    @pl.when(jnp.logical_and(i + 1 < n_tiles, routed(nxt)))
    def _():
        issue(nxt, lax.rem(i + 1, 2))

    slot = lax.rem(i, 2)

    @pl.when(routed(i))
    def _():
        def wait(r, carry):
            for k in range(TOP_K):
                row_copy(0, slot, k, 0).wait()
            return carry

        lax.fori_loop(0, tt, wait, 0)
        rg = rg_ref[...]
        acc = rg[:, 0:1] * buf[slot, 0]
        for k in range(1, TOP_K):
            acc = acc + rg[:, k:k + 1] * buf[slot, k]
        gate = _pick_mod(_is_ctx_rows(i, tt, tpb, lc), cv_ref, lv_ref, 5)
        o_ref[...] = x_ref[...] + gate * acc

    @pl.when(jnp.logical_not(routed(i)))
    def _():
        o_ref[...] = x_ref[...]


def _combine(x, y_rows, rg, idx_flat, rank_flat, pstart, cv, lv, *, l_tot, lc, route_ctx):
    m, d = x.shape
    tt = 128
    assert lc % tt == 0 and l_tot % tt == 0
    tpb = l_tot // tt
    n_tiles = m // tt
    body = functools.partial(_combine_body, tt=tt, n_tiles=n_tiles, tpb=tpb, ctx_tiles=lc // tt, lc=lc,
                             route_ctx=route_ctx)
    return pl.pallas_call(
        body,
        grid_spec=pltpu.PrefetchScalarGridSpec(
            num_scalar_prefetch=3,
            grid=(n_tiles,),
            in_specs=[pl.BlockSpec((tt, d), lambda i, a, r, p: (i, 0)),
                      pl.BlockSpec((tt, LANES), lambda i, a, r, p: (i, 0)),
                      pl.BlockSpec((MOD_ROWS, d), lambda i, a, r, p: (0, 0)),
                      pl.BlockSpec((1, MOD_ROWS, d), lambda i, a, r, p: (i // tpb, 0, 0)),
                      pl.BlockSpec(memory_space=pl.ANY)],
            out_specs=pl.BlockSpec((tt, d), lambda i, a, r, p: (i, 0)),
            scratch_shapes=[pltpu.VMEM((2, TOP_K, tt, d), F32), pltpu.SemaphoreType.DMA((2,))]),
        out_shape=jax.ShapeDtypeStruct((m, d), F32),
        compiler_params=_params(1),
        name="moe_combine",
    )(idx_flat, rank_flat, pstart, x, rg, cv, lv, y_rows)


def _moe_layer(x, gain, cv, lv, router_w_pad, router_b_pad, w_gu, b_gu, w_down, b_down, layer,
               *, l_tot, lc, route_ctx):
    m, d = x.shape
    n_e = w_gu.shape[1]
    fpk, ri, rg, cnt = _ffn_route(x, gain, cv, lv, router_w_pad, router_b_pad, layer,
                                  l_tot=l_tot, lc=lc, route_ctx=route_ctx, n_e=n_e)
    counts = cnt[0, :n_e].astype(jnp.int32)
    padded = (counts + MOE_TM - 1) // MOE_TM * MOE_TM
    pend = jnp.cumsum(padded)
    pstart = (pend - padded).astype(jnp.int32)
    n_blocks = -(-m * TOP_K // MOE_TM) + n_e
    n_rows = n_blocks * MOE_TM
    block_start = jnp.arange(n_blocks, dtype=jnp.int32) * MOE_TM
    block_expert = jnp.minimum(jnp.sum(block_start[:, None] >= pend[None, :], axis=1), n_e - 1).astype(jnp.int32)
    n_used = (pend[-1] // MOE_TM).astype(jnp.int32).reshape(1)
    idx_flat = ri[:, :TOP_K].reshape(m * TOP_K)
    rank_flat = ri[:, TOP_K:2 * TOP_K].reshape(m * TOP_K)
    x_rows = _dispatch(fpk, idx_flat, rank_flat, pstart, n_rows, l_tot=l_tot, lc=lc, route_ctx=route_ctx)
    y_rows = _moe_experts(x_rows, block_expert, n_used, w_gu, b_gu, w_down, b_down, layer)
    return _combine(x, y_rows, rg, idx_flat, rank_flat, pstart, cv, lv, l_tot=l_tot, lc=lc,
                    route_ctx=route_ctx)


def _rope_1d_tables(l_tot, half):
    inv = ROPE_BASE ** (-jnp.arange(half, dtype=F32) / half)
    ang = jnp.arange(l_tot, dtype=F32)[:, None] * inv
    return jnp.cos(ang), jnp.sin(ang)


def _axial_tables(l_tot, lc):
    n_freq = GQA_HEAD_DIM // 4
    t = jnp.maximum(jnp.arange(l_tot) - lc, 0)
    pos = jnp.stack([(t // GRID_W).astype(F32), (t % GRID_W).astype(F32)], axis=-1)
    inv_freq = ROPE_BASE ** (-jnp.arange(n_freq, dtype=F32) / n_freq)
    ang = pos[:, :, None] * inv_freq
    lane = jnp.arange(LANES)
    axis = (lane % GQA_HEAD_DIM) // (2 * n_freq)
    freq = lane % n_freq
    first_half = (lane % (2 * n_freq)) < n_freq
    is_lat = (jnp.arange(l_tot) >= lc)[:, None]
    cos = jnp.where(is_lat, jnp.cos(ang)[:, axis, freq], 1.0)
    sin = jnp.where(is_lat, jnp.sin(ang)[:, axis, freq], 0.0)
    return cos, jnp.where(first_half, -sin, 0.0), jnp.where(first_half, 0.0, sin)


def kernel(x, c, ctx, c_ctx, norm_mix, norm_ffn, mod_w, mod_b, ret_w_in, ret_decay, ret_gn, ret_w_out, mla_w_in, mla_q_a_norm, mla_kv_a_norm, mla_w_q_b, mla_w_kv_b, mla_q_norm, mla_k_norm, mla_w_out, sc_w_in, sc_conv_w, sc_conv_b, sc_w_out, gqa_w_in, gqa_b_in, gqa_q_norm, gqa_k_norm, gqa_sink, gqa_w_out, gqa_b_out, moe_router_w, moe_router_b, moe_w_gu, moe_b_gu, moe_w_down, moe_b_down):
    b, n_lat, d = x.shape
    lc = ctx.shape[1]
    l_tot = lc + n_lat
    m = b * l_tot
    depth = mod_w.shape[0]
    assert b + 1 <= MOD_ROWS and lc % 256 == 0 and n_lat % 256 == 0

    xs = jnp.concatenate([ctx, x], axis=1).reshape(m, d)
    cvec = jnp.zeros((MOD_ROWS, d), F32).at[:b].set(c).at[b].set(c_ctx)
    mod = _modulation(cvec, mod_w, mod_b).reshape(depth, MOD_ROWS, 6, d)
    pad_rows = ((0, 0), (0, MOD_ROWS - 6), (0, 0))
    ret_cos, ret_sin = _rope_1d_tables(l_tot, d // RET_HEADS // 2)
    tabs = _axial_tables(l_tot, lc)
    n_e = moe_router_w.shape[-1]
    router_w_pad = jnp.pad(moe_router_w, ((0, 0), (0, 0), (0, LANES - n_e)))
    router_b_pad = jnp.pad(moe_router_b, ((0, 0), (0, LANES - n_e))).reshape(depth, 1, LANES)

    for i in range(depth):
        kind, j = i % N_MIXERS, i // N_MIXERS
        cv = jnp.pad(mod[i, b], pad_rows[1:])
        lv = jnp.pad(mod[i, :b], pad_rows)
        lay = dict(b=b, l_tot=l_tot, lc=lc)
        h = _norm_mod(xs, norm_mix[i], cv, lv, l_tot=l_tot, lc=lc, r_shift=0, r_scale=1)
        if kind == 0:
            xs = _retention_layer(xs, h, cv, lv, ret_w_in[j], ret_decay[j], ret_gn[j], ret_w_out[j],
                                  ret_cos, ret_sin, **lay)
        elif kind == 1:
            xs = _mla_layer(xs, h, cv, lv, mla_w_in[j], mla_q_a_norm[j], mla_kv_a_norm[j], mla_w_q_b[j],
                            mla_w_kv_b[j], mla_q_norm[j], mla_k_norm[j], mla_w_out[j], tabs, **lay)
        elif kind == 2:
            xs = _shortconv_layer(xs, h, cv, lv, sc_w_in[j], sc_conv_w[j], sc_conv_b[j], sc_w_out[j], **lay)
        else:
            xs = _gqa_layer(xs, h, cv, lv, gqa_w_in[j], gqa_b_in[j], gqa_q_norm[j], gqa_k_norm[j],
                            gqa_sink[j], gqa_w_out[j], gqa_b_out[j], tabs, **lay)
        xs = _moe_layer(xs, norm_ffn[i], cv, lv, router_w_pad, router_b_pad, moe_w_gu, moe_b_gu,
                        moe_w_down, moe_b_down, i, l_tot=l_tot, lc=lc, route_ctx=i < depth - 1)
    return xs.reshape(b, l_tot, d)[:, lc:, :]
```

```python
import functools

import jax
import jax.numpy as jnp
from jax import lax
from jax.experimental import pallas as pl
from jax.experimental.pallas import tpu as pltpu

F32 = jnp.float32
BF16 = jnp.bfloat16

EPS = 1e-6
ROPE_BASE = 10000.0
NEG_INF = -1e30
LOG2_E = 1.4426950408889634
GRID_W = 64

RET_HEADS = 8
RET_CHUNK = 128
MLA_HEADS = 16
MLA_NOPE = 128
MLA_ROPE = 64
MLA_V = 128
MLA_RANK = 512
GQA_HEADS = 32
GQA_KV_HEADS = 4
GQA_GROUP = GQA_HEADS // GQA_KV_HEADS
GQA_HEAD_DIM = 64
WINDOW = 128
N_EXPERTS = 32
TOP_K = 4
SWIGLU_LIMIT = 7.0
SWIGLU_ALPHA = 1.702
N_MIXERS = 4

LANES = 128
VMEM_LIMIT = 56 * 1024 * 1024
MOE_TM = 256
MOD_ROWS = 8
ISSUE_UNROLL = 4


def _params(n_axes, vmem=VMEM_LIMIT):
    return pltpu.CompilerParams(dimension_semantics=("arbitrary",) * n_axes,
                                vmem_limit_bytes=vmem)


def _row_copy_params():
    return pltpu.CompilerParams(dimension_semantics=("arbitrary",), vmem_limit_bytes=VMEM_LIMIT,
                                disable_bounds_checks=True)


def _row_tile(l_tot, cap):
    best = 16
    for t in range(16, cap + 1, 16):
        if l_tot % t == 0:
            best = t
    return best


def _is_ctx_rows(i, tm, tpb, lc):
    rows = lax.rem(i, tpb) * tm + lax.broadcasted_iota(jnp.int32, (tm, 1), 0)
    return rows < lc


def _pick_mod(is_ctx, cv_ref, lv_ref, r):
    return jnp.where(is_ctx, cv_ref[r:r + 1, :], lv_ref[0, r:r + 1, :])


def _sigmoid(x):
    return 1.0 / (1.0 + jnp.exp(-x))


def _rope3(x, c, s1, s2):
    return x * c + pltpu.roll(x, LANES - 16, axis=1) * s1 + pltpu.roll(x, 16, axis=1) * s2


def _mod_body(c_ref, w_ref, b_ref, o_ref):
    c = c_ref[...]
    s = (c * _sigmoid(c)).astype(BF16)
    o_ref[0] = jnp.dot(s, w_ref[0].astype(BF16), preferred_element_type=F32) + b_ref[0]


def _modulation(cvec, mod_w, mod_b):
    depth, d, n = mod_w.shape
    tn = 2048
    return pl.pallas_call(
        _mod_body,
        grid=(depth, n // tn),
        in_specs=[pl.BlockSpec((MOD_ROWS, d), lambda l, j: (0, 0)),
                  pl.BlockSpec((1, d, tn), lambda l, j: (l, 0, j)),
                  pl.BlockSpec((1, 1, tn), lambda l, j: (l, 0, j))],
        out_specs=pl.BlockSpec((1, MOD_ROWS, tn), lambda l, j: (l, 0, j)),
        out_shape=jax.ShapeDtypeStruct((depth, MOD_ROWS, n), F32),
        compiler_params=_params(2),
        name="modulation",
    )(cvec, mod_w, mod_b.reshape(depth, 1, n))


def _norm_mod_body(x_ref, g_ref, cv_ref, lv_ref, o_ref, *, tm, tpb, lc, r_shift, r_scale):
    x = x_ref[...]
    y = x * lax.rsqrt(jnp.mean(x * x, axis=-1, keepdims=True) + EPS) * g_ref[...]
    is_ctx = _is_ctx_rows(pl.program_id(0), tm, tpb, lc)
    shift = _pick_mod(is_ctx, cv_ref, lv_ref, r_shift)
    scale = _pick_mod(is_ctx, cv_ref, lv_ref, r_scale)
    o_ref[...] = (y * (1.0 + scale) + shift).astype(BF16)


def _norm_mod(x, gain, cv, lv, *, l_tot, lc, r_shift, r_scale):
    m, d = x.shape
    tm = _row_tile(l_tot, 544)
    tpb = l_tot // tm
    body = functools.partial(_norm_mod_body, tm=tm, tpb=tpb, lc=lc, r_shift=r_shift, r_scale=r_scale)
    return pl.pallas_call(
        body,
        grid=(m // tm,),
        in_specs=[pl.BlockSpec((tm, d), lambda i: (i, 0)),
                  pl.BlockSpec((1, d), lambda i: (0, 0)),
                  pl.BlockSpec((MOD_ROWS, d), lambda i: (0, 0)),
                  pl.BlockSpec((1, MOD_ROWS, d), lambda i: (i // tpb, 0, 0))],
        out_specs=pl.BlockSpec((tm, d), lambda i: (i, 0)),
        out_shape=jax.ShapeDtypeStruct((m, d), BF16),
        compiler_params=_params(1),
        name="norm_mod",
    )(x, gain.reshape(1, d), cv, lv)


def _mm_body(*refs, n_extra, n_out, epilogue, prologue):
    a_ref, w_ref = refs[0], refs[1]
    extra = refs[2:2 + n_extra]
    outs = refs[2 + n_extra:2 + n_extra + n_out]
    scratch = refs[2 + n_extra + n_out:]
    a = a_ref[...] if prologue is None else prologue(a_ref, extra, scratch)
    acc = jnp.dot(a, w_ref[...].astype(BF16), preferred_element_type=F32)
    epilogue(acc, extra, outs)


def _matmul(a, w, extras, extra_specs, out_shapes, out_specs, epilogue, *, tm, tn, name,
            prologue=None, scratch_shapes=()):
    m, k = a.shape
    n = w.shape[1]
    body = functools.partial(_mm_body, n_extra=len(extras), n_out=len(out_shapes),
                             epilogue=epilogue, prologue=prologue)
    return pl.pallas_call(
        body,
        grid=(m // tm, n // tn),
        in_specs=[pl.BlockSpec((tm, k), lambda i, j: (i, 0)),
                  pl.BlockSpec((k, tn), lambda i, j: (0, j))] + list(extra_specs),
        out_specs=list(out_specs),
        out_shape=list(out_shapes),
        scratch_shapes=list(scratch_shapes),
        compiler_params=_params(2),
        name=name,
    )(a, w, *extras)


def _plain_epilogue(out_dtype):
    def epilogue(acc, extra, outs):
        outs[0][...] = acc.astype(out_dtype)
    return epilogue


def _bias_epilogue(out_dtype):
    def epilogue(acc, extra, outs):
        outs[0][...] = (acc + extra[0][...]).astype(out_dtype)
    return epilogue


def _residual_matmul(a, w, bias, x_res, cv, lv, *, l_tot, lc, r_gate, tm, tn, name,
                     prologue=None, pro_extras=(), pro_specs=(), scratch_shapes=()):
    m = a.shape[0]
    n = w.shape[1]
    tpb = l_tot // tm
    n_pro = len(pro_extras)

    def epilogue(acc, extra, outs):
        b_ref, x_ref, cv_ref, lv_ref = extra[n_pro:n_pro + 4]
        is_ctx = _is_ctx_rows(pl.program_id(0), tm, tpb, lc)
        gate = _pick_mod(is_ctx, cv_ref, lv_ref, r_gate)
        outs[0][...] = x_ref[...] + gate * (acc + b_ref[...])

    extras = list(pro_extras) + [bias.reshape(1, n), x_res, cv, lv]
    specs = list(pro_specs) + [
        pl.BlockSpec((1, tn), lambda i, j: (0, j)),
        pl.BlockSpec((tm, tn), lambda i, j: (i, j)),
        pl.BlockSpec((MOD_ROWS, tn), lambda i, j: (0, j)),
        pl.BlockSpec((1, MOD_ROWS, tn), lambda i, j: (i // tpb, 0, j))]
    return _matmul(a, w, extras, specs,
                   [jax.ShapeDtypeStruct((m, n), F32)],
                   [pl.BlockSpec((tm, tn), lambda i, j: (i, j))],
                   epilogue, tm=tm, tn=tn, name=name, prologue=prologue,
                   scratch_shapes=scratch_shapes)[0]


def _ret_in_proj(h, w_in, cos, sin, *, l_tot, d_model):
    m = h.shape[0]
    n = w_in.shape[1]
    tm = _row_tile(l_tot, 2176)
    tn = 512
    tpb = l_tot // tm
    dk = d_model // RET_HEADS
    half = dk // 2
    n_q_tiles = d_model // tn
    k_scale = dk ** -0.5

    def epilogue(acc, extra, outs):
        cos_ref, sin_ref = extra
        j = pl.program_id(1)

        @pl.when(j < 2 * n_q_tiles)
        def _():
            c = cos_ref[...]
            s = sin_ref[...]
            mult = jnp.where(j >= n_q_tiles, k_scale, 1.0)
            for hh in range(tn // dk):
                x1 = acc[:, hh * dk:hh * dk + half]
                x2 = acc[:, hh * dk + half:(hh + 1) * dk]
                outs[0][:, hh * dk:hh * dk + half] = ((x1 * c - x2 * s) * mult).astype(BF16)
                outs[0][:, hh * dk + half:(hh + 1) * dk] = ((x2 * c + x1 * s) * mult).astype(BF16)

        @pl.when(j >= 2 * n_q_tiles)
        def _():
            outs[0][...] = acc.astype(BF16)

    tab_spec = pl.BlockSpec((tm, half), lambda i, j: (lax.rem(i, tpb), 0))
    return _matmul(h, w_in, [cos, sin], [tab_spec, tab_spec],
                   [jax.ShapeDtypeStruct((m, n), BF16)],
                   [pl.BlockSpec((tm, tn), lambda i, j: (i, j))],
                   epilogue, tm=tm, tn=tn, name="ret_in_proj")[0]


def _ret_body(lg_ref, q_ref, k_ref, v_ref, g_ref, gn_ref, o_ref, oacc_ref, sf_ref, sb_ref, *, nc_ctx, nc_tot):
    h = pl.program_id(1)
    ch = RET_CHUNK
    dv = v_ref.shape[-1]
    ii = lax.broadcasted_iota(jnp.int32, (ch, ch), 0).astype(F32)
    jj = lax.broadcasted_iota(jnp.int32, (ch, ch), 1).astype(F32)
    ri = lax.broadcasted_iota(jnp.int32, (ch, 1), 0).astype(F32)
    consts = []
    for direction in (0, 1):
        lg = lg_ref[direction, h]
        if direction == 0:
            diff = ii - jj
            q_decay = jnp.exp((ri + 1.0) * lg)
            k_decay = jnp.exp((ch - 1.0 - ri) * lg)
        else:
            diff = jj - ii
            q_decay = jnp.exp((ch - ri) * lg)
            k_decay = jnp.exp(ri * lg)
        intra = jnp.where(diff >= 0, jnp.exp(jnp.maximum(diff, 0.0) * lg), 0.0)
        chunk_decay = jnp.exp(jnp.full((1, dv), float(ch), F32) * lg)
        consts.append((intra, q_decay, k_decay, chunk_decay))
    sf_ref[...] = jnp.zeros_like(sf_ref)
    sb_ref[...] = jnp.zeros_like(sb_ref)
    oacc_ref[...] = jnp.zeros_like(oacc_ref)

    def chunk_rows(c):
        return pl.ds(pl.multiple_of(c * ch, ch), ch)

    def advance(c, s_ref, intra, q_decay, k_decay, chunk_decay):
        rows = chunk_rows(c)
        qc = q_ref[0, rows, :]
        kc = k_ref[0, rows, :]
        vc = v_ref[0, rows, :]
        scores = lax.dot_general(qc, kc, (((1,), (1,)), ((), ())), preferred_element_type=F32) * intra
        state = s_ref[...]
        out = (jnp.dot(scores.astype(BF16), vc, preferred_element_type=F32)
               + jnp.dot((qc.astype(F32) * q_decay).astype(BF16), state.astype(BF16),
                         preferred_element_type=F32))
        kt = (kc.astype(F32) * k_decay).T.astype(BF16)
        s_ref[...] = state * chunk_decay + jnp.dot(kt, vc, preferred_element_type=F32)
        return out

    def step(t, carry):
        c_fwd = t
        c_bwd = jnp.where(t < nc_ctx, nc_ctx - 1 - t, nc_tot - 1 - (t - nc_ctx))
        out_f = advance(c_fwd, sf_ref, *consts[0])
        out_b = advance(c_bwd, sb_ref, *consts[1])
        oacc_ref[chunk_rows(c_fwd), :] += out_f
        oacc_ref[chunk_rows(c_bwd), :] += out_b
        return carry

    lax.fori_loop(0, nc_tot, step, 0)

    def finish(c, carry):
        rows = chunk_rows(c)
        o = oacc_ref[rows, :]
        o = o * lax.rsqrt(jnp.mean(o * o, axis=-1, keepdims=True) + EPS) * gn_ref[...]
        g = g_ref[0, rows, :].astype(F32)
        o_ref[0, rows, :] = ((g * _sigmoid(g)) * o).astype(BF16)
        return carry

    lax.fori_loop(0, nc_tot, finish, 0)


def _retention(proj, log_gamma, gn, *, b, l_tot, lc, d_model):
    dk = d_model // RET_HEADS
    dv = 2 * d_model // RET_HEADS
    proj3 = proj.reshape(b, l_tot, proj.shape[-1])
    k_blk0 = d_model // dk
    v_blk0 = 2 * d_model // dv
    g_blk0 = 4 * d_model // dv
    body = functools.partial(_ret_body, nc_ctx=lc // RET_CHUNK, nc_tot=l_tot // RET_CHUNK)
    return pl.pallas_call(
        body,
        grid=(b, RET_HEADS),
        in_specs=[pl.BlockSpec(memory_space=pltpu.SMEM),
                  pl.BlockSpec((1, l_tot, dk), lambda bi, hi: (bi, 0, hi)),
                  pl.BlockSpec((1, l_tot, dk), lambda bi, hi: (bi, 0, k_blk0 + hi)),
                  pl.BlockSpec((1, l_tot, dv), lambda bi, hi: (bi, 0, v_blk0 + hi)),
                  pl.BlockSpec((1, l_tot, dv), lambda bi, hi: (bi, 0, g_blk0 + hi)),
                  pl.BlockSpec((1, dv), lambda bi, hi: (0, hi))],
        out_specs=pl.BlockSpec((1, l_tot, dv), lambda bi, hi: (bi, 0, hi)),
        out_shape=jax.ShapeDtypeStruct((b, l_tot, RET_HEADS * dv), BF16),
        scratch_shapes=[pltpu.VMEM((l_tot, dv), F32), pltpu.VMEM((dk, dv), F32), pltpu.VMEM((dk, dv), F32)],
        compiler_params=_params(2),
        name="retention",
    )(log_gamma, proj3, proj3, proj3, proj3, gn.reshape(1, RET_HEADS * dv))


def _retention_layer(x, h, cv, lv, w_in, decay, gn, w_out, cos, sin, *, b, l_tot, lc):
    m, d = x.shape
    proj = _ret_in_proj(h, w_in, cos, sin, l_tot=l_tot, d_model=d)
    log_gamma = jax.nn.log_sigmoid(decay.astype(F32))
    o = _retention(proj, log_gamma, gn, b=b, l_tot=l_tot, lc=lc, d_model=d).reshape(m, 2 * d)
    return _residual_matmul(o, w_out, jnp.zeros((d,), F32), x, cv, lv, l_tot=l_tot, lc=lc, r_gate=2,
                            tm=_row_tile(l_tot, 1088), tn=512, name="ret_out_proj")


def _mla_in_proj(h, w_in_pad, qa_gain, kva_gain, *, l_tot):
    m = h.shape[0]
    n = w_in_pad.shape[1]
    tm = _row_tile(l_tot, 1088)
    r = MLA_RANK

    def epilogue(acc, extra, outs):
        qg_ref, kg_ref = extra
        ql = acc[:, :r]
        kl = acc[:, r:2 * r]
        outs[0][...] = (ql * lax.rsqrt(jnp.mean(ql * ql, axis=-1, keepdims=True) + EPS)
                        * qg_ref[...]).astype(BF16)
        outs[1][...] = (kl * lax.rsqrt(jnp.mean(kl * kl, axis=-1, keepdims=True) + EPS)
                        * kg_ref[...]).astype(BF16)
        outs[2][...] = acc[:, 2 * r:]

    gspec = pl.BlockSpec((1, r), lambda i, j: (0, 0))
    return _matmul(h, w_in_pad, [qa_gain.reshape(1, r), kva_gain.reshape(1, r)], [gspec, gspec],
                   [jax.ShapeDtypeStruct((m, r), BF16), jax.ShapeDtypeStruct((m, r), BF16),
                    jax.ShapeDtypeStruct((m, LANES), F32)],
                   [pl.BlockSpec((tm, r), lambda i, j: (i, 0)),
                    pl.BlockSpec((tm, r), lambda i, j: (i, 0)),
                    pl.BlockSpec((tm, LANES), lambda i, j: (i, 0))],
                   epilogue, tm=tm, tn=n, name="mla_in_proj")


def _mla_q_proj(qn, wq_pad, q_gain_pad, tabs, *, l_tot):
    m = qn.shape[0]
    n = wq_pad.shape[1]
    tm = _row_tile(l_tot, 1088)
    tn = 1024
    tpb = l_tot // tm
    hw = 2 * LANES
    width = MLA_NOPE + MLA_ROPE
    scale = width ** -0.5 * LOG2_E

    def epilogue(acc, extra, outs):
        g_ref, c_ref, s1_ref, s2_ref = extra
        c, s1, s2 = c_ref[...], s1_ref[...], s2_ref[...]
        for hh in range(tn // hw):
            xs = acc[:, hh * hw:(hh + 1) * hw]
            inv = lax.rsqrt(jnp.sum(xs * xs, axis=-1, keepdims=True) * (1.0 / width) + EPS)
            y = xs * inv * g_ref[...] * scale
            outs[0][:, hh * hw:hh * hw + LANES] = y[:, :LANES].astype(BF16)
            outs[0][:, hh * hw + LANES:(hh + 1) * hw] = _rope3(y[:, LANES:], c, s1, s2).astype(BF16)

    tspec = pl.BlockSpec((tm, LANES), lambda i, j: (lax.rem(i, tpb), 0))
    return _matmul(qn, wq_pad, [q_gain_pad] + list(tabs),
                   [pl.BlockSpec((1, hw), lambda i, j: (0, 0)), tspec, tspec, tspec],
                   [jax.ShapeDtypeStruct((m, n), BF16)],
                   [pl.BlockSpec((tm, tn), lambda i, j: (i, j))],
                   epilogue, tm=tm, tn=tn, name="mla_q_proj")[0]


def _mla_kv_proj(kvn, w_kv, kr, k_gain_pad, tabs, *, l_tot):
    m = kvn.shape[0]
    n = w_kv.shape[1]
    tm = _row_tile(l_tot, 1088)
    tn = 1024
    tpb = l_tot // tm
    hw = MLA_NOPE + MLA_V
    width = MLA_NOPE + MLA_ROPE

    def epilogue(acc, extra, outs):
        kr_ref, g_ref, c_ref, s1_ref, s2_ref = extra
        kr_raw = kr_ref[...]
        ss_r = jnp.sum(kr_raw * kr_raw, axis=-1, keepdims=True)
        kr_rot = _rope3(kr_raw * g_ref[:, LANES:], c_ref[...], s1_ref[...], s2_ref[...])
        g_nope = g_ref[:, :LANES]
        for hh in range(tn // hw):
            kn = acc[:, hh * hw:hh * hw + MLA_NOPE]
            inv = lax.rsqrt((jnp.sum(kn * kn, axis=-1, keepdims=True) + ss_r) * (1.0 / width) + EPS)
            outs[0][:, hh * hw:hh * hw + LANES] = (kn * inv * g_nope).astype(BF16)
            outs[0][:, hh * hw + LANES:(hh + 1) * hw] = (kr_rot * inv).astype(BF16)
            outs[1][:, hh * MLA_V:(hh + 1) * MLA_V] = acc[:, hh * hw + MLA_NOPE:(hh + 1) * hw].astype(BF16)

    tspec = pl.BlockSpec((tm, LANES), lambda i, j: (lax.rem(i, tpb), 0))
    return _matmul(kvn, w_kv, [kr, k_gain_pad] + list(tabs),
                   [pl.BlockSpec((tm, LANES), lambda i, j: (i, 0)),
                    pl.BlockSpec((1, 2 * LANES), lambda i, j: (0, 0)), tspec, tspec, tspec],
                   [jax.ShapeDtypeStruct((m, n), BF16), jax.ShapeDtypeStruct((m, n // 2), BF16)],
                   [pl.BlockSpec((tm, tn), lambda i, j: (i, j)),
                    pl.BlockSpec((tm, tn // 2), lambda i, j: (i, j))],
                   epilogue, tm=tm, tn=tn, name="mla_kv_proj")


def _softmax_pv(q, k, v):
    s = lax.dot_general(q, k, (((1,), (1,)), ((), ())), preferred_element_type=F32)
    p = jnp.exp2(s - jnp.max(s, axis=-1, keepdims=True))
    den = jnp.sum(p, axis=-1, keepdims=True)
    return jnp.dot(p.astype(BF16), v, preferred_element_type=F32) / den


def _mla_attn_body(q_ref, k_ref, v_ref, o_ref, *, lc, tq, n_q):
    o_ref[0, 0:lc, :] = _softmax_pv(q_ref[0, 0:lc, :], k_ref[0, 0:lc, :], v_ref[0, 0:lc, :]).astype(BF16)

    def step(t, carry):
        rows = pl.ds(pl.multiple_of(lc + t * tq, tq), tq)
        o_ref[0, rows, :] = _softmax_pv(q_ref[0, rows, :], k_ref[0], v_ref[0]).astype(BF16)
        return carry

    lax.fori_loop(0, n_q, step, 0, unroll=2)


def _mla_attention(q, k, v, *, b, l_tot, lc):
    hw = 2 * LANES
    tq = 256
    assert lc % tq == 0 and (l_tot - lc) % tq == 0
    q3 = q.reshape(b, l_tot, MLA_HEADS * hw)
    k3 = k.reshape(b, l_tot, MLA_HEADS * hw)
    v3 = v.reshape(b, l_tot, MLA_HEADS * MLA_V)
    body = functools.partial(_mla_attn_body, lc=lc, tq=tq, n_q=(l_tot - lc) // tq)
    return pl.pallas_call(
        body,
        grid=(b, MLA_HEADS),
        in_specs=[pl.BlockSpec((1, l_tot, hw), lambda bi, hi: (bi, 0, hi)),
                  pl.BlockSpec((1, l_tot, hw), lambda bi, hi: (bi, 0, hi)),
                  pl.BlockSpec((1, l_tot, MLA_V), lambda bi, hi: (bi, 0, hi))],
        out_specs=pl.BlockSpec((1, l_tot, MLA_V), lambda bi, hi: (bi, 0, hi)),
        out_shape=jax.ShapeDtypeStruct((b, l_tot, MLA_HEADS * MLA_V), BF16),
        compiler_params=_params(2),
        name="mla_attention",
    )(q3, k3, v3)


def _mla_layer(x, h, cv, lv, w_in, qa_gain, kva_gain, w_q_b, w_kv_b, q_gain, k_gain, w_out, tabs,
               *, b, l_tot, lc):
    m, d = x.shape
    width = MLA_NOPE + MLA_ROPE
    hw = 2 * LANES
    w_in_pad = jnp.pad(w_in, ((0, 0), (0, 2 * MLA_RANK + LANES - w_in.shape[1])))
    wq_pad = jnp.pad(w_q_b.reshape(MLA_RANK, MLA_HEADS, width),
                     ((0, 0), (0, 0), (0, hw - width))).reshape(MLA_RANK, MLA_HEADS * hw)
    q_gain_pad = jnp.pad(q_gain, (0, hw - width)).reshape(1, hw)
    k_gain_pad = jnp.pad(k_gain, (0, hw - width)).reshape(1, hw)
    qn, kvn, kr = _mla_in_proj(h, w_in_pad, qa_gain, kva_gain, l_tot=l_tot)
    q = _mla_q_proj(qn, wq_pad, q_gain_pad, tabs, l_tot=l_tot)
    k, v = _mla_kv_proj(kvn, w_kv_b, kr, k_gain_pad, tabs, l_tot=l_tot)
    o = _mla_attention(q, k, v, b=b, l_tot=l_tot, lc=lc).reshape(m, MLA_HEADS * MLA_V)
    return _residual_matmul(o, w_out, jnp.zeros((d,), F32), x, cv, lv, l_tot=l_tot, lc=lc,
                            r_gate=2, tm=_row_tile(l_tot, 2176), tn=512, name="mla_out_proj")


def _conv_gate_body(bg_ref, cg_ref, u_ref, w_ref, cb_ref, o_ref, z_ref, *, l_tot, lc):
    pad = 8
    tn = o_ref.shape[-1]
    z_ref[0:pad, :] = jnp.zeros((pad, tn), F32)
    z_ref[pad + l_tot:, :] = jnp.zeros((pad, tn), F32)
    z_ref[pad:pad + l_tot, :] = cg_ref[0].astype(F32) * u_ref[0].astype(F32)
    rows = lax.broadcasted_iota(jnp.int32, (l_tot, 1), 0)
    z_prev = jnp.where(rows == lc, 0.0, z_ref[pad - 1:pad - 1 + l_tot, :])
    z_next = jnp.where(rows == lc - 1, 0.0, z_ref[pad + 1:pad + 1 + l_tot, :])
    z_mid = z_ref[pad:pad + l_tot, :]
    conv = z_prev * w_ref[0:1, :] + z_mid * w_ref[1:2, :] + z_next * w_ref[2:3, :] + cb_ref[...]
    o_ref[0] = (bg_ref[0].astype(F32) * conv).astype(BF16)


def _conv_gate(proj, conv_w, conv_b, *, b, l_tot, lc, d):
    tn = 256
    nb = d // tn
    proj3 = proj.reshape(b, l_tot, 3 * d)
    body = functools.partial(_conv_gate_body, l_tot=l_tot, lc=lc)
    return pl.pallas_call(
        body,
        grid=(b, nb),
        in_specs=[pl.BlockSpec((1, l_tot, tn), lambda bi, j: (bi, 0, j)),
                  pl.BlockSpec((1, l_tot, tn), lambda bi, j: (bi, 0, nb + j)),
                  pl.BlockSpec((1, l_tot, tn), lambda bi, j: (bi, 0, 2 * nb + j)),
                  pl.BlockSpec((3, tn), lambda bi, j: (0, j)),
                  pl.BlockSpec((1, tn), lambda bi, j: (0, j))],
        out_specs=pl.BlockSpec((1, l_tot, tn), lambda bi, j: (bi, 0, j)),
        out_shape=jax.ShapeDtypeStruct((b, l_tot, d), BF16),
        scratch_shapes=[pltpu.VMEM((l_tot + 16, tn), F32)],
        compiler_params=_params(2),
        name="conv_gate",
    )(proj3, proj3, proj3, conv_w, conv_b.reshape(1, d))


def _shortconv_layer(x, h, cv, lv, w_in, conv_w, conv_b, w_out, *, b, l_tot, lc):
    m, d = x.shape
    tm = _row_tile(l_tot, 2176)
    proj = _matmul(h, w_in, [], [], [jax.ShapeDtypeStruct((m, 3 * d), BF16)],
                   [pl.BlockSpec((tm, 512), lambda i, j: (i, j))],
                   _plain_epilogue(BF16), tm=tm, tn=512, name="sc_in_proj")[0]
    a = _conv_gate(proj, conv_w, conv_b, b=b, l_tot=l_tot, lc=lc, d=d).reshape(m, d)
    return _residual_matmul(a, w_out, jnp.zeros((d,), F32), x, cv, lv, l_tot=l_tot, lc=lc,
                            r_gate=2, tm=tm, tn=512, name="sc_out_proj")


def _split_hi_lo(x):
    hi = x.astype(BF16)
    lo = (x - hi.astype(F32)).astype(BF16)
    return hi, lo


def _head_rms_scale(x, seg_ref, segt_ref):
    hi, lo = _split_hi_lo(x * x)
    seg = seg_ref[...]
    ss = jnp.dot(hi, seg, preferred_element_type=F32) + jnp.dot(lo, seg, preferred_element_type=F32)
    inv = lax.rsqrt(ss * (1.0 / GQA_HEAD_DIM) + EPS)
    ihi, ilo = _split_hi_lo(inv)
    segt = segt_ref[...]
    return jnp.dot(ihi, segt, preferred_element_type=F32) + jnp.dot(ilo, segt, preferred_element_type=F32)


def _gqa_in_proj(h, w_in, b_in, qg_t, kg_t, seg, segt, tabs, *, l_tot):
    m = h.shape[0]
    n = w_in.shape[1]
    tm = _row_tile(l_tot, 1088)
    tn = 512
    tpb = l_tot // tm
    n_q_tiles = GQA_HEADS * GQA_HEAD_DIM // tn
    kv_w = GQA_KV_HEADS * GQA_HEAD_DIM
    assert n == n_q_tiles * tn + 2 * kv_w and 2 * kv_w == tn
    scale = GQA_HEAD_DIM ** -0.5 * LOG2_E

    def norm_rope(xs, g, extra, mult):
        seg_ref, segt_ref, c_ref, s1_ref, s2_ref = extra[3:8]
        y = xs * _head_rms_scale(xs, seg_ref, segt_ref) * g
        c, s1, s2 = c_ref[...], s1_ref[...], s2_ref[...]
        return [_rope3(y[:, a * LANES:(a + 1) * LANES], c, s1, s2) * mult
                for a in range(xs.shape[1] // LANES)]

    def dup_heads(slab, lo_mask):
        swapped = pltpu.roll(slab, GQA_HEAD_DIM, axis=1)
        return jnp.where(lo_mask, slab, swapped), jnp.where(lo_mask, swapped, slab)

    def epilogue(acc, extra, outs):
        b_ref, qg_ref, kg_ref = extra[0:3]
        j = pl.program_id(1)
        acc = acc + b_ref[...]

        @pl.when(j < n_q_tiles)
        def _():
            for a, slab in enumerate(norm_rope(acc, qg_ref[...], extra, scale)):
                outs[0][:, a * LANES:(a + 1) * LANES] = slab.astype(BF16)

        @pl.when(j == n_q_tiles)
        def _():
            lo_mask = lax.broadcasted_iota(jnp.int32, (tm, LANES), 1) < GQA_HEAD_DIM
            kpad = jnp.concatenate([acc[:, :kv_w], acc[:, :kv_w]], axis=1)
            k_slabs = norm_rope(kpad, kg_ref[...], extra, 1.0)[:kv_w // LANES]
            for a in range(kv_w // LANES):
                ka, kb = dup_heads(k_slabs[a], lo_mask)
                outs[1][:, (2 * a) * LANES:(2 * a + 1) * LANES] = ka.astype(BF16)
                outs[1][:, (2 * a + 1) * LANES:(2 * a + 2) * LANES] = kb.astype(BF16)
                va, vb = dup_heads(acc[:, kv_w + a * LANES:kv_w + (a + 1) * LANES], lo_mask)
                outs[2][:, (2 * a) * LANES:(2 * a + 1) * LANES] = va.astype(BF16)
                outs[2][:, (2 * a + 1) * LANES:(2 * a + 2) * LANES] = vb.astype(BF16)

    tspec = pl.BlockSpec((tm, LANES), lambda i, j: (lax.rem(i, tpb), 0))
    full = lambda shape: pl.BlockSpec(shape, lambda i, j: (0, 0))
    return _matmul(
        h, w_in, [b_in.reshape(1, n), qg_t, kg_t, seg, segt] + list(tabs),
        [pl.BlockSpec((1, tn), lambda i, j: (0, j)), full((1, tn)), full((1, tn)),
         full((tn, LANES)), full((LANES, tn)), tspec, tspec, tspec],
        [jax.ShapeDtypeStruct((m, n_q_tiles * tn), BF16),
         jax.ShapeDtypeStruct((m, GQA_KV_HEADS * LANES), BF16),
         jax.ShapeDtypeStruct((m, GQA_KV_HEADS * LANES), BF16)],
        [pl.BlockSpec((tm, tn), lambda i, j: (i, jnp.minimum(j, n_q_tiles - 1))),
         pl.BlockSpec((tm, GQA_KV_HEADS * LANES), lambda i, j: (i, 0)),
         pl.BlockSpec((tm, GQA_KV_HEADS * LANES), lambda i, j: (i, 0))],
        epilogue, tm=tm, tn=tn, name="gqa_in_proj")


def _gqa_attn_body(sink_ref, q_ref, k_ref, v_ref, o_ref, *, lc, nb):
    g = pl.program_id(1)
    r = WINDOW
    n_slab = GQA_GROUP // 2
    rows_all = GQA_GROUP * r
    row_head = lax.broadcasted_iota(jnp.int32, (rows_all, 1), 0) // r
    sink = jnp.zeros((rows_all, 1), F32)
    for hh in range(GQA_GROUP):
        sink = jnp.where(row_head == hh, sink_ref[g * GQA_GROUP + hh] * LOG2_E, sink)
    lo_mask = lax.broadcasted_iota(jnp.int32, (r, LANES), 1) < GQA_HEAD_DIM
    ii = lax.rem(lax.broadcasted_iota(jnp.int32, (rows_all, r), 0), r)
    jj = lax.broadcasted_iota(jnp.int32, (rows_all, r), 1)
    k_ctx = k_ref[0, 0:lc, :]
    v_ctx = v_ref[0, 0:lc, :]

    def stack_q(rows):
        parts = []
        for a in range(n_slab):
            qa = q_ref[0, rows, a * LANES:(a + 1) * LANES]
            zero = jnp.zeros_like(qa)
            parts += [jnp.where(lo_mask, qa, zero), jnp.where(lo_mask, zero, qa)]
        return jnp.concatenate(parts, axis=0)

    def attend(rows, blocks):
        qs = stack_q(rows)
        scores = []
        for kb, _, mask in blocks:
            s = lax.dot_general(qs, kb, (((1,), (1,)), ((), ())), preferred_element_type=F32)
            scores.append(s if mask is None else jnp.where(mask, s, NEG_INF))
        slabs = lambda a: [a[:, c * LANES:(c + 1) * LANES] for c in range(a.shape[1] // LANES)]
        mx = functools.reduce(jnp.maximum, [sl for s in scores for sl in slabs(s)])
        mx = jnp.maximum(sink, jnp.max(mx, axis=-1, keepdims=True))
        psum = jnp.zeros((rows_all, LANES), F32)
        acc = jnp.zeros((rows_all, LANES), F32)
        for s, (_, vb, _) in zip(scores, blocks):
            p = jnp.exp2(s - mx)
            psum = functools.reduce(jnp.add, slabs(p), psum)
            acc = acc + jnp.dot(p.astype(BF16), vb, preferred_element_type=F32)
        o = acc / (jnp.exp2(sink - mx) + jnp.sum(psum, axis=-1, keepdims=True))
        for a in range(n_slab):
            oa = jnp.where(lo_mask, o[(2 * a) * r:(2 * a + 1) * r], o[(2 * a + 1) * r:(2 * a + 2) * r])
            o_ref[0, rows, a * LANES:(a + 1) * LANES] = oa.astype(BF16)

    for cb in range(lc // r):
        attend(pl.ds(cb * r, r), [(k_ctx, v_ctx, None)])

    def step(t, carry):
        r_cur = pl.multiple_of(lc + t * r, r)
        r_prev = pl.multiple_of(lc + jnp.maximum(t - 1, 0) * r, r)
        r_next = pl.multiple_of(lc + jnp.minimum(t + 1, nb - 1) * r, r)
        blk = lambda r0: (k_ref[0, pl.ds(r0, r), :], v_ref[0, pl.ds(r0, r), :])
        m_prev = jnp.logical_and(jj >= ii, t > 0)
        m_next = jnp.logical_and(jj <= ii, t < nb - 1)
        attend(pl.ds(r_cur, r), [blk(r_prev) + (m_prev,), blk(r_cur) + (None,),
                                 blk(r_next) + (m_next,), (k_ctx, v_ctx, None)])
        return carry

    lax.fori_loop(0, nb, step, 0, unroll=2)


def _gqa_attention(q, k, v, sink, *, b, l_tot, lc):
    gw = GQA_GROUP * GQA_HEAD_DIM
    q3 = q.reshape(b, l_tot, GQA_HEADS * GQA_HEAD_DIM)
    k3 = k.reshape(b, l_tot, GQA_KV_HEADS * LANES)
    v3 = v.reshape(b, l_tot, GQA_KV_HEADS * LANES)
    body = functools.partial(_gqa_attn_body, lc=lc, nb=(l_tot - lc) // WINDOW)
    return pl.pallas_call(
        body,
        grid=(b, GQA_KV_HEADS),
        in_specs=[pl.BlockSpec(memory_space=pltpu.SMEM),
                  pl.BlockSpec((1, l_tot, gw), lambda bi, gi: (bi, 0, gi)),
                  pl.BlockSpec((1, l_tot, LANES), lambda bi, gi: (bi, 0, gi)),
                  pl.BlockSpec((1, l_tot, LANES), lambda bi, gi: (bi, 0, gi))],
        out_specs=pl.BlockSpec((1, l_tot, gw), lambda bi, gi: (bi, 0, gi)),
        out_shape=jax.ShapeDtypeStruct((b, l_tot, GQA_HEADS * GQA_HEAD_DIM), BF16),
        compiler_params=_params(2),
        name="gqa_attention",
    )(sink, q3, k3, v3)


def _gqa_layer(x, h, cv, lv, w_in, b_in, q_gain, k_gain, sink, w_out, b_out, tabs, *, b, l_tot, lc):
    m, d = x.shape
    tn = 512
    heads_per_tile = tn // GQA_HEAD_DIM
    col = jnp.arange(tn)
    seg = (col[:, None] // GQA_HEAD_DIM == jnp.arange(LANES)[None, :]).astype(BF16)
    qg_t = jnp.tile(q_gain, heads_per_tile).reshape(1, tn)
    kg_t = jnp.tile(k_gain, heads_per_tile).reshape(1, tn)
    q, k, v = _gqa_in_proj(h, w_in, b_in, qg_t, kg_t, seg, seg.T, tabs, l_tot=l_tot)
    o = _gqa_attention(q, k, v, sink.astype(F32), b=b, l_tot=l_tot, lc=lc).reshape(m, d)
    return _residual_matmul(o, w_out, b_out, x, cv, lv, l_tot=l_tot, lc=lc, r_gate=2,
                            tm=_row_tile(l_tot, 2176), tn=512, name="gqa_out_proj")


def _pack_halves(yb):
    w = yb.shape[1] // 2
    lo = lax.bitcast_convert_type(yb[:, :w].astype(F32), jnp.uint32)
    hi = lax.bitcast_convert_type(yb[:, w:].astype(F32), jnp.uint32)
    return lax.shift_right_logical(lo, jnp.uint32(16)) | hi


def _unpack_halves(words):
    lo = lax.bitcast_convert_type(lax.shift_left(words, jnp.uint32(16)), F32).astype(BF16)
    hi = lax.bitcast_convert_type(words & jnp.uint32(0xFFFF0000), F32).astype(BF16)
    return lo, hi


def _ffn_route_body(x_ref, g_ref, cv_ref, lv_ref, rw_ref, rb_ref, fpk_ref, ri_ref, rg_ref, cnt_ref,
                    carry_ref, *, tm, tpb, lc, route_ctx, n_e):
    i = pl.program_id(0)

    @pl.when(i == 0)
    def _():
        carry_ref[...] = jnp.zeros_like(carry_ref)

    x = x_ref[...]
    y = x * lax.rsqrt(jnp.mean(x * x, axis=-1, keepdims=True) + EPS) * g_ref[...]
    is_ctx = _is_ctx_rows(i, tm, tpb, lc)
    y = y * (1.0 + _pick_mod(is_ctx, cv_ref, lv_ref, 4)) + _pick_mod(is_ctx, cv_ref, lv_ref, 3)
    yb = y.astype(BF16)
    fpk_ref[...] = _pack_halves(yb)

    lane = lax.broadcasted_iota(jnp.int32, (tm, LANES), 1)
    logits = jnp.dot(yb, rw_ref[...].astype(BF16), preferred_element_type=F32) + rb_ref[...]
    logits = jnp.where(lane < n_e, logits, NEG_INF)
    vals, idxs, hots = [], [], []
    for _ in range(TOP_K):
        mk = jnp.max(logits, axis=-1, keepdims=True)
        ik = jnp.min(jnp.where(logits == mk, lane, LANES), axis=-1, keepdims=True)
        hot = lane == ik
        vals.append(mk)
        idxs.append(ik)
        hots.append(hot)
        logits = jnp.where(hot, NEG_INF, logits)
    exps = [jnp.exp(v - vals[0]) for v in vals]
    den = exps[0] + exps[1] + exps[2] + exps[3]

    routed = jnp.ones_like(is_ctx) if route_ctx else jnp.logical_not(is_ctx)
    chosen = jnp.zeros((tm, LANES), F32)
    for hot in hots:
        chosen = chosen + jnp.where(jnp.logical_and(hot, routed), 1.0, 0.0)
    before = (lax.broadcasted_iota(jnp.int32, (tm, tm), 0)
              > lax.broadcasted_iota(jnp.int32, (tm, tm), 1)).astype(BF16)
    arrived = jnp.dot(before, chosen.astype(BF16), preferred_element_type=F32) + carry_ref[...]
    carry_ref[...] = carry_ref[...] + jnp.sum(chosen, axis=0, keepdims=True)
    cnt_ref[...] = carry_ref[...]

    ri = jnp.zeros((tm, LANES), jnp.int32)
    rg = jnp.zeros((tm, LANES), F32)
    for k in range(TOP_K):
        rank = jnp.sum(jnp.where(hots[k], arrived, 0.0), axis=-1, keepdims=True)
        ri = jnp.where(lane == k, idxs[k], ri)
        ri = jnp.where(lane == TOP_K + k, rank.astype(jnp.int32), ri)
        rg = jnp.where(lane == k, exps[k] / den, rg)
    ri_ref[...] = ri
    rg_ref[...] = rg


def _ffn_route(x, gain, cv, lv, router_w_pad, router_b_pad, layer, *, l_tot, lc, route_ctx, n_e):
    m, d = x.shape
    tm = _row_tile(l_tot, 544)
    tpb = l_tot // tm
    body = functools.partial(_ffn_route_body, tm=tm, tpb=tpb, lc=lc, route_ctx=route_ctx, n_e=n_e)
    return pl.pallas_call(
        body,
        grid=(m // tm,),
        in_specs=[pl.BlockSpec((tm, d), lambda i: (i, 0)),
                  pl.BlockSpec((1, d), lambda i: (0, 0)),
                  pl.BlockSpec((MOD_ROWS, d), lambda i: (0, 0)),
                  pl.BlockSpec((1, MOD_ROWS, d), lambda i: (i // tpb, 0, 0)),
                  pl.BlockSpec((None, d, LANES), lambda i: (layer, 0, 0)),
                  pl.BlockSpec((None, 1, LANES), lambda i: (layer, 0, 0))],
        out_specs=[pl.BlockSpec((tm, d // 2), lambda i: (i, 0)),
                   pl.BlockSpec((tm, LANES), lambda i: (i, 0)),
                   pl.BlockSpec((tm, LANES), lambda i: (i, 0)),
                   pl.BlockSpec((1, LANES), lambda i: (0, 0))],
        out_shape=[jax.ShapeDtypeStruct((m, d // 2), jnp.uint32),
                   jax.ShapeDtypeStruct((m, LANES), jnp.int32),
                   jax.ShapeDtypeStruct((m, LANES), F32),
                   jax.ShapeDtypeStruct((1, LANES), F32)],
        scratch_shapes=[pltpu.VMEM((1, LANES), F32)],
        compiler_params=_params(1),
        name="ffn_route",
    )(x, gain.reshape(1, d), cv, lv, router_w_pad, router_b_pad)


def _dispatch_body(dest_ref, f_ref, init_hbm, xr_hbm, sem, *, tm, tpb, ctx_tiles, route_ctx):
    del init_hbm
    i = pl.program_id(0)

    def scatter_rows():
        def start(r, carry):
            a0 = (i * tm + r) * TOP_K
            for k in range(TOP_K):
                pltpu.make_async_copy(f_ref.at[pl.ds(r, 1)], xr_hbm.at[pl.ds(dest_ref[a0 + k], 1)], sem).start()
            return carry

        lax.fori_loop(0, tm, start, 0, unroll=ISSUE_UNROLL)

        def wait(r, carry):
            for k in range(TOP_K):
                pltpu.make_async_copy(f_ref.at[pl.ds(0, 1)], xr_hbm.at[pl.ds(0, 1)], sem).wait()
            return carry

        lax.fori_loop(0, tm, wait, 0, unroll=ISSUE_UNROLL)

    if route_ctx:
        scatter_rows()
    else:
        pl.when(lax.rem(i, tpb) >= ctx_tiles)(scatter_rows)


def _dispatch(fpk, dest, n_rows, *, l_tot, lc, route_ctx):
    m, w = fpk.shape
    tm = 256
    assert lc % tm == 0 and l_tot % tm == 0
    body = functools.partial(_dispatch_body, tm=tm, tpb=l_tot // tm, ctx_tiles=lc // tm, route_ctx=route_ctx)
    return pl.pallas_call(
        body,
        grid_spec=pltpu.PrefetchScalarGridSpec(
            num_scalar_prefetch=1,
            grid=(m // tm,),
            in_specs=[pl.BlockSpec((tm, w), lambda i, dst: (i, 0)),
                      pl.BlockSpec(memory_space=pl.ANY)],
            out_specs=pl.BlockSpec(memory_space=pl.ANY),
            scratch_shapes=[pltpu.SemaphoreType.DMA(())]),
        out_shape=jax.ShapeDtypeStruct((n_rows, w), jnp.uint32),
        input_output_aliases={2: 0},
        compiler_params=_row_copy_params(),
        name="moe_dispatch",
    )(dest, fpk, jnp.zeros((n_rows, w), jnp.uint32))


def _moe_gu_body(be_ref, nu_ref, x_ref, wg_ref, wl_ref, bg_ref, bl_ref, o_ref, wgb_ref, wlb_ref):
    blk = pl.program_id(0)
    prev = be_ref[jnp.maximum(blk - 1, 0)]

    @pl.when(jnp.logical_or(blk == 0, be_ref[blk] != prev))
    def _():
        wgb_ref[...] = wg_ref[...].astype(BF16)
        wlb_ref[...] = wl_ref[...].astype(BF16)

    @pl.when(blk < nu_ref[0])
    def _():
        x_lo, x_hi = _unpack_halves(x_ref[...])
        w = x_lo.shape[1]
        h_glu = (jnp.dot(x_lo, wgb_ref[0:w, :], preferred_element_type=F32)
                 + jnp.dot(x_hi, wgb_ref[w:, :], preferred_element_type=F32) + bg_ref[...])
        h_lin = (jnp.dot(x_lo, wlb_ref[0:w, :], preferred_element_type=F32)
                 + jnp.dot(x_hi, wlb_ref[w:, :], preferred_element_type=F32) + bl_ref[...])
        h_glu = jnp.minimum(h_glu, SWIGLU_LIMIT)
        h_lin = jnp.clip(h_lin, -SWIGLU_LIMIT, SWIGLU_LIMIT)
        o_ref[...] = (h_glu * _sigmoid(SWIGLU_ALPHA * h_glu) * (h_lin + 1.0)).astype(BF16)

    @pl.when(blk >= nu_ref[0])
    def _():
        o_ref[...] = jnp.zeros_like(o_ref)


def _moe_down_body(be_ref, nu_ref, a_ref, w_ref, b_ref, o_ref, wb_ref):
    blk = pl.program_id(0)
    prev = be_ref[jnp.maximum(blk - 1, 0)]

    @pl.when(jnp.logical_or(blk == 0, be_ref[blk] != prev))
    def _():
        wb_ref[...] = w_ref[...].astype(BF16)

    @pl.when(blk < nu_ref[0])
    def _():
        o_ref[...] = jnp.dot(a_ref[...], wb_ref[...], preferred_element_type=F32) + b_ref[...]

    @pl.when(blk >= nu_ref[0])
    def _():
        o_ref[...] = jnp.zeros_like(o_ref)


def _moe_experts(x_rows, block_expert, n_used, w_gu, b_gu, w_down, b_down, layer):
    n_rows, w = x_rows.shape
    n_blocks = n_rows // MOE_TM
    depth, n_e, ff, d = w_down.shape
    b_gu4 = b_gu.reshape(depth, n_e, 1, 2 * ff)
    act = pl.pallas_call(
        _moe_gu_body,
        grid_spec=pltpu.PrefetchScalarGridSpec(
            num_scalar_prefetch=2,
            grid=(n_blocks,),
            in_specs=[pl.BlockSpec((MOE_TM, w), lambda i, be, nu: (i, 0)),
                      pl.BlockSpec((None, None, d, ff), lambda i, be, nu: (layer, be[i], 0, 0)),
                      pl.BlockSpec((None, None, d, ff), lambda i, be, nu: (layer, be[i], 0, 1)),
                      pl.BlockSpec((None, None, 1, ff), lambda i, be, nu: (layer, be[i], 0, 0)),
                      pl.BlockSpec((None, None, 1, ff), lambda i, be, nu: (layer, be[i], 0, 1))],
            out_specs=pl.BlockSpec((MOE_TM, ff), lambda i, be, nu: (i, 0)),
            scratch_shapes=[pltpu.VMEM((d, ff), BF16), pltpu.VMEM((d, ff), BF16)]),
        out_shape=jax.ShapeDtypeStruct((n_rows, ff), BF16),
        compiler_params=_params(1),
        name="moe_gate_up",
    )(block_expert, n_used, x_rows, w_gu, w_gu, b_gu4, b_gu4)
    return pl.pallas_call(
        _moe_down_body,
        grid_spec=pltpu.PrefetchScalarGridSpec(
            num_scalar_prefetch=2,
            grid=(n_blocks,),
            in_specs=[pl.BlockSpec((MOE_TM, ff), lambda i, be, nu: (i, 0)),
                      pl.BlockSpec((None, None, ff, d), lambda i, be, nu: (layer, be[i], 0, 0)),
                      pl.BlockSpec((None, None, 1, d), lambda i, be, nu: (layer, be[i], 0, 0))],
            out_specs=pl.BlockSpec((MOE_TM, d), lambda i, be, nu: (i, 0)),
            scratch_shapes=[pltpu.VMEM((ff, d), BF16)]),
        out_shape=jax.ShapeDtypeStruct((n_rows, d), F32),
        compiler_params=_params(1),
        name="moe_down",
    )(block_expert, n_used, act, w_down, b_down.reshape(depth, n_e, 1, d))


def _combine_body(dest_ref, x_ref, rg_ref, cv_ref, lv_ref, y_hbm, o_ref, buf, sem,
                  *, tt, n_tiles, tpb, ctx_tiles, lc, route_ctx):
    i = pl.program_id(0)

    def routed(tile):
        return (tile >= 0) if route_ctx else (lax.rem(tile, tpb) >= ctx_tiles)

    def row_copy(src, slot, k, r):
        return pltpu.make_async_copy(y_hbm.at[pl.ds(src, 1)], buf.at[slot, k, pl.ds(r, 1)], sem.at[slot])

    def issue(tile, slot):
        def start(r, carry):
            a0 = (tile * tt + r) * TOP_K
            for k in range(TOP_K):
                row_copy(dest_ref[a0 + k], slot, k, r).start()
            return carry

        lax.fori_loop(0, tt, start, 0, unroll=ISSUE_UNROLL)

    @pl.when(jnp.logical_and(i == 0, routed(i)))
    def _():
        issue(i, 0)

    nxt = jnp.minimum(i + 1, n_tiles - 1)

    @pl.when(jnp.logical_and(i + 1 < n_tiles, routed(nxt)))
    def _():
        issue(nxt, lax.rem(i + 1, 2))

    slot = lax.rem(i, 2)

    @pl.when(routed(i))
    def _():
        def wait(r, carry):
            for k in range(TOP_K):
                row_copy(0, slot, k, 0).wait()
            return carry

        lax.fori_loop(0, tt, wait, 0, unroll=ISSUE_UNROLL)
        rg = rg_ref[...]
        acc = rg[:, 0:1] * buf[slot, 0]
        for k in range(1, TOP_K):
            acc = acc + rg[:, k:k + 1] * buf[slot, k]
        gate = _pick_mod(_is_ctx_rows(i, tt, tpb, lc), cv_ref, lv_ref, 5)
        o_ref[...] = x_ref[...] + gate * acc

    @pl.when(jnp.logical_not(routed(i)))
    def _():
        o_ref[...] = x_ref[...]


def _combine(x, y_rows, rg, dest, cv, lv, *, l_tot, lc, route_ctx):
    m, d = x.shape
    tt = 128
    assert lc % tt == 0 and l_tot % tt == 0
    tpb = l_tot // tt
    n_tiles = m // tt
    body = functools.partial(_combine_body, tt=tt, n_tiles=n_tiles, tpb=tpb, ctx_tiles=lc // tt, lc=lc,
                             route_ctx=route_ctx)
    return pl.pallas_call(
        body,
        grid_spec=pltpu.PrefetchScalarGridSpec(
            num_scalar_prefetch=1,
            grid=(n_tiles,),
            in_specs=[pl.BlockSpec((tt, d), lambda i, dst: (i, 0)),
                      pl.BlockSpec((tt, LANES), lambda i, dst: (i, 0)),
                      pl.BlockSpec((MOD_ROWS, d), lambda i, dst: (0, 0)),
                      pl.BlockSpec((1, MOD_ROWS, d), lambda i, dst: (i // tpb, 0, 0)),
                      pl.BlockSpec(memory_space=pl.ANY)],
            out_specs=pl.BlockSpec((tt, d), lambda i, dst: (i, 0)),
            scratch_shapes=[pltpu.VMEM((2, TOP_K, tt, d), F32), pltpu.SemaphoreType.DMA((2,))]),
        out_shape=jax.ShapeDtypeStruct((m, d), F32),
        compiler_params=_row_copy_params(),
        name="moe_combine",
    )(dest, x, rg, cv, lv, y_rows)


def _moe_layer(x, gain, cv, lv, router_w_pad, router_b_pad, w_gu, b_gu, w_down, b_down, layer,
               *, l_tot, lc, route_ctx):
    m, d = x.shape
    n_e = w_gu.shape[1]
    fpk, ri, rg, cnt = _ffn_route(x, gain, cv, lv, router_w_pad, router_b_pad, layer,
                                  l_tot=l_tot, lc=lc, route_ctx=route_ctx, n_e=n_e)
    counts = cnt[0, :n_e].astype(jnp.int32)
    padded = (counts + MOE_TM - 1) // MOE_TM * MOE_TM
    pend = jnp.cumsum(padded)
    pstart = (pend - padded).astype(jnp.int32)
    n_blocks = -(-m * TOP_K // MOE_TM) + n_e
    n_rows = n_blocks * MOE_TM
    block_start = jnp.arange(n_blocks, dtype=jnp.int32) * MOE_TM
    block_expert = jnp.minimum(jnp.sum(block_start[:, None] >= pend[None, :], axis=1), n_e - 1).astype(jnp.int32)
    n_used = (pend[-1] // MOE_TM).astype(jnp.int32).reshape(1)
    dest = (pstart[ri[:, :TOP_K]] + ri[:, TOP_K:2 * TOP_K]).reshape(m * TOP_K)
    x_rows = _dispatch(fpk, dest, n_rows, l_tot=l_tot, lc=lc, route_ctx=route_ctx)
    y_rows = _moe_experts(x_rows, block_expert, n_used, w_gu, b_gu, w_down, b_down, layer)
    return _combine(x, y_rows, rg, dest, cv, lv, l_tot=l_tot, lc=lc, route_ctx=route_ctx)


def _rope_1d_tables(l_tot, half):
    inv = ROPE_BASE ** (-jnp.arange(half, dtype=F32) / half)
    ang = jnp.arange(l_tot, dtype=F32)[:, None] * inv
    return jnp.cos(ang), jnp.sin(ang)


def _axial_tables(l_tot, lc):
    n_freq = GQA_HEAD_DIM // 4
    t = jnp.maximum(jnp.arange(l_tot) - lc, 0)
    pos = jnp.stack([(t // GRID_W).astype(F32), (t % GRID_W).astype(F32)], axis=-1)
    inv_freq = ROPE_BASE ** (-jnp.arange(n_freq, dtype=F32) / n_freq)
    ang = pos[:, :, None] * inv_freq
    lane = jnp.arange(LANES)
    axis = (lane % GQA_HEAD_DIM) // (2 * n_freq)
    freq = lane % n_freq
    first_half = (lane % (2 * n_freq)) < n_freq
    is_lat = (jnp.arange(l_tot) >= lc)[:, None]
    cos = jnp.where(is_lat, jnp.cos(ang)[:, axis, freq], 1.0)
    sin = jnp.where(is_lat, jnp.sin(ang)[:, axis, freq], 0.0)
    return cos, jnp.where(first_half, -sin, 0.0), jnp.where(first_half, 0.0, sin)


def kernel(x, c, ctx, c_ctx, norm_mix, norm_ffn, mod_w, mod_b, ret_w_in, ret_decay, ret_gn, ret_w_out, mla_w_in, mla_q_a_norm, mla_kv_a_norm, mla_w_q_b, mla_w_kv_b, mla_q_norm, mla_k_norm, mla_w_out, sc_w_in, sc_conv_w, sc_conv_b, sc_w_out, gqa_w_in, gqa_b_in, gqa_q_norm, gqa_k_norm, gqa_sink, gqa_w_out, gqa_b_out, moe_router_w, moe_router_b, moe_w_gu, moe_b_gu, moe_w_down, moe_b_down):
    b, n_lat, d = x.shape
    lc = ctx.shape[1]
    l_tot = lc + n_lat
    m = b * l_tot
    depth = mod_w.shape[0]
    assert b + 1 <= MOD_ROWS and lc % 256 == 0 and n_lat % 256 == 0

    xs = jnp.concatenate([ctx, x], axis=1).reshape(m, d)
    cvec = jnp.zeros((MOD_ROWS, d), F32).at[:b].set(c).at[b].set(c_ctx)
    mod = _modulation(cvec, mod_w, mod_b).reshape(depth, MOD_ROWS, 6, d)
    pad_rows = ((0, 0), (0, MOD_ROWS - 6), (0, 0))
    ret_cos, ret_sin = _rope_1d_tables(l_tot, d // RET_HEADS // 2)
    tabs = _axial_tables(l_tot, lc)
    n_e = moe_router_w.shape[-1]
    router_w_pad = jnp.pad(moe_router_w, ((0, 0), (0, 0), (0, LANES - n_e)))
    router_b_pad = jnp.pad(moe_router_b, ((0, 0), (0, LANES - n_e))).reshape(depth, 1, LANES)

    for i in range(depth):
        kind, j = i % N_MIXERS, i // N_MIXERS
        cv = jnp.pad(mod[i, b], pad_rows[1:])
        lv = jnp.pad(mod[i, :b], pad_rows)
        lay = dict(b=b, l_tot=l_tot, lc=lc)
        h = _norm_mod(xs, norm_mix[i], cv, lv, l_tot=l_tot, lc=lc, r_shift=0, r_scale=1)
        if kind == 0:
            xs = _retention_layer(xs, h, cv, lv, ret_w_in[j], ret_decay[j], ret_gn[j], ret_w_out[j],
                                  ret_cos, ret_sin, **lay)
        elif kind == 1:
            xs = _mla_layer(xs, h, cv, lv, mla_w_in[j], mla_q_a_norm[j], mla_kv_a_norm[j], mla_w_q_b[j],
                            mla_w_kv_b[j], mla_q_norm[j], mla_k_norm[j], mla_w_out[j], tabs, **lay)
        elif kind == 2:
            xs = _shortconv_layer(xs, h, cv, lv, sc_w_in[j], sc_conv_w[j], sc_conv_b[j], sc_w_out[j], **lay)
        else:
            xs = _gqa_layer(xs, h, cv, lv, gqa_w_in[j], gqa_b_in[j], gqa_q_norm[j], gqa_k_norm[j],
                            gqa_sink[j], gqa_w_out[j], gqa_b_out[j], tabs, **lay)
        xs = _moe_layer(xs, norm_ffn[i], cv, lv, router_w_pad, router_b_pad, moe_w_gu, moe_b_gu,
                        moe_w_down, moe_b_down, i, l_tot=l_tot, lc=lc, route_ctx=i < depth - 1)
    return xs.reshape(b, l_tot, d)[:, lc:, :]
```

```python
import functools

import jax
import jax.numpy as jnp
from jax import lax
from jax.experimental import pallas as pl
from jax.experimental.pallas import tpu as pltpu

F32 = jnp.float32
BF16 = jnp.bfloat16

EPS = 1e-6
ROPE_BASE = 10000.0
NEG_INF = -1e30
LOG2_E = 1.4426950408889634
GRID_W = 64

RET_HEADS = 8
RET_CHUNK = 128
MLA_HEADS = 16
MLA_NOPE = 128
MLA_ROPE = 64
MLA_V = 128
MLA_RANK = 512
GQA_HEADS = 32
GQA_KV_HEADS = 4
GQA_GROUP = GQA_HEADS // GQA_KV_HEADS
GQA_HEAD_DIM = 64
WINDOW = 128
N_EXPERTS = 32
TOP_K = 4
SWIGLU_LIMIT = 7.0
SWIGLU_ALPHA = 1.702
N_MIXERS = 4

LANES = 128
VMEM_LIMIT = 56 * 1024 * 1024
MOE_TM = 256
MOD_ROWS = 8
ISSUE_UNROLL = 4


def _params(n_axes, vmem=VMEM_LIMIT):
    return pltpu.CompilerParams(dimension_semantics=("arbitrary",) * n_axes,
                                vmem_limit_bytes=vmem)


def _row_copy_params():
    return pltpu.CompilerParams(dimension_semantics=("arbitrary",), vmem_limit_bytes=VMEM_LIMIT,
                                disable_bounds_checks=True)


def _row_tile(l_tot, cap):
    best = 16
    for t in range(16, cap + 1, 16):
        if l_tot % t == 0:
            best = t
    return best


def _is_ctx_rows(i, tm, tpb, lc):
    rows = lax.rem(i, tpb) * tm + lax.broadcasted_iota(jnp.int32, (tm, 1), 0)
    return rows < lc


def _pick_mod(is_ctx, cv_ref, lv_ref, r):
    return jnp.where(is_ctx, cv_ref[r:r + 1, :], lv_ref[0, r:r + 1, :])


def _sigmoid(x):
    return 1.0 / (1.0 + jnp.exp(-x))


def _rope3(x, c, s1, s2):
    return x * c + pltpu.roll(x, LANES - 16, axis=1) * s1 + pltpu.roll(x, 16, axis=1) * s2


def _mod_body(c_ref, w_ref, b_ref, o_ref):
    c = c_ref[...]
    s = (c * _sigmoid(c)).astype(BF16)
    o_ref[0] = jnp.dot(s, w_ref[0].astype(BF16), preferred_element_type=F32) + b_ref[0]


def _modulation(cvec, mod_w, mod_b):
    depth, d, n = mod_w.shape
    tn = 2048
    return pl.pallas_call(
        _mod_body,
        grid=(depth, n // tn),
        in_specs=[pl.BlockSpec((MOD_ROWS, d), lambda l, j: (0, 0)),
                  pl.BlockSpec((1, d, tn), lambda l, j: (l, 0, j)),
                  pl.BlockSpec((1, 1, tn), lambda l, j: (l, 0, j))],
        out_specs=pl.BlockSpec((1, MOD_ROWS, tn), lambda l, j: (l, 0, j)),
        out_shape=jax.ShapeDtypeStruct((depth, MOD_ROWS, n), F32),
        compiler_params=_params(2),
        name="modulation",
    )(cvec, mod_w, mod_b.reshape(depth, 1, n))


def _norm_mod_body(x_ref, g_ref, cv_ref, lv_ref, o_ref, *, tm, tpb, lc, r_shift, r_scale):
    x = x_ref[...]
    y = x * lax.rsqrt(jnp.mean(x * x, axis=-1, keepdims=True) + EPS) * g_ref[...]
    is_ctx = _is_ctx_rows(pl.program_id(0), tm, tpb, lc)
    shift = _pick_mod(is_ctx, cv_ref, lv_ref, r_shift)
    scale = _pick_mod(is_ctx, cv_ref, lv_ref, r_scale)
    o_ref[...] = (y * (1.0 + scale) + shift).astype(BF16)


def _norm_mod(x, gain, cv, lv, *, l_tot, lc, r_shift, r_scale):
    m, d = x.shape
    tm = _row_tile(l_tot, 544)
    tpb = l_tot // tm
    body = functools.partial(_norm_mod_body, tm=tm, tpb=tpb, lc=lc, r_shift=r_shift, r_scale=r_scale)
    return pl.pallas_call(
        body,
        grid=(m // tm,),
        in_specs=[pl.BlockSpec((tm, d), lambda i: (i, 0)),
                  pl.BlockSpec((1, d), lambda i: (0, 0)),
                  pl.BlockSpec((MOD_ROWS, d), lambda i: (0, 0)),
                  pl.BlockSpec((1, MOD_ROWS, d), lambda i: (i // tpb, 0, 0))],
        out_specs=pl.BlockSpec((tm, d), lambda i: (i, 0)),
        out_shape=jax.ShapeDtypeStruct((m, d), BF16),
        compiler_params=_params(1),
        name="norm_mod",
    )(x, gain.reshape(1, d), cv, lv)


def _mm_body(*refs, n_extra, n_out, epilogue, prologue):
    a_ref, w_ref = refs[0], refs[1]
    extra = refs[2:2 + n_extra]
    outs = refs[2 + n_extra:2 + n_extra + n_out]
    scratch = refs[2 + n_extra + n_out:]
    a = a_ref[...] if prologue is None else prologue(a_ref, extra, scratch)
    acc = jnp.dot(a, w_ref[...].astype(BF16), preferred_element_type=F32)
    epilogue(acc, extra, outs)


def _matmul(a, w, extras, extra_specs, out_shapes, out_specs, epilogue, *, tm, tn, name,
            prologue=None, scratch_shapes=()):
    m, k = a.shape
    n = w.shape[1]
    body = functools.partial(_mm_body, n_extra=len(extras), n_out=len(out_shapes),
                             epilogue=epilogue, prologue=prologue)
    return pl.pallas_call(
        body,
        grid=(m // tm, n // tn),
        in_specs=[pl.BlockSpec((tm, k), lambda i, j: (i, 0)),
                  pl.BlockSpec((k, tn), lambda i, j: (0, j))] + list(extra_specs),
        out_specs=list(out_specs),
        out_shape=list(out_shapes),
        scratch_shapes=list(scratch_shapes),
        compiler_params=_params(2),
        name=name,
    )(a, w, *extras)


def _plain_epilogue(out_dtype):
    def epilogue(acc, extra, outs):
        outs[0][...] = acc.astype(out_dtype)
    return epilogue


def _bias_epilogue(out_dtype):
    def epilogue(acc, extra, outs):
        outs[0][...] = (acc + extra[0][...]).astype(out_dtype)
    return epilogue


def _residual_matmul(a, w, bias, x_res, cv, lv, *, l_tot, lc, r_gate, tm, tn, name,
                     prologue=None, pro_extras=(), pro_specs=(), scratch_shapes=()):
    m = a.shape[0]
    n = w.shape[1]
    tpb = l_tot // tm
    n_pro = len(pro_extras)

    def epilogue(acc, extra, outs):
        b_ref, x_ref, cv_ref, lv_ref = extra[n_pro:n_pro + 4]
        is_ctx = _is_ctx_rows(pl.program_id(0), tm, tpb, lc)
        gate = _pick_mod(is_ctx, cv_ref, lv_ref, r_gate)
        outs[0][...] = x_ref[...] + gate * (acc + b_ref[...])

    extras = list(pro_extras) + [bias.reshape(1, n), x_res, cv, lv]
    specs = list(pro_specs) + [
        pl.BlockSpec((1, tn), lambda i, j: (0, j)),
        pl.BlockSpec((tm, tn), lambda i, j: (i, j)),
        pl.BlockSpec((MOD_ROWS, tn), lambda i, j: (0, j)),
        pl.BlockSpec((1, MOD_ROWS, tn), lambda i, j: (i // tpb, 0, j))]
    return _matmul(a, w, extras, specs,
                   [jax.ShapeDtypeStruct((m, n), F32)],
                   [pl.BlockSpec((tm, tn), lambda i, j: (i, j))],
                   epilogue, tm=tm, tn=tn, name=name, prologue=prologue,
                   scratch_shapes=scratch_shapes)[0]


def _ret_in_proj(h, w_in, cos, sin, *, l_tot, d_model):
    m = h.shape[0]
    n = w_in.shape[1]
    tm = _row_tile(l_tot, 2176)
    tn = 512
    tpb = l_tot // tm
    dk = d_model // RET_HEADS
    half = dk // 2
    n_q_tiles = d_model // tn
    k_scale = dk ** -0.5

    def epilogue(acc, extra, outs):
        cos_ref, sin_ref = extra
        j = pl.program_id(1)

        @pl.when(j < 2 * n_q_tiles)
        def _():
            c = cos_ref[...]
            s = sin_ref[...]
            mult = jnp.where(j >= n_q_tiles, k_scale, 1.0)
            for hh in range(tn // dk):
                x1 = acc[:, hh * dk:hh * dk + half]
                x2 = acc[:, hh * dk + half:(hh + 1) * dk]
                outs[0][:, hh * dk:hh * dk + half] = ((x1 * c - x2 * s) * mult).astype(BF16)
                outs[0][:, hh * dk + half:(hh + 1) * dk] = ((x2 * c + x1 * s) * mult).astype(BF16)

        @pl.when(j >= 2 * n_q_tiles)
        def _():
            outs[0][...] = acc.astype(BF16)

    tab_spec = pl.BlockSpec((tm, half), lambda i, j: (lax.rem(i, tpb), 0))
    return _matmul(h, w_in, [cos, sin], [tab_spec, tab_spec],
                   [jax.ShapeDtypeStruct((m, n), BF16)],
                   [pl.BlockSpec((tm, tn), lambda i, j: (i, j))],
                   epilogue, tm=tm, tn=tn, name="ret_in_proj")[0]


def _ret_body(lg_ref, q_ref, k_ref, v_ref, g_ref, gn_ref, o_ref, oacc_ref, sf_ref, sb_ref, *, nc_ctx, nc_tot):
    h = pl.program_id(1)
    ch = RET_CHUNK
    dv = v_ref.shape[-1]
    ii = lax.broadcasted_iota(jnp.int32, (ch, ch), 0).astype(F32)
    jj = lax.broadcasted_iota(jnp.int32, (ch, ch), 1).astype(F32)
    ri = lax.broadcasted_iota(jnp.int32, (ch, 1), 0).astype(F32)
    consts = []
    for direction in (0, 1):
        lg = lg_ref[direction, h]
        if direction == 0:
            diff = ii - jj
            q_decay = jnp.exp((ri + 1.0) * lg)
            k_decay = jnp.exp((ch - 1.0 - ri) * lg)
        else:
            diff = jj - ii
            q_decay = jnp.exp((ch - ri) * lg)
            k_decay = jnp.exp(ri * lg)
        intra = jnp.where(diff >= 0, jnp.exp(jnp.maximum(diff, 0.0) * lg), 0.0)
        chunk_decay = jnp.exp(jnp.full((1, dv), float(ch), F32) * lg)
        consts.append((intra, q_decay, k_decay, chunk_decay))
    sf_ref[...] = jnp.zeros_like(sf_ref)
    sb_ref[...] = jnp.zeros_like(sb_ref)
    oacc_ref[...] = jnp.zeros_like(oacc_ref)

    def chunk_rows(c):
        return pl.ds(pl.multiple_of(c * ch, ch), ch)

    def advance(c, s_ref, intra, q_decay, k_decay, chunk_decay):
        rows = chunk_rows(c)
        qc = q_ref[0, rows, :]
        kc = k_ref[0, rows, :]
        vc = v_ref[0, rows, :]
        scores = lax.dot_general(qc, kc, (((1,), (1,)), ((), ())), preferred_element_type=F32) * intra
        state = s_ref[...]
        out = (jnp.dot(scores.astype(BF16), vc, preferred_element_type=F32)
               + jnp.dot((qc.astype(F32) * q_decay).astype(BF16), state.astype(BF16),
                         preferred_element_type=F32))
        kt = (kc.astype(F32) * k_decay).T.astype(BF16)
        s_ref[...] = state * chunk_decay + jnp.dot(kt, vc, preferred_element_type=F32)
        return out

    def step(t, carry):
        c_fwd = t
        c_bwd = jnp.where(t < nc_ctx, nc_ctx - 1 - t, nc_tot - 1 - (t - nc_ctx))
        out_f = advance(c_fwd, sf_ref, *consts[0])
        out_b = advance(c_bwd, sb_ref, *consts[1])
        oacc_ref[chunk_rows(c_fwd), :] += out_f
        oacc_ref[chunk_rows(c_bwd), :] += out_b
        return carry

    lax.fori_loop(0, nc_tot, step, 0)

    def finish(c, carry):
        rows = chunk_rows(c)
        o = oacc_ref[rows, :]
        o = o * lax.rsqrt(jnp.mean(o * o, axis=-1, keepdims=True) + EPS) * gn_ref[...]
        g = g_ref[0, rows, :].astype(F32)
        o_ref[0, rows, :] = ((g * _sigmoid(g)) * o).astype(BF16)
        return carry

    lax.fori_loop(0, nc_tot, finish, 0)


def _retention(proj, log_gamma, gn, *, b, l_tot, lc, d_model):
    dk = d_model // RET_HEADS
    dv = 2 * d_model // RET_HEADS
    proj3 = proj.reshape(b, l_tot, proj.shape[-1])
    k_blk0 = d_model // dk
    v_blk0 = 2 * d_model // dv
    g_blk0 = 4 * d_model // dv
    body = functools.partial(_ret_body, nc_ctx=lc // RET_CHUNK, nc_tot=l_tot // RET_CHUNK)
    return pl.pallas_call(
        body,
        grid=(b, RET_HEADS),
        in_specs=[pl.BlockSpec(memory_space=pltpu.SMEM),
                  pl.BlockSpec((1, l_tot, dk), lambda bi, hi: (bi, 0, hi)),
                  pl.BlockSpec((1, l_tot, dk), lambda bi, hi: (bi, 0, k_blk0 + hi)),
                  pl.BlockSpec((1, l_tot, dv), lambda bi, hi: (bi, 0, v_blk0 + hi)),
                  pl.BlockSpec((1, l_tot, dv), lambda bi, hi: (bi, 0, g_blk0 + hi)),
                  pl.BlockSpec((1, dv), lambda bi, hi: (0, hi))],
        out_specs=pl.BlockSpec((1, l_tot, dv), lambda bi, hi: (bi, 0, hi)),
        out_shape=jax.ShapeDtypeStruct((b, l_tot, RET_HEADS * dv), BF16),
        scratch_shapes=[pltpu.VMEM((l_tot, dv), F32), pltpu.VMEM((dk, dv), F32), pltpu.VMEM((dk, dv), F32)],
        compiler_params=_params(2),
        name="retention",
    )(log_gamma, proj3, proj3, proj3, proj3, gn.reshape(1, RET_HEADS * dv))


def _retention_layer(x, h, cv, lv, w_in, decay, gn, w_out, cos, sin, *, b, l_tot, lc):
    m, d = x.shape
    proj = _ret_in_proj(h, w_in, cos, sin, l_tot=l_tot, d_model=d)
    log_gamma = jax.nn.log_sigmoid(decay.astype(F32))
    o = _retention(proj, log_gamma, gn, b=b, l_tot=l_tot, lc=lc, d_model=d).reshape(m, 2 * d)
    return _residual_matmul(o, w_out, jnp.zeros((d,), F32), x, cv, lv, l_tot=l_tot, lc=lc, r_gate=2,
                            tm=_row_tile(l_tot, 1088), tn=512, name="ret_out_proj")


def _mla_in_proj(h, w_in_pad, qa_gain, kva_gain, *, l_tot):
    m = h.shape[0]
    n = w_in_pad.shape[1]
    tm = _row_tile(l_tot, 1088)
    r = MLA_RANK

    def epilogue(acc, extra, outs):
        qg_ref, kg_ref = extra
        ql = acc[:, :r]
        kl = acc[:, r:2 * r]
        outs[0][...] = (ql * lax.rsqrt(jnp.mean(ql * ql, axis=-1, keepdims=True) + EPS)
                        * qg_ref[...]).astype(BF16)
        outs[1][...] = (kl * lax.rsqrt(jnp.mean(kl * kl, axis=-1, keepdims=True) + EPS)
                        * kg_ref[...]).astype(BF16)
        outs[2][...] = acc[:, 2 * r:]

    gspec = pl.BlockSpec((1, r), lambda i, j: (0, 0))
    return _matmul(h, w_in_pad, [qa_gain.reshape(1, r), kva_gain.reshape(1, r)], [gspec, gspec],
                   [jax.ShapeDtypeStruct((m, r), BF16), jax.ShapeDtypeStruct((m, r), BF16),
                    jax.ShapeDtypeStruct((m, LANES), F32)],
                   [pl.BlockSpec((tm, r), lambda i, j: (i, 0)),
                    pl.BlockSpec((tm, r), lambda i, j: (i, 0)),
                    pl.BlockSpec((tm, LANES), lambda i, j: (i, 0))],
                   epilogue, tm=tm, tn=n, name="mla_in_proj")


def _mla_q_proj(qn, wq_pad, q_gain_pad, tabs, *, l_tot):
    m = qn.shape[0]
    n = wq_pad.shape[1]
    tm = _row_tile(l_tot, 1088)
    tn = 1024
    tpb = l_tot // tm
    hw = 2 * LANES
    width = MLA_NOPE + MLA_ROPE
    scale = width ** -0.5 * LOG2_E

    def epilogue(acc, extra, outs):
        g_ref, c_ref, s1_ref, s2_ref = extra
        c, s1, s2 = c_ref[...], s1_ref[...], s2_ref[...]
        for hh in range(tn // hw):
            xs = acc[:, hh * hw:(hh + 1) * hw]
            inv = lax.rsqrt(jnp.sum(xs * xs, axis=-1, keepdims=True) * (1.0 / width) + EPS)
            y = xs * inv * g_ref[...] * scale
            outs[0][:, hh * hw:hh * hw + LANES] = y[:, :LANES].astype(BF16)
            outs[0][:, hh * hw + LANES:(hh + 1) * hw] = _rope3(y[:, LANES:], c, s1, s2).astype(BF16)

    tspec = pl.BlockSpec((tm, LANES), lambda i, j: (lax.rem(i, tpb), 0))
    return _matmul(qn, wq_pad, [q_gain_pad] + list(tabs),
                   [pl.BlockSpec((1, hw), lambda i, j: (0, 0)), tspec, tspec, tspec],
                   [jax.ShapeDtypeStruct((m, n), BF16)],
                   [pl.BlockSpec((tm, tn), lambda i, j: (i, j))],
                   epilogue, tm=tm, tn=tn, name="mla_q_proj")[0]


def _mla_kv_proj(kvn, w_kv, kr, k_gain_pad, tabs, *, l_tot):
    m = kvn.shape[0]
    n = w_kv.shape[1]
    tm = _row_tile(l_tot, 1088)
    tn = 1024
    tpb = l_tot // tm
    hw = MLA_NOPE + MLA_V
    width = MLA_NOPE + MLA_ROPE

    def epilogue(acc, extra, outs):
        kr_ref, g_ref, c_ref, s1_ref, s2_ref = extra
        kr_raw = kr_ref[...]
        ss_r = jnp.sum(kr_raw * kr_raw, axis=-1, keepdims=True)
        kr_rot = _rope3(kr_raw * g_ref[:, LANES:], c_ref[...], s1_ref[...], s2_ref[...])
        g_nope = g_ref[:, :LANES]
        for hh in range(tn // hw):
            kn = acc[:, hh * hw:hh * hw + MLA_NOPE]
            inv = lax.rsqrt((jnp.sum(kn * kn, axis=-1, keepdims=True) + ss_r) * (1.0 / width) + EPS)
            outs[0][:, hh * hw:hh * hw + LANES] = (kn * inv * g_nope).astype(BF16)
            outs[0][:, hh * hw + LANES:(hh + 1) * hw] = (kr_rot * inv).astype(BF16)
            outs[1][:, hh * MLA_V:(hh + 1) * MLA_V] = acc[:, hh * hw + MLA_NOPE:(hh + 1) * hw].astype(BF16)

    tspec = pl.BlockSpec((tm, LANES), lambda i, j: (lax.rem(i, tpb), 0))
    return _matmul(kvn, w_kv, [kr, k_gain_pad] + list(tabs),
                   [pl.BlockSpec((tm, LANES), lambda i, j: (i, 0)),
                    pl.BlockSpec((1, 2 * LANES), lambda i, j: (0, 0)), tspec, tspec, tspec],
                   [jax.ShapeDtypeStruct((m, n), BF16), jax.ShapeDtypeStruct((m, n // 2), BF16)],
                   [pl.BlockSpec((tm, tn), lambda i, j: (i, j)),
                    pl.BlockSpec((tm, tn // 2), lambda i, j: (i, j))],
                   epilogue, tm=tm, tn=tn, name="mla_kv_proj")


def _softmax_pv(q, k, v):
    s = lax.dot_general(q, k, (((1,), (1,)), ((), ())), preferred_element_type=F32)
    p = jnp.exp2(s - jnp.max(s, axis=-1, keepdims=True))
    den = jnp.sum(p, axis=-1, keepdims=True)
    return jnp.dot(p.astype(BF16), v, preferred_element_type=F32) / den


def _mla_attn_body(q_ref, k_ref, v_ref, o_ref, *, lc, tq, n_q):
    o_ref[0, 0:lc, :] = _softmax_pv(q_ref[0, 0:lc, :], k_ref[0, 0:lc, :], v_ref[0, 0:lc, :]).astype(BF16)

    def step(t, carry):
        rows = pl.ds(pl.multiple_of(lc + t * tq, tq), tq)
        o_ref[0, rows, :] = _softmax_pv(q_ref[0, rows, :], k_ref[0], v_ref[0]).astype(BF16)
        return carry

    lax.fori_loop(0, n_q, step, 0, unroll=4)


def _mla_attention(q, k, v, *, b, l_tot, lc):
    hw = 2 * LANES
    tq = 256
    assert lc % tq == 0 and (l_tot - lc) % tq == 0
    q3 = q.reshape(b, l_tot, MLA_HEADS * hw)
    k3 = k.reshape(b, l_tot, MLA_HEADS * hw)
    v3 = v.reshape(b, l_tot, MLA_HEADS * MLA_V)
    body = functools.partial(_mla_attn_body, lc=lc, tq=tq, n_q=(l_tot - lc) // tq)
    return pl.pallas_call(
        body,
        grid=(b, MLA_HEADS),
        in_specs=[pl.BlockSpec((1, l_tot, hw), lambda bi, hi: (bi, 0, hi)),
                  pl.BlockSpec((1, l_tot, hw), lambda bi, hi: (bi, 0, hi)),
                  pl.BlockSpec((1, l_tot, MLA_V), lambda bi, hi: (bi, 0, hi))],
        out_specs=pl.BlockSpec((1, l_tot, MLA_V), lambda bi, hi: (bi, 0, hi)),
        out_shape=jax.ShapeDtypeStruct((b, l_tot, MLA_HEADS * MLA_V), BF16),
        compiler_params=_params(2),
        name="mla_attention",
    )(q3, k3, v3)


def _mla_layer(x, h, cv, lv, w_in, qa_gain, kva_gain, w_q_b, w_kv_b, q_gain, k_gain, w_out, tabs,
               *, b, l_tot, lc):
    m, d = x.shape
    width = MLA_NOPE + MLA_ROPE
    hw = 2 * LANES
    w_in_pad = jnp.pad(w_in, ((0, 0), (0, 2 * MLA_RANK + LANES - w_in.shape[1])))
    wq_pad = jnp.pad(w_q_b.reshape(MLA_RANK, MLA_HEADS, width),
                     ((0, 0), (0, 0), (0, hw - width))).reshape(MLA_RANK, MLA_HEADS * hw)
    q_gain_pad = jnp.pad(q_gain, (0, hw - width)).reshape(1, hw)
    k_gain_pad = jnp.pad(k_gain, (0, hw - width)).reshape(1, hw)
    qn, kvn, kr = _mla_in_proj(h, w_in_pad, qa_gain, kva_gain, l_tot=l_tot)
    q = _mla_q_proj(qn, wq_pad, q_gain_pad, tabs, l_tot=l_tot)
    k, v = _mla_kv_proj(kvn, w_kv_b, kr, k_gain_pad, tabs, l_tot=l_tot)
    o = _mla_attention(q, k, v, b=b, l_tot=l_tot, lc=lc).reshape(m, MLA_HEADS * MLA_V)
    return _residual_matmul(o, w_out, jnp.zeros((d,), F32), x, cv, lv, l_tot=l_tot, lc=lc,
                            r_gate=2, tm=_row_tile(l_tot, 2176), tn=512, name="mla_out_proj")


def _conv_gate_body(bg_ref, cg_ref, u_ref, w_ref, cb_ref, o_ref, z_ref, *, l_tot, lc):
    pad = 8
    tn = o_ref.shape[-1]
    z_ref[0:pad, :] = jnp.zeros((pad, tn), F32)
    z_ref[pad + l_tot:, :] = jnp.zeros((pad, tn), F32)
    z_ref[pad:pad + l_tot, :] = cg_ref[0].astype(F32) * u_ref[0].astype(F32)
    rows = lax.broadcasted_iota(jnp.int32, (l_tot, 1), 0)
    z_prev = jnp.where(rows == lc, 0.0, z_ref[pad - 1:pad - 1 + l_tot, :])
    z_next = jnp.where(rows == lc - 1, 0.0, z_ref[pad + 1:pad + 1 + l_tot, :])
    z_mid = z_ref[pad:pad + l_tot, :]
    conv = z_prev * w_ref[0:1, :] + z_mid * w_ref[1:2, :] + z_next * w_ref[2:3, :] + cb_ref[...]
    o_ref[0] = (bg_ref[0].astype(F32) * conv).astype(BF16)


def _conv_gate(proj, conv_w, conv_b, *, b, l_tot, lc, d):
    tn = 256
    nb = d // tn
    proj3 = proj.reshape(b, l_tot, 3 * d)
    body = functools.partial(_conv_gate_body, l_tot=l_tot, lc=lc)
    return pl.pallas_call(
        body,
        grid=(b, nb),
        in_specs=[pl.BlockSpec((1, l_tot, tn), lambda bi, j: (bi, 0, j)),
                  pl.BlockSpec((1, l_tot, tn), lambda bi, j: (bi, 0, nb + j)),
                  pl.BlockSpec((1, l_tot, tn), lambda bi, j: (bi, 0, 2 * nb + j)),
                  pl.BlockSpec((3, tn), lambda bi, j: (0, j)),
                  pl.BlockSpec((1, tn), lambda bi, j: (0, j))],
        out_specs=pl.BlockSpec((1, l_tot, tn), lambda bi, j: (bi, 0, j)),
        out_shape=jax.ShapeDtypeStruct((b, l_tot, d), BF16),
        scratch_shapes=[pltpu.VMEM((l_tot + 16, tn), F32)],
        compiler_params=_params(2),
        name="conv_gate",
    )(proj3, proj3, proj3, conv_w, conv_b.reshape(1, d))


def _shortconv_layer(x, h, cv, lv, w_in, conv_w, conv_b, w_out, *, b, l_tot, lc):
    m, d = x.shape
    tm = _row_tile(l_tot, 2176)
    proj = _matmul(h, w_in, [], [], [jax.ShapeDtypeStruct((m, 3 * d), BF16)],
                   [pl.BlockSpec((tm, 512), lambda i, j: (i, j))],
                   _plain_epilogue(BF16), tm=tm, tn=512, name="sc_in_proj")[0]
    a = _conv_gate(proj, conv_w, conv_b, b=b, l_tot=l_tot, lc=lc, d=d).reshape(m, d)
    return _residual_matmul(a, w_out, jnp.zeros((d,), F32), x, cv, lv, l_tot=l_tot, lc=lc,
                            r_gate=2, tm=tm, tn=512, name="sc_out_proj")


def _split_hi_lo(x):
    hi = x.astype(BF16)
    lo = (x - hi.astype(F32)).astype(BF16)
    return hi, lo


def _head_rms_scale(x, seg_ref, segt_ref):
    hi, lo = _split_hi_lo(x * x)
    seg = seg_ref[...]
    ss = jnp.dot(hi, seg, preferred_element_type=F32) + jnp.dot(lo, seg, preferred_element_type=F32)
    inv = lax.rsqrt(ss * (1.0 / GQA_HEAD_DIM) + EPS)
    ihi, ilo = _split_hi_lo(inv)
    segt = segt_ref[...]
    return jnp.dot(ihi, segt, preferred_element_type=F32) + jnp.dot(ilo, segt, preferred_element_type=F32)


def _gqa_in_proj(h, w_in, b_in, qg_t, kg_t, seg, segt, tabs, *, l_tot):
    m = h.shape[0]
    n = w_in.shape[1]
    tm = _row_tile(l_tot, 1088)
    tn = 512
    tpb = l_tot // tm
    n_q_tiles = GQA_HEADS * GQA_HEAD_DIM // tn
    kv_w = GQA_KV_HEADS * GQA_HEAD_DIM
    assert n == n_q_tiles * tn + 2 * kv_w and 2 * kv_w == tn
    scale = GQA_HEAD_DIM ** -0.5 * LOG2_E

    def norm_rope(xs, g, extra, mult):
        seg_ref, segt_ref, c_ref, s1_ref, s2_ref = extra[3:8]
        y = xs * _head_rms_scale(xs, seg_ref, segt_ref) * g
        c, s1, s2 = c_ref[...], s1_ref[...], s2_ref[...]
        return [_rope3(y[:, a * LANES:(a + 1) * LANES], c, s1, s2) * mult
                for a in range(xs.shape[1] // LANES)]

    def dup_heads(slab, lo_mask):
        swapped = pltpu.roll(slab, GQA_HEAD_DIM, axis=1)
        return jnp.where(lo_mask, slab, swapped), jnp.where(lo_mask, swapped, slab)

    def epilogue(acc, extra, outs):
        b_ref, qg_ref, kg_ref = extra[0:3]
        j = pl.program_id(1)
        acc = acc + b_ref[...]

        @pl.when(j < n_q_tiles)
        def _():
            for a, slab in enumerate(norm_rope(acc, qg_ref[...], extra, scale)):
                outs[0][:, a * LANES:(a + 1) * LANES] = slab.astype(BF16)

        @pl.when(j == n_q_tiles)
        def _():
            lo_mask = lax.broadcasted_iota(jnp.int32, (tm, LANES), 1) < GQA_HEAD_DIM
            kpad = jnp.concatenate([acc[:, :kv_w], acc[:, :kv_w]], axis=1)
            k_slabs = norm_rope(kpad, kg_ref[...], extra, 1.0)[:kv_w // LANES]
            for a in range(kv_w // LANES):
                ka, kb = dup_heads(k_slabs[a], lo_mask)
                outs[1][:, (2 * a) * LANES:(2 * a + 1) * LANES] = ka.astype(BF16)
                outs[1][:, (2 * a + 1) * LANES:(2 * a + 2) * LANES] = kb.astype(BF16)
                va, vb = dup_heads(acc[:, kv_w + a * LANES:kv_w + (a + 1) * LANES], lo_mask)
                outs[2][:, (2 * a) * LANES:(2 * a + 1) * LANES] = va.astype(BF16)
                outs[2][:, (2 * a + 1) * LANES:(2 * a + 2) * LANES] = vb.astype(BF16)

    tspec = pl.BlockSpec((tm, LANES), lambda i, j: (lax.rem(i, tpb), 0))
    full = lambda shape: pl.BlockSpec(shape, lambda i, j: (0, 0))
    return _matmul(
        h, w_in, [b_in.reshape(1, n), qg_t, kg_t, seg, segt] + list(tabs),
        [pl.BlockSpec((1, tn), lambda i, j: (0, j)), full((1, tn)), full((1, tn)),
         full((tn, LANES)), full((LANES, tn)), tspec, tspec, tspec],
        [jax.ShapeDtypeStruct((m, n_q_tiles * tn), BF16),
         jax.ShapeDtypeStruct((m, GQA_KV_HEADS * LANES), BF16),
         jax.ShapeDtypeStruct((m, GQA_KV_HEADS * LANES), BF16)],
        [pl.BlockSpec((tm, tn), lambda i, j: (i, jnp.minimum(j, n_q_tiles - 1))),
         pl.BlockSpec((tm, GQA_KV_HEADS * LANES), lambda i, j: (i, 0)),
         pl.BlockSpec((tm, GQA_KV_HEADS * LANES), lambda i, j: (i, 0))],
        epilogue, tm=tm, tn=tn, name="gqa_in_proj")


def _gqa_attn_body(sink_ref, q_ref, k_ref, v_ref, o_ref, *, lc, nb):
    g = pl.program_id(1)
    r = WINDOW
    n_slab = GQA_GROUP // 2
    rows_all = GQA_GROUP * r
    row_head = lax.broadcasted_iota(jnp.int32, (rows_all, 1), 0) // r
    sink = jnp.zeros((rows_all, 1), F32)
    for hh in range(GQA_GROUP):
        sink = jnp.where(row_head == hh, sink_ref[g * GQA_GROUP + hh] * LOG2_E, sink)
    lo_mask = lax.broadcasted_iota(jnp.int32, (r, LANES), 1) < GQA_HEAD_DIM
    ii = lax.rem(lax.broadcasted_iota(jnp.int32, (rows_all, r), 0), r)
    jj = lax.broadcasted_iota(jnp.int32, (rows_all, r), 1)
    k_ctx = k_ref[0, 0:lc, :]
    v_ctx = v_ref[0, 0:lc, :]

    def stack_q(rows):
        parts = []
        for a in range(n_slab):
            qa = q_ref[0, rows, a * LANES:(a + 1) * LANES]
            zero = jnp.zeros_like(qa)
            parts += [jnp.where(lo_mask, qa, zero), jnp.where(lo_mask, zero, qa)]
        return jnp.concatenate(parts, axis=0)

    def attend(rows, blocks):
        qs = stack_q(rows)
        scores = []
        for kb, _, mask in blocks:
            s = lax.dot_general(qs, kb, (((1,), (1,)), ((), ())), preferred_element_type=F32)
            scores.append(s if mask is None else jnp.where(mask, s, NEG_INF))
        slabs = lambda a: [a[:, c * LANES:(c + 1) * LANES] for c in range(a.shape[1] // LANES)]
        mx = functools.reduce(jnp.maximum, [sl for s in scores for sl in slabs(s)])
        mx = jnp.maximum(sink, jnp.max(mx, axis=-1, keepdims=True))
        psum = jnp.zeros((rows_all, LANES), F32)
        acc = jnp.zeros((rows_all, LANES), F32)
        for s, (_, vb, _) in zip(scores, blocks):
            p = jnp.exp2(s - mx)
            psum = functools.reduce(jnp.add, slabs(p), psum)
            acc = acc + jnp.dot(p.astype(BF16), vb, preferred_element_type=F32)
        o = acc / (jnp.exp2(sink - mx) + jnp.sum(psum, axis=-1, keepdims=True))
        for a in range(n_slab):
            oa = jnp.where(lo_mask, o[(2 * a) * r:(2 * a + 1) * r], o[(2 * a + 1) * r:(2 * a + 2) * r])
            o_ref[0, rows, a * LANES:(a + 1) * LANES] = oa.astype(BF16)

    for cb in range(lc // r):
        attend(pl.ds(cb * r, r), [(k_ctx, v_ctx, None)])

    def step(t, carry):
        r_cur = pl.multiple_of(lc + t * r, r)
        r_prev = pl.multiple_of(lc + jnp.maximum(t - 1, 0) * r, r)
        r_next = pl.multiple_of(lc + jnp.minimum(t + 1, nb - 1) * r, r)
        blk = lambda r0: (k_ref[0, pl.ds(r0, r), :], v_ref[0, pl.ds(r0, r), :])
        m_prev = jnp.logical_and(jj >= ii, t > 0)
        m_next = jnp.logical_and(jj <= ii, t < nb - 1)
        attend(pl.ds(r_cur, r), [blk(r_prev) + (m_prev,), blk(r_cur) + (None,),
                                 blk(r_next) + (m_next,), (k_ctx, v_ctx, None)])
        return carry

    lax.fori_loop(0, nb, step, 0, unroll=2)


def _gqa_attention(q, k, v, sink, *, b, l_tot, lc):
    gw = GQA_GROUP * GQA_HEAD_DIM
    q3 = q.reshape(b, l_tot, GQA_HEADS * GQA_HEAD_DIM)
    k3 = k.reshape(b, l_tot, GQA_KV_HEADS * LANES)
    v3 = v.reshape(b, l_tot, GQA_KV_HEADS * LANES)
    body = functools.partial(_gqa_attn_body, lc=lc, nb=(l_tot - lc) // WINDOW)
    return pl.pallas_call(
        body,
        grid=(b, GQA_KV_HEADS),
        in_specs=[pl.BlockSpec(memory_space=pltpu.SMEM),
                  pl.BlockSpec((1, l_tot, gw), lambda bi, gi: (bi, 0, gi)),
                  pl.BlockSpec((1, l_tot, LANES), lambda bi, gi: (bi, 0, gi)),
                  pl.BlockSpec((1, l_tot, LANES), lambda bi, gi: (bi, 0, gi))],
        out_specs=pl.BlockSpec((1, l_tot, gw), lambda bi, gi: (bi, 0, gi)),
        out_shape=jax.ShapeDtypeStruct((b, l_tot, GQA_HEADS * GQA_HEAD_DIM), BF16),
        compiler_params=_params(2),
        name="gqa_attention",
    )(sink, q3, k3, v3)


def _gqa_layer(x, h, cv, lv, w_in, b_in, q_gain, k_gain, sink, w_out, b_out, tabs, *, b, l_tot, lc):
    m, d = x.shape
    tn = 512
    heads_per_tile = tn // GQA_HEAD_DIM
    col = jnp.arange(tn)
    seg = (col[:, None] // GQA_HEAD_DIM == jnp.arange(LANES)[None, :]).astype(BF16)
    qg_t = jnp.tile(q_gain, heads_per_tile).reshape(1, tn)
    kg_t = jnp.tile(k_gain, heads_per_tile).reshape(1, tn)
    q, k, v = _gqa_in_proj(h, w_in, b_in, qg_t, kg_t, seg, seg.T, tabs, l_tot=l_tot)
    o = _gqa_attention(q, k, v, sink.astype(F32), b=b, l_tot=l_tot, lc=lc).reshape(m, d)
    return _residual_matmul(o, w_out, b_out, x, cv, lv, l_tot=l_tot, lc=lc, r_gate=2,
                            tm=_row_tile(l_tot, 2176), tn=512, name="gqa_out_proj")


def _pack_halves(yb):
    w = yb.shape[1] // 2
    lo = lax.bitcast_convert_type(yb[:, :w].astype(F32), jnp.uint32)
    hi = lax.bitcast_convert_type(yb[:, w:].astype(F32), jnp.uint32)
    return lax.shift_right_logical(lo, jnp.uint32(16)) | hi


def _unpack_halves(words):
    lo = lax.bitcast_convert_type(lax.shift_left(words, jnp.uint32(16)), F32).astype(BF16)
    hi = lax.bitcast_convert_type(words & jnp.uint32(0xFFFF0000), F32).astype(BF16)
    return lo, hi


def _ffn_route_body(x_ref, g_ref, cv_ref, lv_ref, rw_ref, rb_ref, fpk_ref, ri_ref, rg_ref, cnt_ref,
                    carry_ref, *, tm, tpb, lc, route_ctx, n_e):
    i = pl.program_id(0)

    @pl.when(i == 0)
    def _():
        carry_ref[...] = jnp.zeros_like(carry_ref)

    x = x_ref[...]
    y = x * lax.rsqrt(jnp.mean(x * x, axis=-1, keepdims=True) + EPS) * g_ref[...]
    is_ctx = _is_ctx_rows(i, tm, tpb, lc)
    y = y * (1.0 + _pick_mod(is_ctx, cv_ref, lv_ref, 4)) + _pick_mod(is_ctx, cv_ref, lv_ref, 3)
    yb = y.astype(BF16)
    fpk_ref[...] = _pack_halves(yb)

    lane = lax.broadcasted_iota(jnp.int32, (tm, LANES), 1)
    logits = jnp.dot(yb, rw_ref[...].astype(BF16), preferred_element_type=F32) + rb_ref[...]
    logits = jnp.where(lane < n_e, logits, NEG_INF)
    vals, idxs, hots = [], [], []
    for _ in range(TOP_K):
        mk = jnp.max(logits, axis=-1, keepdims=True)
        ik = jnp.min(jnp.where(logits == mk, lane, LANES), axis=-1, keepdims=True)
        hot = lane == ik
        vals.append(mk)
        idxs.append(ik)
        hots.append(hot)
        logits = jnp.where(hot, NEG_INF, logits)
    exps = [jnp.exp(v - vals[0]) for v in vals]
    den = exps[0] + exps[1] + exps[2] + exps[3]

    routed = jnp.ones_like(is_ctx) if route_ctx else jnp.logical_not(is_ctx)
    chosen = jnp.zeros((tm, LANES), F32)
    for hot in hots:
        chosen = chosen + jnp.where(jnp.logical_and(hot, routed), 1.0, 0.0)
    before = (lax.broadcasted_iota(jnp.int32, (tm, tm), 0)
              > lax.broadcasted_iota(jnp.int32, (tm, tm), 1)).astype(BF16)
    arrived = jnp.dot(before, chosen.astype(BF16), preferred_element_type=F32) + carry_ref[...]
    carry_ref[...] = carry_ref[...] + jnp.sum(chosen, axis=0, keepdims=True)
    cnt_ref[...] = carry_ref[...]

    ri = jnp.zeros((tm, LANES), jnp.int32)
    rg = jnp.zeros((tm, LANES), F32)
    for k in range(TOP_K):
        rank = jnp.sum(jnp.where(hots[k], arrived, 0.0), axis=-1, keepdims=True)
        ri = jnp.where(lane == k, idxs[k], ri)
        ri = jnp.where(lane == TOP_K + k, rank.astype(jnp.int32), ri)
        rg = jnp.where(lane == k, exps[k] / den, rg)
    ri_ref[...] = ri
    rg_ref[...] = rg


def _ffn_route(x, gain, cv, lv, router_w_pad, router_b_pad, layer, *, l_tot, lc, route_ctx, n_e):
    m, d = x.shape
    tm = _row_tile(l_tot, 544)
    tpb = l_tot // tm
    body = functools.partial(_ffn_route_body, tm=tm, tpb=tpb, lc=lc, route_ctx=route_ctx, n_e=n_e)
    return pl.pallas_call(
        body,
        grid=(m // tm,),
        in_specs=[pl.BlockSpec((tm, d), lambda i: (i, 0)),
                  pl.BlockSpec((1, d), lambda i: (0, 0)),
                  pl.BlockSpec((MOD_ROWS, d), lambda i: (0, 0)),
                  pl.BlockSpec((1, MOD_ROWS, d), lambda i: (i // tpb, 0, 0)),
                  pl.BlockSpec((None, d, LANES), lambda i: (layer, 0, 0)),
                  pl.BlockSpec((None, 1, LANES), lambda i: (layer, 0, 0))],
        out_specs=[pl.BlockSpec((tm, d // 2), lambda i: (i, 0)),
                   pl.BlockSpec((tm, LANES), lambda i: (i, 0)),
                   pl.BlockSpec((tm, LANES), lambda i: (i, 0)),
                   pl.BlockSpec((1, LANES), lambda i: (0, 0))],
        out_shape=[jax.ShapeDtypeStruct((m, d // 2), jnp.uint32),
                   jax.ShapeDtypeStruct((m, LANES), jnp.int32),
                   jax.ShapeDtypeStruct((m, LANES), F32),
                   jax.ShapeDtypeStruct((1, LANES), F32)],
        scratch_shapes=[pltpu.VMEM((1, LANES), F32)],
        compiler_params=_params(1),
        name="ffn_route",
    )(x, gain.reshape(1, d), cv, lv, router_w_pad, router_b_pad)


def _dispatch_body(dest_ref, f_ref, init_hbm, xr_hbm, sem, *, tm, tpb, ctx_tiles, route_ctx):
    del init_hbm
    i = pl.program_id(0)

    def scatter_rows():
        def start(r, carry):
            a0 = (i * tm + r) * TOP_K
            for k in range(TOP_K):
                pltpu.make_async_copy(f_ref.at[pl.ds(r, 1)], xr_hbm.at[pl.ds(dest_ref[a0 + k], 1)], sem).start()
            return carry

        lax.fori_loop(0, tm, start, 0, unroll=ISSUE_UNROLL)

        def wait(r, carry):
            for k in range(TOP_K):
                pltpu.make_async_copy(f_ref.at[pl.ds(0, 1)], xr_hbm.at[pl.ds(0, 1)], sem).wait()
            return carry

        lax.fori_loop(0, tm, wait, 0, unroll=ISSUE_UNROLL)

    if route_ctx:
        scatter_rows()
    else:
        pl.when(lax.rem(i, tpb) >= ctx_tiles)(scatter_rows)


def _dispatch(fpk, dest, n_rows, *, l_tot, lc, route_ctx):
    m, w = fpk.shape
    tm = 256
    assert lc % tm == 0 and l_tot % tm == 0
    body = functools.partial(_dispatch_body, tm=tm, tpb=l_tot // tm, ctx_tiles=lc // tm, route_ctx=route_ctx)
    return pl.pallas_call(
        body,
        grid_spec=pltpu.PrefetchScalarGridSpec(
            num_scalar_prefetch=1,
            grid=(m // tm,),
            in_specs=[pl.BlockSpec((tm, w), lambda i, dst: (i, 0)),
                      pl.BlockSpec(memory_space=pl.ANY)],
            out_specs=pl.BlockSpec(memory_space=pl.ANY),
            scratch_shapes=[pltpu.SemaphoreType.DMA(())]),
        out_shape=jax.ShapeDtypeStruct((n_rows, w), jnp.uint32),
        input_output_aliases={2: 0},
        compiler_params=_row_copy_params(),
        name="moe_dispatch",
    )(dest, fpk, jnp.zeros((n_rows, w), jnp.uint32))


def _stream_expert_rows(row0, nb, src_hbm, dst_hbm, inbuf, outbuf, sin, sout, compute):
    def rows(j):
        return pl.ds(pl.multiple_of(row0 + j * MOE_TM, MOE_TM), MOE_TM)

    def in_copy(j, slot):
        return pltpu.make_async_copy(src_hbm.at[rows(j)], inbuf.at[slot], sin.at[slot])

    def out_copy(j, slot):
        return pltpu.make_async_copy(outbuf.at[slot], dst_hbm.at[rows(j)], sout.at[slot])

    in_copy(0, 0).start()

    def block(j, carry):
        slot = lax.rem(j, 2)

        @pl.when(j + 1 < nb)
        def _():
            in_copy(j + 1, 1 - slot).start()

        in_copy(j, slot).wait()

        @pl.when(j >= 2)
        def _():
            out_copy(j - 2, slot).wait()

        outbuf[slot] = compute(inbuf[slot])
        out_copy(j, slot).start()
        return carry

    lax.fori_loop(0, nb, block, 0)

    @pl.when(nb >= 2)
    def _():
        out_copy(nb - 2, lax.rem(nb, 2)).wait()

    out_copy(nb - 1, lax.rem(nb - 1, 2)).wait()


def _zero_unused_rows(ps_ref, nb_ref, dst_hbm, outbuf, sout):
    e = pl.program_id(0)
    n_total = dst_hbm.shape[0] // MOE_TM

    @pl.when(e == pl.num_programs(0) - 1)
    def _():
        first = ps_ref[e] // MOE_TM + nb_ref[e]
        outbuf[0] = jnp.zeros(outbuf.shape[1:], outbuf.dtype)

        def tail_copy(j):
            rows = pl.ds(pl.multiple_of(j * MOE_TM, MOE_TM), MOE_TM)
            return pltpu.make_async_copy(outbuf.at[0], dst_hbm.at[rows], sout.at[0])

        def start(j, carry):
            tail_copy(j).start()
            return carry

        def wait(j, carry):
            tail_copy(j).wait()
            return carry

        lax.fori_loop(first, n_total, start, 0)
        lax.fori_loop(first, n_total, wait, 0)


def _moe_gu_body(ps_ref, nb_ref, wg_ref, wl_ref, bg_ref, bl_ref, x_hbm, act_hbm,
                 wgb_ref, wlb_ref, xbuf, obuf, sin, sout):
    e = pl.program_id(0)

    def compute(words):
        x_lo, x_hi = _unpack_halves(words)
        w = x_lo.shape[1]
        h_glu = (jnp.dot(x_lo, wgb_ref[0:w, :], preferred_element_type=F32)
                 + jnp.dot(x_hi, wgb_ref[w:, :], preferred_element_type=F32) + bg_ref[...])
        h_lin = (jnp.dot(x_lo, wlb_ref[0:w, :], preferred_element_type=F32)
                 + jnp.dot(x_hi, wlb_ref[w:, :], preferred_element_type=F32) + bl_ref[...])
        h_glu = jnp.minimum(h_glu, SWIGLU_LIMIT)
        h_lin = jnp.clip(h_lin, -SWIGLU_LIMIT, SWIGLU_LIMIT)
        return (h_glu * _sigmoid(SWIGLU_ALPHA * h_glu) * (h_lin + 1.0)).astype(BF16)

    @pl.when(nb_ref[e] > 0)
    def _():
        wgb_ref[...] = wg_ref[...].astype(BF16)
        wlb_ref[...] = wl_ref[...].astype(BF16)
        _stream_expert_rows(ps_ref[e], nb_ref[e], x_hbm, act_hbm, xbuf, obuf, sin, sout, compute)

    _zero_unused_rows(ps_ref, nb_ref, act_hbm, obuf, sout)


def _moe_down_body(ps_ref, nb_ref, w_ref, b_ref, act_hbm, y_hbm, wb_ref, abuf, ybuf, sin, sout):
    e = pl.program_id(0)

    def compute(act):
        return jnp.dot(act, wb_ref[...], preferred_element_type=F32) + b_ref[...]

    @pl.when(nb_ref[e] > 0)
    def _():
        wb_ref[...] = w_ref[...].astype(BF16)
        _stream_expert_rows(ps_ref[e], nb_ref[e], act_hbm, y_hbm, abuf, ybuf, sin, sout, compute)

    _zero_unused_rows(ps_ref, nb_ref, y_hbm, ybuf, sout)


def _moe_experts(x_rows, pstart, n_blk, w_gu, b_gu, w_down, b_down, layer):
    n_rows, w = x_rows.shape
    depth, n_e, ff, d = w_down.shape
    b_gu4 = b_gu.reshape(depth, n_e, 1, 2 * ff)
    dma2 = pltpu.SemaphoreType.DMA((2,))
    act = pl.pallas_call(
        _moe_gu_body,
        grid_spec=pltpu.PrefetchScalarGridSpec(
            num_scalar_prefetch=2,
            grid=(n_e,),
            in_specs=[pl.BlockSpec((None, None, d, ff), lambda e, ps, nb: (layer, e, 0, 0)),
                      pl.BlockSpec((None, None, d, ff), lambda e, ps, nb: (layer, e, 0, 1)),
                      pl.BlockSpec((None, None, 1, ff), lambda e, ps, nb: (layer, e, 0, 0)),
                      pl.BlockSpec((None, None, 1, ff), lambda e, ps, nb: (layer, e, 0, 1)),
                      pl.BlockSpec(memory_space=pl.ANY)],
            out_specs=pl.BlockSpec(memory_space=pl.ANY),
            scratch_shapes=[pltpu.VMEM((d, ff), BF16), pltpu.VMEM((d, ff), BF16),
                            pltpu.VMEM((2, MOE_TM, w), jnp.uint32), pltpu.VMEM((2, MOE_TM, ff), BF16),
                            dma2, dma2]),
        out_shape=jax.ShapeDtypeStruct((n_rows, ff), BF16),
        compiler_params=_params(1),
        name="moe_gate_up",
    )(pstart, n_blk, w_gu, w_gu, b_gu4, b_gu4, x_rows)
    return pl.pallas_call(
        _moe_down_body,
        grid_spec=pltpu.PrefetchScalarGridSpec(
            num_scalar_prefetch=2,
            grid=(n_e,),
            in_specs=[pl.BlockSpec((None, None, ff, d), lambda e, ps, nb: (layer, e, 0, 0)),
                      pl.BlockSpec((None, None, 1, d), lambda e, ps, nb: (layer, e, 0, 0)),
                      pl.BlockSpec(memory_space=pl.ANY)],
            out_specs=pl.BlockSpec(memory_space=pl.ANY),
            scratch_shapes=[pltpu.VMEM((ff, d), BF16),
                            pltpu.VMEM((2, MOE_TM, ff), BF16), pltpu.VMEM((2, MOE_TM, d), F32),
                            dma2, dma2]),
        out_shape=jax.ShapeDtypeStruct((n_rows, d), F32),
        compiler_params=_params(1),
        name="moe_down",
    )(pstart, n_blk, w_down, b_down.reshape(depth, n_e, 1, d), act)


def _combine_body(dest_ref, x_ref, rg_ref, cv_ref, lv_ref, y_hbm, o_ref, buf, sem,
                  *, tt, n_tiles, tpb, ctx_tiles, lc, route_ctx):
    i = pl.program_id(0)

    def routed(tile):
        return (tile >= 0) if route_ctx else (lax.rem(tile, tpb) >= ctx_tiles)

    def row_copy(src, slot, k, r):
        return pltpu.make_async_copy(y_hbm.at[pl.ds(src, 1)], buf.at[slot, k, pl.ds(r, 1)], sem.at[slot])

    def issue(tile, slot):
        def start(r, carry):
            a0 = (tile * tt + r) * TOP_K
            for k in range(TOP_K):
                row_copy(dest_ref[a0 + k], slot, k, r).start()
            return carry

        lax.fori_loop(0, tt, start, 0, unroll=ISSUE_UNROLL)

    @pl.when(jnp.logical_and(i == 0, routed(i)))
    def _():
        issue(i, 0)

    nxt = jnp.minimum(i + 1, n_tiles - 1)

    @pl.when(jnp.logical_and(i + 1 < n_tiles, routed(nxt)))
    def _():
        issue(nxt, lax.rem(i + 1, 2))

    slot = lax.rem(i, 2)

    @pl.when(routed(i))
    def _():
        def wait(r, carry):
            for k in range(TOP_K):
                row_copy(0, slot, k, 0).wait()
            return carry

        lax.fori_loop(0, tt, wait, 0, unroll=ISSUE_UNROLL)
        rg = rg_ref[...]
        acc = rg[:, 0:1] * buf[slot, 0]
        for k in range(1, TOP_K):
            acc = acc + rg[:, k:k + 1] * buf[slot, k]
        gate = _pick_mod(_is_ctx_rows(i, tt, tpb, lc), cv_ref, lv_ref, 5)
        o_ref[...] = x_ref[...] + gate * acc

    @pl.when(jnp.logical_not(routed(i)))
    def _():
        o_ref[...] = x_ref[...]


def _combine(x, y_rows, rg, dest, cv, lv, *, l_tot, lc, route_ctx):
    m, d = x.shape
    tt = 128
    assert lc % tt == 0 and l_tot % tt == 0
    tpb = l_tot // tt
    n_tiles = m // tt
    body = functools.partial(_combine_body, tt=tt, n_tiles=n_tiles, tpb=tpb, ctx_tiles=lc // tt, lc=lc,
                             route_ctx=route_ctx)
    return pl.pallas_call(
        body,
        grid_spec=pltpu.PrefetchScalarGridSpec(
            num_scalar_prefetch=1,
            grid=(n_tiles,),
            in_specs=[pl.BlockSpec((tt, d), lambda i, dst: (i, 0)),
                      pl.BlockSpec((tt, LANES), lambda i, dst: (i, 0)),
                      pl.BlockSpec((MOD_ROWS, d), lambda i, dst: (0, 0)),
                      pl.BlockSpec((1, MOD_ROWS, d), lambda i, dst: (i // tpb, 0, 0)),
                      pl.BlockSpec(memory_space=pl.ANY)],
            out_specs=pl.BlockSpec((tt, d), lambda i, dst: (i, 0)),
            scratch_shapes=[pltpu.VMEM((2, TOP_K, tt, d), F32), pltpu.SemaphoreType.DMA((2,))]),
        out_shape=jax.ShapeDtypeStruct((m, d), F32),
        compiler_params=_row_copy_params(),
        name="moe_combine",
    )(dest, x, rg, cv, lv, y_rows)


def _moe_layer(x, gain, cv, lv, router_w_pad, router_b_pad, w_gu, b_gu, w_down, b_down, layer,
               *, l_tot, lc, route_ctx):
    m, d = x.shape
    n_e = w_gu.shape[1]
    fpk, ri, rg, cnt = _ffn_route(x, gain, cv, lv, router_w_pad, router_b_pad, layer,
                                  l_tot=l_tot, lc=lc, route_ctx=route_ctx, n_e=n_e)
    counts = cnt[0, :n_e].astype(jnp.int32)
    n_blk = (counts + MOE_TM - 1) // MOE_TM
    pstart = (jnp.cumsum(n_blk) - n_blk) * MOE_TM
    n_rows = (-(-m * TOP_K // MOE_TM) + n_e) * MOE_TM
    dest = (pstart[ri[:, :TOP_K]] + ri[:, TOP_K:2 * TOP_K]).reshape(m * TOP_K)
    x_rows = _dispatch(fpk, dest, n_rows, l_tot=l_tot, lc=lc, route_ctx=route_ctx)
    y_rows = _moe_experts(x_rows, pstart, n_blk, w_gu, b_gu, w_down, b_down, layer)
    return _combine(x, y_rows, rg, dest, cv, lv, l_tot=l_tot, lc=lc, route_ctx=route_ctx)


def _rope_1d_tables(l_tot, half):
    inv = ROPE_BASE ** (-jnp.arange(half, dtype=F32) / half)
    ang = jnp.arange(l_tot, dtype=F32)[:, None] * inv
    return jnp.cos(ang), jnp.sin(ang)


def _axial_tables(l_tot, lc):
    n_freq = GQA_HEAD_DIM // 4
    t = jnp.maximum(jnp.arange(l_tot) - lc, 0)
    pos = jnp.stack([(t // GRID_W).astype(F32), (t % GRID_W).astype(F32)], axis=-1)
    inv_freq = ROPE_BASE ** (-jnp.arange(n_freq, dtype=F32) / n_freq)
    ang = pos[:, :, None] * inv_freq
    lane = jnp.arange(LANES)
    axis = (lane % GQA_HEAD_DIM) // (2 * n_freq)
    freq = lane % n_freq
    first_half = (lane % (2 * n_freq)) < n_freq
    is_lat = (jnp.arange(l_tot) >= lc)[:, None]
    cos = jnp.where(is_lat, jnp.cos(ang)[:, axis, freq], 1.0)
    sin = jnp.where(is_lat, jnp.sin(ang)[:, axis, freq], 0.0)
    return cos, jnp.where(first_half, -sin, 0.0), jnp.where(first_half, 0.0, sin)


def kernel(x, c, ctx, c_ctx, norm_mix, norm_ffn, mod_w, mod_b, ret_w_in, ret_decay, ret_gn, ret_w_out, mla_w_in, mla_q_a_norm, mla_kv_a_norm, mla_w_q_b, mla_w_kv_b, mla_q_norm, mla_k_norm, mla_w_out, sc_w_in, sc_conv_w, sc_conv_b, sc_w_out, gqa_w_in, gqa_b_in, gqa_q_norm, gqa_k_norm, gqa_sink, gqa_w_out, gqa_b_out, moe_router_w, moe_router_b, moe_w_gu, moe_b_gu, moe_w_down, moe_b_down):
    b, n_lat, d = x.shape
    lc = ctx.shape[1]
    l_tot = lc + n_lat
    m = b * l_tot
    depth = mod_w.shape[0]
    assert b + 1 <= MOD_ROWS and lc % 256 == 0 and n_lat % 256 == 0

    xs = jnp.concatenate([ctx, x], axis=1).reshape(m, d)
    cvec = jnp.zeros((MOD_ROWS, d), F32).at[:b].set(c).at[b].set(c_ctx)
    mod = _modulation(cvec, mod_w, mod_b).reshape(depth, MOD_ROWS, 6, d)
    pad_rows = ((0, 0), (0, MOD_ROWS - 6), (0, 0))
    ret_cos, ret_sin = _rope_1d_tables(l_tot, d // RET_HEADS // 2)
    tabs = _axial_tables(l_tot, lc)
    n_e = moe_router_w.shape[-1]
    router_w_pad = jnp.pad(moe_router_w, ((0, 0), (0, 0), (0, LANES - n_e)))
    router_b_pad = jnp.pad(moe_router_b, ((0, 0), (0, LANES - n_e))).reshape(depth, 1, LANES)

    for i in range(depth):
        kind, j = i % N_MIXERS, i // N_MIXERS
        cv = jnp.pad(mod[i, b], pad_rows[1:])
        lv = jnp.pad(mod[i, :b], pad_rows)
        lay = dict(b=b, l_tot=l_tot, lc=lc)
        h = _norm_mod(xs, norm_mix[i], cv, lv, l_tot=l_tot, lc=lc, r_shift=0, r_scale=1)
        if kind == 0:
            xs = _retention_layer(xs, h, cv, lv, ret_w_in[j], ret_decay[j], ret_gn[j], ret_w_out[j],
                                  ret_cos, ret_sin, **lay)
        elif kind == 1:
            xs = _mla_layer(xs, h, cv, lv, mla_w_in[j], mla_q_a_norm[j], mla_kv_a_norm[j], mla_w_q_b[j],
                            mla_w_kv_b[j], mla_q_norm[j], mla_k_norm[j], mla_w_out[j], tabs, **lay)
        elif kind == 2:
            xs = _shortconv_layer(xs, h, cv, lv, sc_w_in[j], sc_conv_w[j], sc_conv_b[j], sc_w_out[j], **lay)
        else:
            xs = _gqa_layer(xs, h, cv, lv, gqa_w_in[j], gqa_b_in[j], gqa_q_norm[j], gqa_k_norm[j],
                            gqa_sink[j], gqa_w_out[j], gqa_b_out[j], tabs, **lay)
        xs = _moe_layer(xs, norm_ffn[i], cv, lv, router_w_pad, router_b_pad, moe_w_gu, moe_b_gu,
                        moe_w_down, moe_b_down, i, l_tot=l_tot, lc=lc, route_ctx=i < depth - 1)
    return xs.reshape(b, l_tot, d)[:, lc:, :]
```

```python
import functools

import jax
import jax.numpy as jnp
from jax import lax
from jax.experimental import pallas as pl
from jax.experimental.pallas import tpu as pltpu

F32 = jnp.float32
BF16 = jnp.bfloat16

EPS = 1e-6
ROPE_BASE = 10000.0
NEG_INF = -1e30
LOG2_E = 1.4426950408889634
GRID_W = 64

RET_HEADS = 8
RET_CHUNK = 256
MLA_HEADS = 16
MLA_NOPE = 128
MLA_ROPE = 64
MLA_V = 128
MLA_RANK = 512
GQA_HEADS = 32
GQA_KV_HEADS = 4
GQA_GROUP = GQA_HEADS // GQA_KV_HEADS
GQA_HEAD_DIM = 64
WINDOW = 128
N_EXPERTS = 32
TOP_K = 4
SWIGLU_LIMIT = 7.0
SWIGLU_ALPHA = 1.702
N_MIXERS = 4

LANES = 128
VMEM_LIMIT = 56 * 1024 * 1024
MOE_TM = 512
MOD_ROWS = 8
ISSUE_UNROLL = 4


def _params(n_axes, vmem=VMEM_LIMIT):
    return pltpu.CompilerParams(dimension_semantics=("arbitrary",) * n_axes,
                                vmem_limit_bytes=vmem)


def _row_copy_params():
    return pltpu.CompilerParams(dimension_semantics=("arbitrary",), vmem_limit_bytes=VMEM_LIMIT,
                                disable_bounds_checks=True)


def _row_tile(l_tot, cap):
    best = 16
    for t in range(16, cap + 1, 16):
        if l_tot % t == 0:
            best = t
    return best


def _is_ctx_rows(i, tm, tpb, lc):
    rows = lax.rem(i, tpb) * tm + lax.broadcasted_iota(jnp.int32, (tm, 1), 0)
    return rows < lc


def _pick_mod(is_ctx, cv_ref, lv_ref, r):
    return jnp.where(is_ctx, cv_ref[r:r + 1, :], lv_ref[0, r:r + 1, :])


def _sigmoid(x):
    return 1.0 / (1.0 + jnp.exp(-x))


def _rope3(x, c, s1, s2):
    return x * c + pltpu.roll(x, LANES - 16, axis=1) * s1 + pltpu.roll(x, 16, axis=1) * s2


def _mod_body(c_ref, w_ref, b_ref, o_ref):
    c = c_ref[...]
    s = (c * _sigmoid(c)).astype(BF16)
    o_ref[0] = jnp.dot(s, w_ref[0].astype(BF16), preferred_element_type=F32) + b_ref[0]


def _modulation(cvec, mod_w, mod_b):
    depth, d, n = mod_w.shape
    tn = 2048
    return pl.pallas_call(
        _mod_body,
        grid=(depth, n // tn),
        in_specs=[pl.BlockSpec((MOD_ROWS, d), lambda l, j: (0, 0)),
                  pl.BlockSpec((1, d, tn), lambda l, j: (l, 0, j)),
                  pl.BlockSpec((1, 1, tn), lambda l, j: (l, 0, j))],
        out_specs=pl.BlockSpec((1, MOD_ROWS, tn), lambda l, j: (l, 0, j)),
        out_shape=jax.ShapeDtypeStruct((depth, MOD_ROWS, n), F32),
        compiler_params=_params(2),
        name="modulation",
    )(cvec, mod_w, mod_b.reshape(depth, 1, n))


def _norm_mod_body(x_ref, g_ref, cv_ref, lv_ref, o_ref, *, tm, tpb, lc, r_shift, r_scale):
    x = x_ref[...]
    y = x * lax.rsqrt(jnp.mean(x * x, axis=-1, keepdims=True) + EPS) * g_ref[...]
    is_ctx = _is_ctx_rows(pl.program_id(0), tm, tpb, lc)
    shift = _pick_mod(is_ctx, cv_ref, lv_ref, r_shift)
    scale = _pick_mod(is_ctx, cv_ref, lv_ref, r_scale)
    o_ref[...] = (y * (1.0 + scale) + shift).astype(BF16)


def _norm_mod(x, gain, cv, lv, *, l_tot, lc, r_shift, r_scale):
    m, d = x.shape
    tm = _row_tile(l_tot, 544)
    tpb = l_tot // tm
    body = functools.partial(_norm_mod_body, tm=tm, tpb=tpb, lc=lc, r_shift=r_shift, r_scale=r_scale)
    return pl.pallas_call(
        body,
        grid=(m // tm,),
        in_specs=[pl.BlockSpec((tm, d), lambda i: (i, 0)),
                  pl.BlockSpec((1, d), lambda i: (0, 0)),
                  pl.BlockSpec((MOD_ROWS, d), lambda i: (0, 0)),
                  pl.BlockSpec((1, MOD_ROWS, d), lambda i: (i // tpb, 0, 0))],
        out_specs=pl.BlockSpec((tm, d), lambda i: (i, 0)),
        out_shape=jax.ShapeDtypeStruct((m, d), BF16),
        compiler_params=_params(1),
        name="norm_mod",
    )(x, gain.reshape(1, d), cv, lv)


def _mm_body(*refs, n_extra, n_out, epilogue, prologue):
    a_ref, w_ref = refs[0], refs[1]
    extra = refs[2:2 + n_extra]
    outs = refs[2 + n_extra:2 + n_extra + n_out]
    scratch = refs[2 + n_extra + n_out:]
    a = a_ref[...] if prologue is None else prologue(a_ref, extra, scratch)
    acc = jnp.dot(a, w_ref[...].astype(BF16), preferred_element_type=F32)
    epilogue(acc, extra, outs)


def _matmul(a, w, extras, extra_specs, out_shapes, out_specs, epilogue, *, tm, tn, name,
            prologue=None, scratch_shapes=()):
    m, k = a.shape
    n = w.shape[1]
    body = functools.partial(_mm_body, n_extra=len(extras), n_out=len(out_shapes),
                             epilogue=epilogue, prologue=prologue)
    return pl.pallas_call(
        body,
        grid=(m // tm, n // tn),
        in_specs=[pl.BlockSpec((tm, k), lambda i, j: (i, 0)),
                  pl.BlockSpec((k, tn), lambda i, j: (0, j))] + list(extra_specs),
        out_specs=list(out_specs),
        out_shape=list(out_shapes),
        scratch_shapes=list(scratch_shapes),
        compiler_params=_params(2),
        name=name,
    )(a, w, *extras)


def _plain_epilogue(out_dtype):
    def epilogue(acc, extra, outs):
        outs[0][...] = acc.astype(out_dtype)
    return epilogue


def _bias_epilogue(out_dtype):
    def epilogue(acc, extra, outs):
        outs[0][...] = (acc + extra[0][...]).astype(out_dtype)
    return epilogue


def _residual_matmul(a, w, bias, x_res, cv, lv, *, l_tot, lc, r_gate, tm, tn, name,
                     prologue=None, pro_extras=(), pro_specs=(), scratch_shapes=()):
    m = a.shape[0]
    n = w.shape[1]
    tpb = l_tot // tm
    n_pro = len(pro_extras)

    def epilogue(acc, extra, outs):
        b_ref, x_ref, cv_ref, lv_ref = extra[n_pro:n_pro + 4]
        is_ctx = _is_ctx_rows(pl.program_id(0), tm, tpb, lc)
        gate = _pick_mod(is_ctx, cv_ref, lv_ref, r_gate)
        outs[0][...] = x_ref[...] + gate * (acc + b_ref[...])

    extras = list(pro_extras) + [bias.reshape(1, n), x_res, cv, lv]
    specs = list(pro_specs) + [
        pl.BlockSpec((1, tn), lambda i, j: (0, j)),
        pl.BlockSpec((tm, tn), lambda i, j: (i, j)),
        pl.BlockSpec((MOD_ROWS, tn), lambda i, j: (0, j)),
        pl.BlockSpec((1, MOD_ROWS, tn), lambda i, j: (i // tpb, 0, j))]
    return _matmul(a, w, extras, specs,
                   [jax.ShapeDtypeStruct((m, n), F32)],
                   [pl.BlockSpec((tm, tn), lambda i, j: (i, j))],
                   epilogue, tm=tm, tn=tn, name=name, prologue=prologue,
                   scratch_shapes=scratch_shapes)[0]


def _ret_in_proj(h, w_in, cos, sin, *, l_tot, d_model):
    m = h.shape[0]
    n = w_in.shape[1]
    tm = _row_tile(l_tot, 2176)
    tn = 512
    tpb = l_tot // tm
    dk = d_model // RET_HEADS
    half = dk // 2
    n_q_tiles = d_model // tn
    k_scale = dk ** -0.5

    def epilogue(acc, extra, outs):
        cos_ref, sin_ref = extra
        j = pl.program_id(1)

        @pl.when(j < 2 * n_q_tiles)
        def _():
            c = cos_ref[...]
            s = sin_ref[...]
            mult = jnp.where(j >= n_q_tiles, k_scale, 1.0)
            for hh in range(tn // dk):
                x1 = acc[:, hh * dk:hh * dk + half]
                x2 = acc[:, hh * dk + half:(hh + 1) * dk]
                outs[0][:, hh * dk:hh * dk + half] = ((x1 * c - x2 * s) * mult).astype(BF16)
                outs[0][:, hh * dk + half:(hh + 1) * dk] = ((x2 * c + x1 * s) * mult).astype(BF16)

        @pl.when(j >= 2 * n_q_tiles)
        def _():
            outs[0][...] = acc.astype(BF16)

    tab_spec = pl.BlockSpec((tm, half), lambda i, j: (lax.rem(i, tpb), 0))
    return _matmul(h, w_in, [cos, sin], [tab_spec, tab_spec],
                   [jax.ShapeDtypeStruct((m, n), BF16)],
                   [pl.BlockSpec((tm, tn), lambda i, j: (i, j))],
                   epilogue, tm=tm, tn=tn, name="ret_in_proj")[0]


def _ret_body(lg_ref, q_ref, k_ref, v_ref, g_ref, gn_ref, o_ref, oacc_ref, sf_ref, sb_ref, *, nc_ctx, nc_tot):
    h = pl.program_id(1)
    ch = RET_CHUNK
    dv = v_ref.shape[-1]
    ii = lax.broadcasted_iota(jnp.int32, (ch, ch), 0).astype(F32)
    jj = lax.broadcasted_iota(jnp.int32, (ch, ch), 1).astype(F32)
    ri = lax.broadcasted_iota(jnp.int32, (ch, 1), 0).astype(F32)
    consts = []
    for direction in (0, 1):
        lg = lg_ref[direction, h]
        if direction == 0:
            diff = ii - jj
            q_decay = jnp.exp((ri + 1.0) * lg)
            k_decay = jnp.exp((ch - 1.0 - ri) * lg)
        else:
            diff = jj - ii
            q_decay = jnp.exp((ch - ri) * lg)
            k_decay = jnp.exp(ri * lg)
        intra = jnp.where(diff >= 0, jnp.exp(jnp.maximum(diff, 0.0) * lg), 0.0)
        chunk_decay = jnp.exp(jnp.full((1, dv), float(ch), F32) * lg)
        consts.append((intra, q_decay, k_decay, chunk_decay))
    sf_ref[...] = jnp.zeros_like(sf_ref)
    sb_ref[...] = jnp.zeros_like(sb_ref)
    oacc_ref[...] = jnp.zeros_like(oacc_ref)

    def chunk_rows(c):
        return pl.ds(pl.multiple_of(c * ch, ch), ch)

    def advance(c, s_ref, intra, q_decay, k_decay, chunk_decay):
        rows = chunk_rows(c)
        qc = q_ref[0, rows, :]
        kc = k_ref[0, rows, :]
        vc = v_ref[0, rows, :]
        scores = lax.dot_general(qc, kc, (((1,), (1,)), ((), ())), preferred_element_type=F32) * intra
        state = s_ref[...]
        out = (jnp.dot(scores.astype(BF16), vc, preferred_element_type=F32)
               + jnp.dot((qc.astype(F32) * q_decay).astype(BF16), state.astype(BF16),
                         preferred_element_type=F32))
        kt = (kc.astype(F32) * k_decay).T.astype(BF16)
        s_ref[...] = state * chunk_decay + jnp.dot(kt, vc, preferred_element_type=F32)
        return out

    def step(t, carry):
        c_fwd = t
        c_bwd = jnp.where(t < nc_ctx, nc_ctx - 1 - t, nc_tot - 1 - (t - nc_ctx))
        out_f = advance(c_fwd, sf_ref, *consts[0])
        out_b = advance(c_bwd, sb_ref, *consts[1])
        oacc_ref[chunk_rows(c_fwd), :] += out_f
        oacc_ref[chunk_rows(c_bwd), :] += out_b
        return carry

    lax.fori_loop(0, nc_tot, step, 0)

    def finish(c, carry):
        rows = chunk_rows(c)
        o = oacc_ref[rows, :]
        o = o * lax.rsqrt(jnp.mean(o * o, axis=-1, keepdims=True) + EPS) * gn_ref[...]
        g = g_ref[0, rows, :].astype(F32)
        o_ref[0, rows, :] = ((g * _sigmoid(g)) * o).astype(BF16)
        return carry

    lax.fori_loop(0, nc_tot, finish, 0)


def _retention(proj, log_gamma, gn, *, b, l_tot, lc, d_model):
    dk = d_model // RET_HEADS
    dv = 2 * d_model // RET_HEADS
    proj3 = proj.reshape(b, l_tot, proj.shape[-1])
    k_blk0 = d_model // dk
    v_blk0 = 2 * d_model // dv
    g_blk0 = 4 * d_model // dv
    body = functools.partial(_ret_body, nc_ctx=lc // RET_CHUNK, nc_tot=l_tot // RET_CHUNK)
    return pl.pallas_call(
        body,
        grid=(b, RET_HEADS),
        in_specs=[pl.BlockSpec(memory_space=pltpu.SMEM),
                  pl.BlockSpec((1, l_tot, dk), lambda bi, hi: (bi, 0, hi)),
                  pl.BlockSpec((1, l_tot, dk), lambda bi, hi: (bi, 0, k_blk0 + hi)),
                  pl.BlockSpec((1, l_tot, dv), lambda bi, hi: (bi, 0, v_blk0 + hi)),
                  pl.BlockSpec((1, l_tot, dv), lambda bi, hi: (bi, 0, g_blk0 + hi)),
                  pl.BlockSpec((1, dv), lambda bi, hi: (0, hi))],
        out_specs=pl.BlockSpec((1, l_tot, dv), lambda bi, hi: (bi, 0, hi)),
        out_shape=jax.ShapeDtypeStruct((b, l_tot, RET_HEADS * dv), BF16),
        scratch_shapes=[pltpu.VMEM((l_tot, dv), F32), pltpu.VMEM((dk, dv), F32), pltpu.VMEM((dk, dv), F32)],
        compiler_params=_params(2),
        name="retention",
    )(log_gamma, proj3, proj3, proj3, proj3, gn.reshape(1, RET_HEADS * dv))


def _retention_layer(x, h, cv, lv, w_in, decay, gn, w_out, cos, sin, *, b, l_tot, lc):
    m, d = x.shape
    proj = _ret_in_proj(h, w_in, cos, sin, l_tot=l_tot, d_model=d)
    log_gamma = jax.nn.log_sigmoid(decay.astype(F32))
    o = _retention(proj, log_gamma, gn, b=b, l_tot=l_tot, lc=lc, d_model=d).reshape(m, 2 * d)
    return _residual_matmul(o, w_out, jnp.zeros((d,), F32), x, cv, lv, l_tot=l_tot, lc=lc, r_gate=2,
                            tm=_row_tile(l_tot, 1088), tn=512, name="ret_out_proj")


def _mla_in_proj(h, w_in_pad, qa_gain, kva_gain, *, l_tot):
    m = h.shape[0]
    n = w_in_pad.shape[1]
    tm = _row_tile(l_tot, 1088)
    r = MLA_RANK

    def epilogue(acc, extra, outs):
        qg_ref, kg_ref = extra
        ql = acc[:, :r]
        kl = acc[:, r:2 * r]
        outs[0][...] = (ql * lax.rsqrt(jnp.mean(ql * ql, axis=-1, keepdims=True) + EPS)
                        * qg_ref[...]).astype(BF16)
        outs[1][...] = (kl * lax.rsqrt(jnp.mean(kl * kl, axis=-1, keepdims=True) + EPS)
                        * kg_ref[...]).astype(BF16)
        outs[2][...] = acc[:, 2 * r:]

    gspec = pl.BlockSpec((1, r), lambda i, j: (0, 0))
    return _matmul(h, w_in_pad, [qa_gain.reshape(1, r), kva_gain.reshape(1, r)], [gspec, gspec],
                   [jax.ShapeDtypeStruct((m, r), BF16), jax.ShapeDtypeStruct((m, r), BF16),
                    jax.ShapeDtypeStruct((m, LANES), F32)],
                   [pl.BlockSpec((tm, r), lambda i, j: (i, 0)),
                    pl.BlockSpec((tm, r), lambda i, j: (i, 0)),
                    pl.BlockSpec((tm, LANES), lambda i, j: (i, 0))],
                   epilogue, tm=tm, tn=n, name="mla_in_proj")


def _mla_q_proj(qn, wq_pad, q_gain_pad, tabs, *, l_tot):
    m = qn.shape[0]
    n = wq_pad.shape[1]
    tm = _row_tile(l_tot, 1088)
    tn = 1024
    tpb = l_tot // tm
    hw = 2 * LANES
    width = MLA_NOPE + MLA_ROPE
    scale = width ** -0.5 * LOG2_E

    def epilogue(acc, extra, outs):
        g_ref, c_ref, s1_ref, s2_ref = extra
        c, s1, s2 = c_ref[...], s1_ref[...], s2_ref[...]
        for hh in range(tn // hw):
            xs = acc[:, hh * hw:(hh + 1) * hw]
            inv = lax.rsqrt(jnp.sum(xs * xs, axis=-1, keepdims=True) * (1.0 / width) + EPS)
            y = xs * inv * g_ref[...] * scale
            outs[0][:, hh * hw:hh * hw + LANES] = y[:, :LANES].astype(BF16)
            outs[0][:, hh * hw + LANES:(hh + 1) * hw] = _rope3(y[:, LANES:], c, s1, s2).astype(BF16)

    tspec = pl.BlockSpec((tm, LANES), lambda i, j: (lax.rem(i, tpb), 0))
    return _matmul(qn, wq_pad, [q_gain_pad] + list(tabs),
                   [pl.BlockSpec((1, hw), lambda i, j: (0, 0)), tspec, tspec, tspec],
                   [jax.ShapeDtypeStruct((m, n), BF16)],
                   [pl.BlockSpec((tm, tn), lambda i, j: (i, j))],
                   epilogue, tm=tm, tn=tn, name="mla_q_proj")[0]


def _mla_kv_proj(kvn, w_kv, kr, k_gain_pad, tabs, *, l_tot):
    m = kvn.shape[0]
    n = w_kv.shape[1]
    tm = _row_tile(l_tot, 1088)
    tn = 1024
    tpb = l_tot // tm
    hw = MLA_NOPE + MLA_V
    width = MLA_NOPE + MLA_ROPE

    def epilogue(acc, extra, outs):
        kr_ref, g_ref, c_ref, s1_ref, s2_ref = extra
        kr_raw = kr_ref[...]
        ss_r = jnp.sum(kr_raw * kr_raw, axis=-1, keepdims=True)
        kr_rot = _rope3(kr_raw * g_ref[:, LANES:], c_ref[...], s1_ref[...], s2_ref[...])
        g_nope = g_ref[:, :LANES]
        for hh in range(tn // hw):
            kn = acc[:, hh * hw:hh * hw + MLA_NOPE]
            inv = lax.rsqrt((jnp.sum(kn * kn, axis=-1, keepdims=True) + ss_r) * (1.0 / width) + EPS)
            outs[0][:, hh * hw:hh * hw + LANES] = (kn * inv * g_nope).astype(BF16)
            outs[0][:, hh * hw + LANES:(hh + 1) * hw] = (kr_rot * inv).astype(BF16)
            outs[1][:, hh * MLA_V:(hh + 1) * MLA_V] = acc[:, hh * hw + MLA_NOPE:(hh + 1) * hw].astype(BF16)

    tspec = pl.BlockSpec((tm, LANES), lambda i, j: (lax.rem(i, tpb), 0))
    return _matmul(kvn, w_kv, [kr, k_gain_pad] + list(tabs),
                   [pl.BlockSpec((tm, LANES), lambda i, j: (i, 0)),
                    pl.BlockSpec((1, 2 * LANES), lambda i, j: (0, 0)), tspec, tspec, tspec],
                   [jax.ShapeDtypeStruct((m, n), BF16), jax.ShapeDtypeStruct((m, n // 2), BF16)],
                   [pl.BlockSpec((tm, tn), lambda i, j: (i, j)),
                    pl.BlockSpec((tm, tn // 2), lambda i, j: (i, j))],
                   epilogue, tm=tm, tn=tn, name="mla_kv_proj")


def _softmax_pv(q, k, v):
    s = lax.dot_general(q, k, (((1,), (1,)), ((), ())), preferred_element_type=F32)
    p = jnp.exp2(s - jnp.max(s, axis=-1, keepdims=True))
    den = jnp.sum(p, axis=-1, keepdims=True)
    return jnp.dot(p.astype(BF16), v, preferred_element_type=F32) / den


def _mla_attn_body(q_ref, k_ref, v_ref, o_ref, *, lc, tq, n_q):
    o_ref[0, 0:lc, :] = _softmax_pv(q_ref[0, 0:lc, :], k_ref[0, 0:lc, :], v_ref[0, 0:lc, :]).astype(BF16)

    def step(t, carry):
        rows = pl.ds(pl.multiple_of(lc + t * tq, LANES), tq)
        o_ref[0, rows, :] = _softmax_pv(q_ref[0, rows, :], k_ref[0], v_ref[0]).astype(BF16)
        return carry

    lax.fori_loop(0, n_q, step, 0, unroll=4)


def _mla_attention(q, k, v, *, b, l_tot, lc):
    hw = 2 * LANES
    tq = 256
    assert lc % LANES == 0 and (l_tot - lc) % tq == 0
    q3 = q.reshape(b, l_tot, MLA_HEADS * hw)
    k3 = k.reshape(b, l_tot, MLA_HEADS * hw)
    v3 = v.reshape(b, l_tot, MLA_HEADS * MLA_V)
    body = functools.partial(_mla_attn_body, lc=lc, tq=tq, n_q=(l_tot - lc) // tq)
    return pl.pallas_call(
        body,
        grid=(b, MLA_HEADS),
        in_specs=[pl.BlockSpec((1, l_tot, hw), lambda bi, hi: (bi, 0, hi)),
                  pl.BlockSpec((1, l_tot, hw), lambda bi, hi: (bi, 0, hi)),
                  pl.BlockSpec((1, l_tot, MLA_V), lambda bi, hi: (bi, 0, hi))],
        out_specs=pl.BlockSpec((1, l_tot, MLA_V), lambda bi, hi: (bi, 0, hi)),
        out_shape=jax.ShapeDtypeStruct((b, l_tot, MLA_HEADS * MLA_V), BF16),
        compiler_params=_params(2),
        name="mla_attention",
    )(q3, k3, v3)


def _mla_layer(x, h, cv, lv, w_in, qa_gain, kva_gain, w_q_b, w_kv_b, q_gain, k_gain, w_out, tabs,
               *, b, l_tot, lc):
    m, d = x.shape
    width = MLA_NOPE + MLA_ROPE
    hw = 2 * LANES
    w_in_pad = jnp.pad(w_in, ((0, 0), (0, 2 * MLA_RANK + LANES - w_in.shape[1])))
    wq_pad = jnp.pad(w_q_b.reshape(MLA_RANK, MLA_HEADS, width),
                     ((0, 0), (0, 0), (0, hw - width))).reshape(MLA_RANK, MLA_HEADS * hw)
    q_gain_pad = jnp.pad(q_gain, (0, hw - width)).reshape(1, hw)
    k_gain_pad = jnp.pad(k_gain, (0, hw - width)).reshape(1, hw)
    qn, kvn, kr = _mla_in_proj(h, w_in_pad, qa_gain, kva_gain, l_tot=l_tot)
    q = _mla_q_proj(qn, wq_pad, q_gain_pad, tabs, l_tot=l_tot)
    k, v = _mla_kv_proj(kvn, w_kv_b, kr, k_gain_pad, tabs, l_tot=l_tot)
    o = _mla_attention(q, k, v, b=b, l_tot=l_tot, lc=lc).reshape(m, MLA_HEADS * MLA_V)
    return _residual_matmul(o, w_out, jnp.zeros((d,), F32), x, cv, lv, l_tot=l_tot, lc=lc,
                            r_gate=2, tm=_row_tile(l_tot, 2176), tn=512, name="mla_out_proj")


def _conv_gate_body(bg_ref, cg_ref, u_ref, w_ref, cb_ref, o_ref, z_ref, *, l_tot, lc):
    pad = 8
    tn = o_ref.shape[-1]
    z_ref[0:pad, :] = jnp.zeros((pad, tn), F32)
    z_ref[pad + l_tot:, :] = jnp.zeros((pad, tn), F32)
    z_ref[pad:pad + l_tot, :] = cg_ref[0].astype(F32) * u_ref[0].astype(F32)
    rows = lax.broadcasted_iota(jnp.int32, (l_tot, 1), 0)
    z_prev = jnp.where(rows == lc, 0.0, z_ref[pad - 1:pad - 1 + l_tot, :])
    z_next = jnp.where(rows == lc - 1, 0.0, z_ref[pad + 1:pad + 1 + l_tot, :])
    z_mid = z_ref[pad:pad + l_tot, :]
    conv = z_prev * w_ref[0:1, :] + z_mid * w_ref[1:2, :] + z_next * w_ref[2:3, :] + cb_ref[...]
    o_ref[0] = (bg_ref[0].astype(F32) * conv).astype(BF16)


def _conv_gate(proj, conv_w, conv_b, *, b, l_tot, lc, d):
    tn = 256
    nb = d // tn
    proj3 = proj.reshape(b, l_tot, 3 * d)
    body = functools.partial(_conv_gate_body, l_tot=l_tot, lc=lc)
    return pl.pallas_call(
        body,
        grid=(b, nb),
        in_specs=[pl.BlockSpec((1, l_tot, tn), lambda bi, j: (bi, 0, j)),
                  pl.BlockSpec((1, l_tot, tn), lambda bi, j: (bi, 0, nb + j)),
                  pl.BlockSpec((1, l_tot, tn), lambda bi, j: (bi, 0, 2 * nb + j)),
                  pl.BlockSpec((3, tn), lambda bi, j: (0, j)),
                  pl.BlockSpec((1, tn), lambda bi, j: (0, j))],
        out_specs=pl.BlockSpec((1, l_tot, tn), lambda bi, j: (bi, 0, j)),
        out_shape=jax.ShapeDtypeStruct((b, l_tot, d), BF16),
        scratch_shapes=[pltpu.VMEM((l_tot + 16, tn), F32)],
        compiler_params=_params(2),
        name="conv_gate",
    )(proj3, proj3, proj3, conv_w, conv_b.reshape(1, d))


def _shortconv_layer(x, h, cv, lv, w_in, conv_w, conv_b, w_out, *, b, l_tot, lc):
    m, d = x.shape
    tm = _row_tile(l_tot, 2176)
    proj = _matmul(h, w_in, [], [], [jax.ShapeDtypeStruct((m, 3 * d), BF16)],
                   [pl.BlockSpec((tm, 512), lambda i, j: (i, j))],
                   _plain_epilogue(BF16), tm=tm, tn=512, name="sc_in_proj")[0]
    a = _conv_gate(proj, conv_w, conv_b, b=b, l_tot=l_tot, lc=lc, d=d).reshape(m, d)
    return _residual_matmul(a, w_out, jnp.zeros((d,), F32), x, cv, lv, l_tot=l_tot, lc=lc,
                            r_gate=2, tm=tm, tn=512, name="sc_out_proj")


def _split_hi_lo(x):
    hi = x.astype(BF16)
    lo = (x - hi.astype(F32)).astype(BF16)
    return hi, lo


def _head_rms_scale(x, seg_ref, segt_ref):
    hi, lo = _split_hi_lo(x * x)
    seg = seg_ref[...]
    ss = jnp.dot(hi, seg, preferred_element_type=F32) + jnp.dot(lo, seg, preferred_element_type=F32)
    inv = lax.rsqrt(ss * (1.0 / GQA_HEAD_DIM) + EPS)
    ihi, ilo = _split_hi_lo(inv)
    segt = segt_ref[...]
    return jnp.dot(ihi, segt, preferred_element_type=F32) + jnp.dot(ilo, segt, preferred_element_type=F32)


def _gqa_in_proj(h, w_in, b_in, qg_t, kg_t, seg, segt, tabs, *, l_tot):
    m = h.shape[0]
    n = w_in.shape[1]
    tm = _row_tile(l_tot, 1088)
    tn = 512
    tpb = l_tot // tm
    n_q_tiles = GQA_HEADS * GQA_HEAD_DIM // tn
    kv_w = GQA_KV_HEADS * GQA_HEAD_DIM
    assert n == n_q_tiles * tn + 2 * kv_w and 2 * kv_w == tn
    scale = GQA_HEAD_DIM ** -0.5 * LOG2_E

    def norm_rope(xs, g, extra, mult):
        seg_ref, segt_ref, c_ref, s1_ref, s2_ref = extra[3:8]
        y = xs * _head_rms_scale(xs, seg_ref, segt_ref) * g
        c, s1, s2 = c_ref[...], s1_ref[...], s2_ref[...]
        return [_rope3(y[:, a * LANES:(a + 1) * LANES], c, s1, s2) * mult
                for a in range(xs.shape[1] // LANES)]

    def dup_heads(slab, lo_mask):
        swapped = pltpu.roll(slab, GQA_HEAD_DIM, axis=1)
        return jnp.where(lo_mask, slab, swapped), jnp.where(lo_mask, swapped, slab)

    def epilogue(acc, extra, outs):
        b_ref, qg_ref, kg_ref = extra[0:3]
        j = pl.program_id(1)
        acc = acc + b_ref[...]

        @pl.when(j < n_q_tiles)
        def _():
            for a, slab in enumerate(norm_rope(acc, qg_ref[...], extra, scale)):
                outs[0][:, a * LANES:(a + 1) * LANES] = slab.astype(BF16)

        @pl.when(j == n_q_tiles)
        def _():
            lo_mask = lax.broadcasted_iota(jnp.int32, (tm, LANES), 1) < GQA_HEAD_DIM
            kpad = jnp.concatenate([acc[:, :kv_w], acc[:, :kv_w]], axis=1)
            k_slabs = norm_rope(kpad, kg_ref[...], extra, 1.0)[:kv_w // LANES]
            for a in range(kv_w // LANES):
                ka, kb = dup_heads(k_slabs[a], lo_mask)
                outs[1][:, (2 * a) * LANES:(2 * a + 1) * LANES] = ka.astype(BF16)
                outs[1][:, (2 * a + 1) * LANES:(2 * a + 2) * LANES] = kb.astype(BF16)
                va, vb = dup_heads(acc[:, kv_w + a * LANES:kv_w + (a + 1) * LANES], lo_mask)
                outs[2][:, (2 * a) * LANES:(2 * a + 1) * LANES] = va.astype(BF16)
                outs[2][:, (2 * a + 1) * LANES:(2 * a + 2) * LANES] = vb.astype(BF16)

    tspec = pl.BlockSpec((tm, LANES), lambda i, j: (lax.rem(i, tpb), 0))
    full = lambda shape: pl.BlockSpec(shape, lambda i, j: (0, 0))
    return _matmul(
        h, w_in, [b_in.reshape(1, n), qg_t, kg_t, seg, segt] + list(tabs),
        [pl.BlockSpec((1, tn), lambda i, j: (0, j)), full((1, tn)), full((1, tn)),
         full((tn, LANES)), full((LANES, tn)), tspec, tspec, tspec],
        [jax.ShapeDtypeStruct((m, n_q_tiles * tn), BF16),
         jax.ShapeDtypeStruct((m, GQA_KV_HEADS * LANES), BF16),
         jax.ShapeDtypeStruct((m, GQA_KV_HEADS * LANES), BF16)],
        [pl.BlockSpec((tm, tn), lambda i, j: (i, jnp.minimum(j, n_q_tiles - 1))),
         pl.BlockSpec((tm, GQA_KV_HEADS * LANES), lambda i, j: (i, 0)),
         pl.BlockSpec((tm, GQA_KV_HEADS * LANES), lambda i, j: (i, 0))],
        epilogue, tm=tm, tn=tn, name="gqa_in_proj")


def _gqa_attn_body(sink_ref, q_ref, k_ref, v_ref, o_ref, *, lc, nb):
    g = pl.program_id(1)
    r = WINDOW
    n_slab = GQA_GROUP // 2
    rows_all = GQA_GROUP * r
    row_head = lax.broadcasted_iota(jnp.int32, (rows_all, 1), 0) // r
    sink = jnp.zeros((rows_all, 1), F32)
    for hh in range(GQA_GROUP):
        sink = jnp.where(row_head == hh, sink_ref[g * GQA_GROUP + hh] * LOG2_E, sink)
    lo_mask = lax.broadcasted_iota(jnp.int32, (r, LANES), 1) < GQA_HEAD_DIM
    ii = lax.rem(lax.broadcasted_iota(jnp.int32, (rows_all, r), 0), r)
    jj = lax.broadcasted_iota(jnp.int32, (rows_all, r), 1)
    k_ctx = k_ref[0, 0:lc, :]
    v_ctx = v_ref[0, 0:lc, :]

    def stack_q(rows):
        parts = []
        for a in range(n_slab):
            qa = q_ref[0, rows, a * LANES:(a + 1) * LANES]
            zero = jnp.zeros_like(qa)
            parts += [jnp.where(lo_mask, qa, zero), jnp.where(lo_mask, zero, qa)]
        return jnp.concatenate(parts, axis=0)

    def attend(rows, blocks):
        qs = stack_q(rows)
        scores = []
        for kb, _, mask in blocks:
            s = lax.dot_general(qs, kb, (((1,), (1,)), ((), ())), preferred_element_type=F32)
            scores.append(s if mask is None else jnp.where(mask, s, NEG_INF))
        slabs = lambda a: [a[:, c * LANES:(c + 1) * LANES] for c in range(a.shape[1] // LANES)]
        mx = functools.reduce(jnp.maximum, [sl for s in scores for sl in slabs(s)])
        mx = jnp.maximum(sink, jnp.max(mx, axis=-1, keepdims=True))
        psum = jnp.zeros((rows_all, LANES), F32)
        acc = jnp.zeros((rows_all, LANES), F32)
        for s, (_, vb, _) in zip(scores, blocks):
            p = jnp.exp2(s - mx)
            psum = functools.reduce(jnp.add, slabs(p), psum)
            acc = acc + jnp.dot(p.astype(BF16), vb, preferred_element_type=F32)
        o = acc / (jnp.exp2(sink - mx) + jnp.sum(psum, axis=-1, keepdims=True))
        for a in range(n_slab):
            oa = jnp.where(lo_mask, o[(2 * a) * r:(2 * a + 1) * r], o[(2 * a + 1) * r:(2 * a + 2) * r])
            o_ref[0, rows, a * LANES:(a + 1) * LANES] = oa.astype(BF16)

    for cb in range(lc // r):
        attend(pl.ds(cb * r, r), [(k_ctx, v_ctx, None)])

    def step(t, carry):
        r_cur = pl.multiple_of(lc + t * r, r)
        r_prev = pl.multiple_of(lc + jnp.maximum(t - 1, 0) * r, r)
        r_next = pl.multiple_of(lc + jnp.minimum(t + 1, nb - 1) * r, r)
        blk = lambda r0: (k_ref[0, pl.ds(r0, r), :], v_ref[0, pl.ds(r0, r), :])
        m_prev = jnp.logical_and(jj >= ii, t > 0)
        m_next = jnp.logical_and(jj <= ii, t < nb - 1)
        attend(pl.ds(r_cur, r), [blk(r_prev) + (m_prev,), blk(r_cur) + (None,),
                                 blk(r_next) + (m_next,), (k_ctx, v_ctx, None)])
        return carry

    lax.fori_loop(0, nb, step, 0, unroll=2)


def _gqa_attention(q, k, v, sink, *, b, l_tot, lc):
    gw = GQA_GROUP * GQA_HEAD_DIM
    q3 = q.reshape(b, l_tot, GQA_HEADS * GQA_HEAD_DIM)
    k3 = k.reshape(b, l_tot, GQA_KV_HEADS * LANES)
    v3 = v.reshape(b, l_tot, GQA_KV_HEADS * LANES)
    body = functools.partial(_gqa_attn_body, lc=lc, nb=(l_tot - lc) // WINDOW)
    return pl.pallas_call(
        body,
        grid=(b, GQA_KV_HEADS),
        in_specs=[pl.BlockSpec(memory_space=pltpu.SMEM),
                  pl.BlockSpec((1, l_tot, gw), lambda bi, gi: (bi, 0, gi)),
                  pl.BlockSpec((1, l_tot, LANES), lambda bi, gi: (bi, 0, gi)),
                  pl.BlockSpec((1, l_tot, LANES), lambda bi, gi: (bi, 0, gi))],
        out_specs=pl.BlockSpec((1, l_tot, gw), lambda bi, gi: (bi, 0, gi)),
        out_shape=jax.ShapeDtypeStruct((b, l_tot, GQA_HEADS * GQA_HEAD_DIM), BF16),
        compiler_params=_params(2),
        name="gqa_attention",
    )(sink, q3, k3, v3)


def _gqa_layer(x, h, cv, lv, w_in, b_in, q_gain, k_gain, sink, w_out, b_out, tabs, *, b, l_tot, lc):
    m, d = x.shape
    tn = 512
    heads_per_tile = tn // GQA_HEAD_DIM
    col = jnp.arange(tn)
    seg = (col[:, None] // GQA_HEAD_DIM == jnp.arange(LANES)[None, :]).astype(BF16)
    qg_t = jnp.tile(q_gain, heads_per_tile).reshape(1, tn)
    kg_t = jnp.tile(k_gain, heads_per_tile).reshape(1, tn)
    q, k, v = _gqa_in_proj(h, w_in, b_in, qg_t, kg_t, seg, seg.T, tabs, l_tot=l_tot)
    o = _gqa_attention(q, k, v, sink.astype(F32), b=b, l_tot=l_tot, lc=lc).reshape(m, d)
    return _residual_matmul(o, w_out, b_out, x, cv, lv, l_tot=l_tot, lc=lc, r_gate=2,
                            tm=_row_tile(l_tot, 2176), tn=512, name="gqa_out_proj")


def _pack_halves(yb):
    w = yb.shape[1] // 2
    lo = lax.bitcast_convert_type(yb[:, :w].astype(F32), jnp.uint32)
    hi = lax.bitcast_convert_type(yb[:, w:].astype(F32), jnp.uint32)
    return lax.shift_right_logical(lo, jnp.uint32(16)) | hi


def _unpack_halves(words):
    lo = lax.bitcast_convert_type(lax.shift_left(words, jnp.uint32(16)), F32).astype(BF16)
    hi = lax.bitcast_convert_type(words & jnp.uint32(0xFFFF0000), F32).astype(BF16)
    return lo, hi


def _ffn_route_body(x_ref, g_ref, cv_ref, lv_ref, rw_ref, rb_ref, fpk_ref, ri_ref, rg_ref, cnt_ref,
                    carry_ref, *, tm, tpb, lc, route_ctx, n_e):
    i = pl.program_id(0)

    @pl.when(i == 0)
    def _():
        carry_ref[...] = jnp.zeros_like(carry_ref)

    x = x_ref[...]
    y = x * lax.rsqrt(jnp.mean(x * x, axis=-1, keepdims=True) + EPS) * g_ref[...]
    is_ctx = _is_ctx_rows(i, tm, tpb, lc)
    y = y * (1.0 + _pick_mod(is_ctx, cv_ref, lv_ref, 4)) + _pick_mod(is_ctx, cv_ref, lv_ref, 3)
    yb = y.astype(BF16)
    fpk_ref[...] = _pack_halves(yb)

    lane = lax.broadcasted_iota(jnp.int32, (tm, LANES), 1)
    logits = jnp.dot(yb, rw_ref[...].astype(BF16), preferred_element_type=F32) + rb_ref[...]
    logits = jnp.where(lane < n_e, logits, NEG_INF)
    vals, idxs, hots = [], [], []
    for _ in range(TOP_K):
        mk = jnp.max(logits, axis=-1, keepdims=True)
        ik = jnp.min(jnp.where(logits == mk, lane, LANES), axis=-1, keepdims=True)
        hot = lane == ik
        vals.append(mk)
        idxs.append(ik)
        hots.append(hot)
        logits = jnp.where(hot, NEG_INF, logits)
    exps = [jnp.exp(v - vals[0]) for v in vals]
    den = exps[0] + exps[1] + exps[2] + exps[3]

    routed = jnp.ones_like(is_ctx) if route_ctx else jnp.logical_not(is_ctx)
    chosen = jnp.zeros((tm, LANES), F32)
    for hot in hots:
        chosen = chosen + jnp.where(jnp.logical_and(hot, routed), 1.0, 0.0)
    before = (lax.broadcasted_iota(jnp.int32, (tm, tm), 0)
              > lax.broadcasted_iota(jnp.int32, (tm, tm), 1)).astype(BF16)
    arrived = jnp.dot(before, chosen.astype(BF16), preferred_element_type=F32) + carry_ref[...]
    carry_ref[...] = carry_ref[...] + jnp.sum(chosen, axis=0, keepdims=True)
    cnt_ref[...] = carry_ref[...]

    ri = jnp.zeros((tm, LANES), jnp.int32)
    rg = jnp.zeros((tm, LANES), F32)
    for k in range(TOP_K):
        rank = jnp.sum(jnp.where(hots[k], arrived, 0.0), axis=-1, keepdims=True)
        ri = jnp.where(lane == k, idxs[k], ri)
        ri = jnp.where(lane == TOP_K + k, rank.astype(jnp.int32), ri)
        rg = jnp.where(lane == k, exps[k] / den, rg)
    ri_ref[...] = ri
    rg_ref[...] = rg


def _ffn_route(x, gain, cv, lv, router_w_pad, router_b_pad, layer, *, l_tot, lc, route_ctx, n_e):
    m, d = x.shape
    tm = _row_tile(l_tot, 544)
    tpb = l_tot // tm
    body = functools.partial(_ffn_route_body, tm=tm, tpb=tpb, lc=lc, route_ctx=route_ctx, n_e=n_e)
    return pl.pallas_call(
        body,
        grid=(m // tm,),
        in_specs=[pl.BlockSpec((tm, d), lambda i: (i, 0)),
                  pl.BlockSpec((1, d), lambda i: (0, 0)),
                  pl.BlockSpec((MOD_ROWS, d), lambda i: (0, 0)),
                  pl.BlockSpec((1, MOD_ROWS, d), lambda i: (i // tpb, 0, 0)),
                  pl.BlockSpec((None, d, LANES), lambda i: (layer, 0, 0)),
                  pl.BlockSpec((None, 1, LANES), lambda i: (layer, 0, 0))],
        out_specs=[pl.BlockSpec((tm, d // 2), lambda i: (i, 0)),
                   pl.BlockSpec((tm, LANES), lambda i: (i, 0)),
                   pl.BlockSpec((tm, LANES), lambda i: (i, 0)),
                   pl.BlockSpec((1, LANES), lambda i: (0, 0))],
        out_shape=[jax.ShapeDtypeStruct((m, d // 2), jnp.uint32),
                   jax.ShapeDtypeStruct((m, LANES), jnp.int32),
                   jax.ShapeDtypeStruct((m, LANES), F32),
                   jax.ShapeDtypeStruct((1, LANES), F32)],
        scratch_shapes=[pltpu.VMEM((1, LANES), F32)],
        compiler_params=_params(1),
        name="ffn_route",
    )(x, gain.reshape(1, d), cv, lv, router_w_pad, router_b_pad)


def _dispatch_body(dest_ref, f_ref, init_hbm, xr_hbm, sem, *, tm, tpb, ctx_tiles, route_ctx):
    del init_hbm
    i = pl.program_id(0)

    def scatter_rows():
        def start(r, carry):
            a0 = (i * tm + r) * TOP_K
            for k in range(TOP_K):
                pltpu.make_async_copy(f_ref.at[pl.ds(r, 1)], xr_hbm.at[pl.ds(dest_ref[a0 + k], 1)], sem).start()
            return carry

        lax.fori_loop(0, tm, start, 0, unroll=ISSUE_UNROLL)

        def wait(r, carry):
            for k in range(TOP_K):
                pltpu.make_async_copy(f_ref.at[pl.ds(0, 1)], xr_hbm.at[pl.ds(0, 1)], sem).wait()
            return carry

        lax.fori_loop(0, tm, wait, 0, unroll=ISSUE_UNROLL)

    if route_ctx:
        scatter_rows()
    else:
        pl.when(lax.rem(i, tpb) >= ctx_tiles)(scatter_rows)


def _dispatch(fpk, dest, n_rows, *, l_tot, lc, route_ctx):
    m, w = fpk.shape
    tm = 256
    assert lc % tm == 0 and l_tot % tm == 0
    body = functools.partial(_dispatch_body, tm=tm, tpb=l_tot // tm, ctx_tiles=lc // tm, route_ctx=route_ctx)
    return pl.pallas_call(
        body,
        grid_spec=pltpu.PrefetchScalarGridSpec(
            num_scalar_prefetch=1,
            grid=(m // tm,),
            in_specs=[pl.BlockSpec((tm, w), lambda i, dst: (i, 0)),
                      pl.BlockSpec(memory_space=pl.ANY)],
            out_specs=pl.BlockSpec(memory_space=pl.ANY),
            scratch_shapes=[pltpu.SemaphoreType.DMA(())]),
        out_shape=jax.ShapeDtypeStruct((n_rows, w), jnp.uint32),
        input_output_aliases={2: 0},
        compiler_params=_row_copy_params(),
        name="moe_dispatch",
    )(dest, fpk, jnp.zeros((n_rows, w), jnp.uint32))


def _stream_expert_rows(row0, nb, src_hbm, dst_hbm, inbuf, outbuf, sin, sout, compute):
    def rows(j):
        return pl.ds(pl.multiple_of(row0 + j * MOE_TM, MOE_TM), MOE_TM)

    def in_copy(j, slot):
        return pltpu.make_async_copy(src_hbm.at[rows(j)], inbuf.at[slot], sin.at[slot])

    def out_copy(j, slot):
        return pltpu.make_async_copy(outbuf.at[slot], dst_hbm.at[rows(j)], sout.at[slot])

    in_copy(0, 0).start()

    def block(j, carry):
        slot = lax.rem(j, 2)

        @pl.when(j + 1 < nb)
        def _():
            in_copy(j + 1, 1 - slot).start()

        in_copy(j, slot).wait()

        @pl.when(j >= 2)
        def _():
            out_copy(j - 2, slot).wait()

        outbuf[slot] = compute(inbuf[slot])
        out_copy(j, slot).start()
        return carry

    lax.fori_loop(0, nb, block, 0)

    @pl.when(nb >= 2)
    def _():
        out_copy(nb - 2, lax.rem(nb, 2)).wait()

    out_copy(nb - 1, lax.rem(nb - 1, 2)).wait()


def _zero_unused_rows(ps_ref, nb_ref, dst_hbm, outbuf, sout):
    e = pl.program_id(0)
    n_total = dst_hbm.shape[0] // MOE_TM

    @pl.when(e == pl.num_programs(0) - 1)
    def _():
        first = ps_ref[e] // MOE_TM + nb_ref[e]
        outbuf[0] = jnp.zeros(outbuf.shape[1:], outbuf.dtype)

        def tail_copy(j):
            rows = pl.ds(pl.multiple_of(j * MOE_TM, MOE_TM), MOE_TM)
            return pltpu.make_async_copy(outbuf.at[0], dst_hbm.at[rows], sout.at[0])

        def start(j, carry):
            tail_copy(j).start()
            return carry

        def wait(j, carry):
            tail_copy(j).wait()
            return carry

        lax.fori_loop(first, n_total, start, 0)
        lax.fori_loop(first, n_total, wait, 0)


def _moe_gu_body(ps_ref, nb_ref, wgu_ref, bgu_ref, x_hbm, act_hbm,
                 wgb_ref, wlb_ref, xbuf, obuf, sin, sout):
    e = pl.program_id(0)
    ff = wgb_ref.shape[1]

    def compute(words):
        x_lo, x_hi = _unpack_halves(words)
        w = x_lo.shape[1]
        h_glu = (jnp.dot(x_lo, wgb_ref[0:w, :], preferred_element_type=F32)
                 + jnp.dot(x_hi, wgb_ref[w:, :], preferred_element_type=F32) + bgu_ref[:, :ff])
        h_lin = (jnp.dot(x_lo, wlb_ref[0:w, :], preferred_element_type=F32)
                 + jnp.dot(x_hi, wlb_ref[w:, :], preferred_element_type=F32) + bgu_ref[:, ff:])
        h_glu = jnp.minimum(h_glu, SWIGLU_LIMIT)
        h_lin = jnp.clip(h_lin, -SWIGLU_LIMIT, SWIGLU_LIMIT)
        return (h_glu * _sigmoid(SWIGLU_ALPHA * h_glu) * (h_lin + 1.0)).astype(BF16)

    @pl.when(nb_ref[e] > 0)
    def _():
        wgb_ref[...] = wgu_ref[:, :ff].astype(BF16)
        wlb_ref[...] = wgu_ref[:, ff:].astype(BF16)
        _stream_expert_rows(ps_ref[e], nb_ref[e], x_hbm, act_hbm, xbuf, obuf, sin, sout, compute)

    _zero_unused_rows(ps_ref, nb_ref, act_hbm, obuf, sout)


def _moe_down_body(ps_ref, nb_ref, w_ref, b_ref, act_hbm, y_hbm, wb_ref, abuf, ybuf, sin, sout):
    e = pl.program_id(0)

    def compute(act):
        return jnp.dot(act, wb_ref[...], preferred_element_type=F32) + b_ref[...]

    @pl.when(nb_ref[e] > 0)
    def _():
        wb_ref[...] = w_ref[...].astype(BF16)
        _stream_expert_rows(ps_ref[e], nb_ref[e], act_hbm, y_hbm, abuf, ybuf, sin, sout, compute)

    _zero_unused_rows(ps_ref, nb_ref, y_hbm, ybuf, sout)


def _moe_experts(x_rows, pstart, n_blk, w_gu, b_gu, w_down, b_down, layer):
    n_rows, w = x_rows.shape
    depth, n_e, ff, d = w_down.shape
    b_gu4 = b_gu.reshape(depth, n_e, 1, 2 * ff)
    dma2 = pltpu.SemaphoreType.DMA((2,))
    act = pl.pallas_call(
        _moe_gu_body,
        grid_spec=pltpu.PrefetchScalarGridSpec(
            num_scalar_prefetch=2,
            grid=(n_e,),
            in_specs=[pl.BlockSpec((None, None, d, 2 * ff), lambda e, ps, nb: (layer, e, 0, 0)),
                      pl.BlockSpec((None, None, 1, 2 * ff), lambda e, ps, nb: (layer, e, 0, 0)),
                      pl.BlockSpec(memory_space=pl.ANY)],
            out_specs=pl.BlockSpec(memory_space=pl.ANY),
            scratch_shapes=[pltpu.VMEM((d, ff), BF16), pltpu.VMEM((d, ff), BF16),
                            pltpu.VMEM((2, MOE_TM, w), jnp.uint32), pltpu.VMEM((2, MOE_TM, ff), BF16),
                            dma2, dma2]),
        out_shape=jax.ShapeDtypeStruct((n_rows, ff), BF16),
        compiler_params=_params(1),
        name="moe_gate_up",
    )(pstart, n_blk, w_gu, b_gu4, x_rows)
    return pl.pallas_call(
        _moe_down_body,
        grid_spec=pltpu.PrefetchScalarGridSpec(
            num_scalar_prefetch=2,
            grid=(n_e,),
            in_specs=[pl.BlockSpec((None, None, ff, d), lambda e, ps, nb: (layer, e, 0, 0)),
                      pl.BlockSpec((None, None, 1, d), lambda e, ps, nb: (layer, e, 0, 0)),
                      pl.BlockSpec(memory_space=pl.ANY)],
            out_specs=pl.BlockSpec(memory_space=pl.ANY),
            scratch_shapes=[pltpu.VMEM((ff, d), BF16),
                            pltpu.VMEM((2, MOE_TM, ff), BF16), pltpu.VMEM((2, MOE_TM, d), F32),
                            dma2, dma2]),
        out_shape=jax.ShapeDtypeStruct((n_rows, d), F32),
        compiler_params=_params(1),
        name="moe_down",
    )(pstart, n_blk, w_down, b_down.reshape(depth, n_e, 1, d), act)


def _combine_body(dest_ref, x_ref, rg_ref, cv_ref, lv_ref, y_hbm, o_ref, buf, sem,
                  *, tt, n_tiles, tpb, ctx_tiles, lc, route_ctx):
    i = pl.program_id(0)

    def routed(tile):
        return (tile >= 0) if route_ctx else (lax.rem(tile, tpb) >= ctx_tiles)

    def row_copy(src, slot, k, r):
        return pltpu.make_async_copy(y_hbm.at[pl.ds(src, 1)], buf.at[slot, k, pl.ds(r, 1)], sem.at[slot])

    def issue(tile, slot):
        def start(r, carry):
            a0 = (tile * tt + r) * TOP_K
            for k in range(TOP_K):
                row_copy(dest_ref[a0 + k], slot, k, r).start()
            return carry

        lax.fori_loop(0, tt, start, 0, unroll=ISSUE_UNROLL)

    @pl.when(jnp.logical_and(i == 0, routed(i)))
    def _():
        issue(i, 0)

    nxt = jnp.minimum(i + 1, n_tiles - 1)

    @pl.when(jnp.logical_and(i + 1 < n_tiles, routed(nxt)))
    def _():
        issue(nxt, lax.rem(i + 1, 2))

    slot = lax.rem(i, 2)

    @pl.when(routed(i))
    def _():
        def wait(r, carry):
            for k in range(TOP_K):
                row_copy(0, slot, k, 0).wait()
            return carry

        lax.fori_loop(0, tt, wait, 0, unroll=ISSUE_UNROLL)
        rg = rg_ref[...]
        acc = rg[:, 0:1] * buf[slot, 0]
        for k in range(1, TOP_K):
            acc = acc + rg[:, k:k + 1] * buf[slot, k]
        gate = _pick_mod(_is_ctx_rows(i, tt, tpb, lc), cv_ref, lv_ref, 5)
        o_ref[...] = x_ref[...] + gate * acc

    @pl.when(jnp.logical_not(routed(i)))
    def _():
        o_ref[...] = x_ref[...]


def _combine(x, y_rows, rg, dest, cv, lv, *, l_tot, lc, route_ctx):
    m, d = x.shape
    tt = 128
    assert lc % tt == 0 and l_tot % tt == 0
    tpb = l_tot // tt
    n_tiles = m // tt
    body = functools.partial(_combine_body, tt=tt, n_tiles=n_tiles, tpb=tpb, ctx_tiles=lc // tt, lc=lc,
                             route_ctx=route_ctx)
    return pl.pallas_call(
        body,
        grid_spec=pltpu.PrefetchScalarGridSpec(
            num_scalar_prefetch=1,
            grid=(n_tiles,),
            in_specs=[pl.BlockSpec((tt, d), lambda i, dst: (i, 0)),
                      pl.BlockSpec((tt, LANES), lambda i, dst: (i, 0)),
                      pl.BlockSpec((MOD_ROWS, d), lambda i, dst: (0, 0)),
                      pl.BlockSpec((1, MOD_ROWS, d), lambda i, dst: (i // tpb, 0, 0)),
                      pl.BlockSpec(memory_space=pl.ANY)],
            out_specs=pl.BlockSpec((tt, d), lambda i, dst: (i, 0)),
            scratch_shapes=[pltpu.VMEM((2, TOP_K, tt, d), F32), pltpu.SemaphoreType.DMA((2,))]),
        out_shape=jax.ShapeDtypeStruct((m, d), F32),
        compiler_params=_row_copy_params(),
        name="moe_combine",
    )(dest, x, rg, cv, lv, y_rows)


def _moe_layer(x, gain, cv, lv, router_w_pad, router_b_pad, w_gu, b_gu, w_down, b_down, layer,
               *, l_tot, lc, route_ctx):
    m, d = x.shape
    n_e = w_gu.shape[1]
    fpk, ri, rg, cnt = _ffn_route(x, gain, cv, lv, router_w_pad, router_b_pad, layer,
                                  l_tot=l_tot, lc=lc, route_ctx=route_ctx, n_e=n_e)
    counts = cnt[0, :n_e].astype(jnp.int32)
    n_blk = (counts + MOE_TM - 1) // MOE_TM
    pstart = (jnp.cumsum(n_blk) - n_blk) * MOE_TM
    n_rows = (-(-m * TOP_K // MOE_TM) + n_e) * MOE_TM
    dest = (pstart[ri[:, :TOP_K]] + ri[:, TOP_K:2 * TOP_K]).reshape(m * TOP_K)
    x_rows = _dispatch(fpk, dest, n_rows, l_tot=l_tot, lc=lc, route_ctx=route_ctx)
    y_rows = _moe_experts(x_rows, pstart, n_blk, w_gu, b_gu, w_down, b_down, layer)
    return _combine(x, y_rows, rg, dest, cv, lv, l_tot=l_tot, lc=lc, route_ctx=route_ctx)


def _rope_1d_tables(l_tot, half):
    inv = ROPE_BASE ** (-jnp.arange(half, dtype=F32) / half)
    ang = jnp.arange(l_tot, dtype=F32)[:, None] * inv
    return jnp.cos(ang), jnp.sin(ang)


def _axial_tables(l_tot, lc):
    n_freq = GQA_HEAD_DIM // 4
    t = jnp.maximum(jnp.arange(l_tot) - lc, 0)
    pos = jnp.stack([(t // GRID_W).astype(F32), (t % GRID_W).astype(F32)], axis=-1)
    inv_freq = ROPE_BASE ** (-jnp.arange(n_freq, dtype=F32) / n_freq)
    ang = pos[:, :, None] * inv_freq
    lane = jnp.arange(LANES)
    axis = (lane % GQA_HEAD_DIM) // (2 * n_freq)
    freq = lane % n_freq
    first_half = (lane % (2 * n_freq)) < n_freq
    is_lat = (jnp.arange(l_tot) >= lc)[:, None]
    cos = jnp.where(is_lat, jnp.cos(ang)[:, axis, freq], 1.0)
    sin = jnp.where(is_lat, jnp.sin(ang)[:, axis, freq], 0.0)
    return cos, jnp.where(first_half, -sin, 0.0), jnp.where(first_half, 0.0, sin)


def kernel(x, c, ctx, c_ctx, norm_mix, norm_ffn, mod_w, mod_b, ret_w_in, ret_decay, ret_gn, ret_w_out, mla_w_in, mla_q_a_norm, mla_kv_a_norm, mla_w_q_b, mla_w_kv_b, mla_q_norm, mla_k_norm, mla_w_out, sc_w_in, sc_conv_w, sc_conv_b, sc_w_out, gqa_w_in, gqa_b_in, gqa_q_norm, gqa_k_norm, gqa_sink, gqa_w_out, gqa_b_out, moe_router_w, moe_router_b, moe_w_gu, moe_b_gu, moe_w_down, moe_b_down):
    b, n_lat, d = x.shape
    lc = ctx.shape[1]
    l_tot = lc + n_lat
    m = b * l_tot
    depth = mod_w.shape[0]
    assert b + 1 <= MOD_ROWS and lc % 256 == 0 and n_lat % 256 == 0

    xs = jnp.concatenate([ctx, x], axis=1).reshape(m, d)
    cvec = jnp.zeros((MOD_ROWS, d), F32).at[:b].set(c).at[b].set(c_ctx)
    mod = _modulation(cvec, mod_w, mod_b).reshape(depth, MOD_ROWS, 6, d)
    pad_rows = ((0, 0), (0, MOD_ROWS - 6), (0, 0))
    ret_cos, ret_sin = _rope_1d_tables(l_tot, d // RET_HEADS // 2)
    tabs = _axial_tables(l_tot, lc)
    n_e = moe_router_w.shape[-1]
    router_w_pad = jnp.pad(moe_router_w, ((0, 0), (0, 0), (0, LANES - n_e)))
    router_b_pad = jnp.pad(moe_router_b, ((0, 0), (0, LANES - n_e))).reshape(depth, 1, LANES)

    for i in range(depth):
        kind, j = i % N_MIXERS, i // N_MIXERS
        cv = jnp.pad(mod[i, b], pad_rows[1:])
        lv = jnp.pad(mod[i, :b], pad_rows)
        lay = dict(b=b, l_tot=l_tot, lc=lc)
        h = _norm_mod(xs, norm_mix[i], cv, lv, l_tot=l_tot, lc=lc, r_shift=0, r_scale=1)
        if kind == 0:
            xs = _retention_layer(xs, h, cv, lv, ret_w_in[j], ret_decay[j], ret_gn[j], ret_w_out[j],
                                  ret_cos, ret_sin, **lay)
        elif kind == 1:
            xs = _mla_layer(xs, h, cv, lv, mla_w_in[j], mla_q_a_norm[j], mla_kv_a_norm[j], mla_w_q_b[j],
                            mla_w_kv_b[j], mla_q_norm[j], mla_k_norm[j], mla_w_out[j], tabs, **lay)
        elif kind == 2:
            xs = _shortconv_layer(xs, h, cv, lv, sc_w_in[j], sc_conv_w[j], sc_conv_b[j], sc_w_out[j], **lay)
        else:
            xs = _gqa_layer(xs, h, cv, lv, gqa_w_in[j], gqa_b_in[j], gqa_q_norm[j], gqa_k_norm[j],
                            gqa_sink[j], gqa_w_out[j], gqa_b_out[j], tabs, **lay)
        xs = _moe_layer(xs, norm_ffn[i], cv, lv, router_w_pad, router_b_pad, moe_w_gu, moe_b_gu,
                        moe_w_down, moe_b_down, i, l_tot=l_tot, lc=lc, route_ctx=i < depth - 1)
    return xs.reshape(b, l_tot, d)[:, lc:, :]
```

```python
import functools

import jax
import jax.numpy as jnp
from jax import lax
from jax.experimental import pallas as pl
from jax.experimental.pallas import tpu as pltpu

F32 = jnp.float32
BF16 = jnp.bfloat16

EPS = 1e-6
ROPE_BASE = 10000.0
NEG_INF = -1e30
LOG2_E = 1.4426950408889634
GRID_W = 64

RET_HEADS = 8
RET_CHUNK = 256
MLA_HEADS = 16
MLA_NOPE = 128
MLA_ROPE = 64
MLA_V = 128
MLA_RANK = 512
GQA_HEADS = 32
GQA_KV_HEADS = 4
GQA_GROUP = GQA_HEADS // GQA_KV_HEADS
GQA_HEAD_DIM = 64
WINDOW = 128
N_EXPERTS = 32
TOP_K = 4
SWIGLU_LIMIT = 7.0
SWIGLU_ALPHA = 1.702
N_MIXERS = 4

LANES = 128
VMEM_LIMIT = 56 * 1024 * 1024
MOE_TM = 256
MOD_ROWS = 8
ISSUE_UNROLL = 4


def _params(n_axes, vmem=VMEM_LIMIT):
    return pltpu.CompilerParams(dimension_semantics=("arbitrary",) * n_axes,
                                vmem_limit_bytes=vmem)


def _row_copy_params():
    return pltpu.CompilerParams(dimension_semantics=("arbitrary",), vmem_limit_bytes=VMEM_LIMIT,
                                disable_bounds_checks=True)


def _row_tile(l_tot, cap):
    best = 16
    for t in range(16, cap + 1, 16):
        if l_tot % t == 0:
            best = t
    return best


def _is_ctx_rows(i, tm, tpb, lc):
    rows = lax.rem(i, tpb) * tm + lax.broadcasted_iota(jnp.int32, (tm, 1), 0)
    return rows < lc


def _pick_mod(is_ctx, cv_ref, lv_ref, r):
    return jnp.where(is_ctx, cv_ref[r:r + 1, :], lv_ref[0, r:r + 1, :])


def _sigmoid(x):
    return 1.0 / (1.0 + jnp.exp(-x))


def _rope3(x, c, s1, s2):
    return x * c + pltpu.roll(x, LANES - 16, axis=1) * s1 + pltpu.roll(x, 16, axis=1) * s2


def _mod_body(c_ref, w_ref, b_ref, o_ref):
    c = c_ref[...]
    s = (c * _sigmoid(c)).astype(BF16)
    o_ref[0] = jnp.dot(s, w_ref[0].astype(BF16), preferred_element_type=F32) + b_ref[0]


def _modulation(cvec, mod_w, mod_b):
    depth, d, n = mod_w.shape
    tn = 2048
    return pl.pallas_call(
        _mod_body,
        grid=(depth, n // tn),
        in_specs=[pl.BlockSpec((MOD_ROWS, d), lambda l, j: (0, 0)),
                  pl.BlockSpec((1, d, tn), lambda l, j: (l, 0, j)),
                  pl.BlockSpec((1, 1, tn), lambda l, j: (l, 0, j))],
        out_specs=pl.BlockSpec((1, MOD_ROWS, tn), lambda l, j: (l, 0, j)),
        out_shape=jax.ShapeDtypeStruct((depth, MOD_ROWS, n), F32),
        compiler_params=_params(2),
        name="modulation",
    )(cvec, mod_w, mod_b.reshape(depth, 1, n))


def _norm_mod_body(x_ref, g_ref, cv_ref, lv_ref, o_ref, *, tm, tpb, lc, r_shift, r_scale):
    x = x_ref[...]
    y = x * lax.rsqrt(jnp.mean(x * x, axis=-1, keepdims=True) + EPS) * g_ref[...]
    is_ctx = _is_ctx_rows(pl.program_id(0), tm, tpb, lc)
    shift = _pick_mod(is_ctx, cv_ref, lv_ref, r_shift)
    scale = _pick_mod(is_ctx, cv_ref, lv_ref, r_scale)
    o_ref[...] = (y * (1.0 + scale) + shift).astype(BF16)


def _norm_mod(x, gain, cv, lv, *, l_tot, lc, r_shift, r_scale):
    m, d = x.shape
    tm = _row_tile(l_tot, 544)
    tpb = l_tot // tm
    body = functools.partial(_norm_mod_body, tm=tm, tpb=tpb, lc=lc, r_shift=r_shift, r_scale=r_scale)
    return pl.pallas_call(
        body,
        grid=(m // tm,),
        in_specs=[pl.BlockSpec((tm, d), lambda i: (i, 0)),
                  pl.BlockSpec((1, d), lambda i: (0, 0)),
                  pl.BlockSpec((MOD_ROWS, d), lambda i: (0, 0)),
                  pl.BlockSpec((1, MOD_ROWS, d), lambda i: (i // tpb, 0, 0))],
        out_specs=pl.BlockSpec((tm, d), lambda i: (i, 0)),
        out_shape=jax.ShapeDtypeStruct((m, d), BF16),
        compiler_params=_params(1),
        name="norm_mod",
    )(x, gain.reshape(1, d), cv, lv)


def _mm_body(*refs, n_extra, n_out, epilogue, prologue):
    a_ref, w_ref = refs[0], refs[1]
    extra = refs[2:2 + n_extra]
    outs = refs[2 + n_extra:2 + n_extra + n_out]
    scratch = refs[2 + n_extra + n_out:]
    a = a_ref[...] if prologue is None else prologue(a_ref, extra, scratch)
    acc = jnp.dot(a, w_ref[...].astype(BF16), preferred_element_type=F32)
    epilogue(acc, extra, outs)


def _matmul(a, w, extras, extra_specs, out_shapes, out_specs, epilogue, *, tm, tn, name,
            prologue=None, scratch_shapes=()):
    m, k = a.shape
    n = w.shape[1]
    body = functools.partial(_mm_body, n_extra=len(extras), n_out=len(out_shapes),
                             epilogue=epilogue, prologue=prologue)
    return pl.pallas_call(
        body,
        grid=(m // tm, n // tn),
        in_specs=[pl.BlockSpec((tm, k), lambda i, j: (i, 0)),
                  pl.BlockSpec((k, tn), lambda i, j: (0, j))] + list(extra_specs),
        out_specs=list(out_specs),
        out_shape=list(out_shapes),
        scratch_shapes=list(scratch_shapes),
        compiler_params=_params(2),
        name=name,
    )(a, w, *extras)


def _plain_epilogue(out_dtype):
    def epilogue(acc, extra, outs):
        outs[0][...] = acc.astype(out_dtype)
    return epilogue


def _bias_epilogue(out_dtype):
    def epilogue(acc, extra, outs):
        outs[0][...] = (acc + extra[0][...]).astype(out_dtype)
    return epilogue


def _residual_matmul(a, w, bias, x_res, cv, lv, *, l_tot, lc, r_gate, tm, tn, name,
                     prologue=None, pro_extras=(), pro_specs=(), scratch_shapes=()):
    m = a.shape[0]
    n = w.shape[1]
    tpb = l_tot // tm
    n_pro = len(pro_extras)

    def epilogue(acc, extra, outs):
        b_ref, x_ref, cv_ref, lv_ref = extra[n_pro:n_pro + 4]
        is_ctx = _is_ctx_rows(pl.program_id(0), tm, tpb, lc)
        gate = _pick_mod(is_ctx, cv_ref, lv_ref, r_gate)
        outs[0][...] = x_ref[...] + gate * (acc + b_ref[...])

    extras = list(pro_extras) + [bias.reshape(1, n), x_res, cv, lv]
    specs = list(pro_specs) + [
        pl.BlockSpec((1, tn), lambda i, j: (0, j)),
        pl.BlockSpec((tm, tn), lambda i, j: (i, j)),
        pl.BlockSpec((MOD_ROWS, tn), lambda i, j: (0, j)),
        pl.BlockSpec((1, MOD_ROWS, tn), lambda i, j: (i // tpb, 0, j))]
    return _matmul(a, w, extras, specs,
                   [jax.ShapeDtypeStruct((m, n), F32)],
                   [pl.BlockSpec((tm, tn), lambda i, j: (i, j))],
                   epilogue, tm=tm, tn=tn, name=name, prologue=prologue,
                   scratch_shapes=scratch_shapes)[0]


def _ret_in_proj(h, w_in, cos, sin, *, l_tot, d_model):
    m = h.shape[0]
    n = w_in.shape[1]
    tm = _row_tile(l_tot, 2176)
    tn = 512
    tpb = l_tot // tm
    dk = d_model // RET_HEADS
    half = dk // 2
    n_q_tiles = d_model // tn
    k_scale = dk ** -0.5

    def epilogue(acc, extra, outs):
        cos_ref, sin_ref = extra
        j = pl.program_id(1)

        @pl.when(j < 2 * n_q_tiles)
        def _():
            c = cos_ref[...]
            s = sin_ref[...]
            mult = jnp.where(j >= n_q_tiles, k_scale, 1.0)
            for hh in range(tn // dk):
                x1 = acc[:, hh * dk:hh * dk + half]
                x2 = acc[:, hh * dk + half:(hh + 1) * dk]
                outs[0][:, hh * dk:hh * dk + half] = ((x1 * c - x2 * s) * mult).astype(BF16)
                outs[0][:, hh * dk + half:(hh + 1) * dk] = ((x2 * c + x1 * s) * mult).astype(BF16)

        @pl.when(j >= 2 * n_q_tiles)
        def _():
            outs[0][...] = acc.astype(BF16)

    tab_spec = pl.BlockSpec((tm, half), lambda i, j: (lax.rem(i, tpb), 0))
    return _matmul(h, w_in, [cos, sin], [tab_spec, tab_spec],
                   [jax.ShapeDtypeStruct((m, n), BF16)],
                   [pl.BlockSpec((tm, tn), lambda i, j: (i, j))],
                   epilogue, tm=tm, tn=tn, name="ret_in_proj")[0]


def _ret_body(lg_ref, q_ref, k_ref, v_ref, g_ref, gn_ref, o_ref, oacc_ref, sf_ref, sb_ref, *, nc_ctx, nc_tot):
    h = pl.program_id(1)
    ch = RET_CHUNK
    dv = v_ref.shape[-1]
    ii = lax.broadcasted_iota(jnp.int32, (ch, ch), 0).astype(F32)
    jj = lax.broadcasted_iota(jnp.int32, (ch, ch), 1).astype(F32)
    ri = lax.broadcasted_iota(jnp.int32, (ch, 1), 0).astype(F32)
    consts = []
    for direction in (0, 1):
        lg = lg_ref[direction, h]
        if direction == 0:
            diff = ii - jj
            q_decay = jnp.exp((ri + 1.0) * lg)
            k_decay = jnp.exp((ch - 1.0 - ri) * lg)
        else:
            diff = jj - ii
            q_decay = jnp.exp((ch - ri) * lg)
            k_decay = jnp.exp(ri * lg)
        intra = jnp.where(diff >= 0, jnp.exp(jnp.maximum(diff, 0.0) * lg), 0.0)
        chunk_decay = jnp.exp(jnp.full((1, dv), float(ch), F32) * lg)
        consts.append((intra, q_decay, k_decay, chunk_decay))
    sf_ref[...] = jnp.zeros_like(sf_ref)
    sb_ref[...] = jnp.zeros_like(sb_ref)
    oacc_ref[...] = jnp.zeros_like(oacc_ref)

    def chunk_rows(c):
        return pl.ds(pl.multiple_of(c * ch, ch), ch)

    def advance(c, s_ref, intra, q_decay, k_decay, chunk_decay):
        rows = chunk_rows(c)
        qc = q_ref[0, rows, :]
        kc = k_ref[0, rows, :]
        vc = v_ref[0, rows, :]
        scores = lax.dot_general(qc, kc, (((1,), (1,)), ((), ())), preferred_element_type=F32) * intra
        state = s_ref[...]
        out = (jnp.dot(scores.astype(BF16), vc, preferred_element_type=F32)
               + jnp.dot((qc.astype(F32) * q_decay).astype(BF16), state.astype(BF16),
                         preferred_element_type=F32))
        kt = (kc.astype(F32) * k_decay).T.astype(BF16)
        s_ref[...] = state * chunk_decay + jnp.dot(kt, vc, preferred_element_type=F32)
        return out

    def step(t, carry):
        c_fwd = t
        c_bwd = jnp.where(t < nc_ctx, nc_ctx - 1 - t, nc_tot - 1 - (t - nc_ctx))
        out_f = advance(c_fwd, sf_ref, *consts[0])
        out_b = advance(c_bwd, sb_ref, *consts[1])
        oacc_ref[chunk_rows(c_fwd), :] += out_f
        oacc_ref[chunk_rows(c_bwd), :] += out_b
        return carry

    lax.fori_loop(0, nc_tot, step, 0)

    def finish(c, carry):
        rows = chunk_rows(c)
        o = oacc_ref[rows, :]
        o = o * lax.rsqrt(jnp.mean(o * o, axis=-1, keepdims=True) + EPS) * gn_ref[...]
        g = g_ref[0, rows, :].astype(F32)
        o_ref[0, rows, :] = ((g * _sigmoid(g)) * o).astype(BF16)
        return carry

    lax.fori_loop(0, nc_tot, finish, 0)


def _retention(proj, log_gamma, gn, *, b, l_tot, lc, d_model):
    dk = d_model // RET_HEADS
    dv = 2 * d_model // RET_HEADS
    proj3 = proj.reshape(b, l_tot, proj.shape[-1])
    k_blk0 = d_model // dk
    v_blk0 = 2 * d_model // dv
    g_blk0 = 4 * d_model // dv
    body = functools.partial(_ret_body, nc_ctx=lc // RET_CHUNK, nc_tot=l_tot // RET_CHUNK)
    return pl.pallas_call(
        body,
        grid=(b, RET_HEADS),
        in_specs=[pl.BlockSpec(memory_space=pltpu.SMEM),
                  pl.BlockSpec((1, l_tot, dk), lambda bi, hi: (bi, 0, hi)),
                  pl.BlockSpec((1, l_tot, dk), lambda bi, hi: (bi, 0, k_blk0 + hi)),
                  pl.BlockSpec((1, l_tot, dv), lambda bi, hi: (bi, 0, v_blk0 + hi)),
                  pl.BlockSpec((1, l_tot, dv), lambda bi, hi: (bi, 0, g_blk0 + hi)),
                  pl.BlockSpec((1, dv), lambda bi, hi: (0, hi))],
        out_specs=pl.BlockSpec((1, l_tot, dv), lambda bi, hi: (bi, 0, hi)),
        out_shape=jax.ShapeDtypeStruct((b, l_tot, RET_HEADS * dv), BF16),
        scratch_shapes=[pltpu.VMEM((l_tot, dv), F32), pltpu.VMEM((dk, dv), F32), pltpu.VMEM((dk, dv), F32)],
        compiler_params=_params(2),
        name="retention",
    )(log_gamma, proj3, proj3, proj3, proj3, gn.reshape(1, RET_HEADS * dv))


def _retention_layer(x, h, cv, lv, w_in, decay, gn, w_out, cos, sin, *, b, l_tot, lc):
    m, d = x.shape
    proj = _ret_in_proj(h, w_in, cos, sin, l_tot=l_tot, d_model=d)
    log_gamma = jax.nn.log_sigmoid(decay.astype(F32))
    o = _retention(proj, log_gamma, gn, b=b, l_tot=l_tot, lc=lc, d_model=d).reshape(m, 2 * d)
    return _residual_matmul(o, w_out, jnp.zeros((d,), F32), x, cv, lv, l_tot=l_tot, lc=lc, r_gate=2,
                            tm=_row_tile(l_tot, 1088), tn=512, name="ret_out_proj")


def _mla_in_proj(h, w_in_pad, qa_gain, kva_gain, *, l_tot):
    m = h.shape[0]
    n = w_in_pad.shape[1]
    tm = _row_tile(l_tot, 1088)
    r = MLA_RANK

    def epilogue(acc, extra, outs):
        qg_ref, kg_ref = extra
        ql = acc[:, :r]
        kl = acc[:, r:2 * r]
        outs[0][...] = (ql * lax.rsqrt(jnp.mean(ql * ql, axis=-1, keepdims=True) + EPS)
                        * qg_ref[...]).astype(BF16)
        outs[1][...] = (kl * lax.rsqrt(jnp.mean(kl * kl, axis=-1, keepdims=True) + EPS)
                        * kg_ref[...]).astype(BF16)
        outs[2][...] = acc[:, 2 * r:]

    gspec = pl.BlockSpec((1, r), lambda i, j: (0, 0))
    return _matmul(h, w_in_pad, [qa_gain.reshape(1, r), kva_gain.reshape(1, r)], [gspec, gspec],
                   [jax.ShapeDtypeStruct((m, r), BF16), jax.ShapeDtypeStruct((m, r), BF16),
                    jax.ShapeDtypeStruct((m, LANES), F32)],
                   [pl.BlockSpec((tm, r), lambda i, j: (i, 0)),
                    pl.BlockSpec((tm, r), lambda i, j: (i, 0)),
                    pl.BlockSpec((tm, LANES), lambda i, j: (i, 0))],
                   epilogue, tm=tm, tn=n, name="mla_in_proj")


def _mla_q_proj(qn, wq_pad, q_gain_pad, tabs, *, l_tot):
    m = qn.shape[0]
    n = wq_pad.shape[1]
    tm = _row_tile(l_tot, 1088)
    tn = 1024
    tpb = l_tot // tm
    hw = 2 * LANES
    width = MLA_NOPE + MLA_ROPE
    scale = width ** -0.5 * LOG2_E

    def epilogue(acc, extra, outs):
        g_ref, c_ref, s1_ref, s2_ref = extra
        c, s1, s2 = c_ref[...], s1_ref[...], s2_ref[...]
        for hh in range(tn // hw):
            xs = acc[:, hh * hw:(hh + 1) * hw]
            inv = lax.rsqrt(jnp.sum(xs * xs, axis=-1, keepdims=True) * (1.0 / width) + EPS)
            y = xs * inv * g_ref[...] * scale
            outs[0][:, hh * hw:hh * hw + LANES] = y[:, :LANES].astype(BF16)
            outs[0][:, hh * hw + LANES:(hh + 1) * hw] = _rope3(y[:, LANES:], c, s1, s2).astype(BF16)

    tspec = pl.BlockSpec((tm, LANES), lambda i, j: (lax.rem(i, tpb), 0))
    return _matmul(qn, wq_pad, [q_gain_pad] + list(tabs),
                   [pl.BlockSpec((1, hw), lambda i, j: (0, 0)), tspec, tspec, tspec],
                   [jax.ShapeDtypeStruct((m, n), BF16)],
                   [pl.BlockSpec((tm, tn), lambda i, j: (i, j))],
                   epilogue, tm=tm, tn=tn, name="mla_q_proj")[0]


def _mla_kv_proj(kvn, w_kv, kr, k_gain_pad, tabs, *, l_tot):
    m = kvn.shape[0]
    n = w_kv.shape[1]
    tm = _row_tile(l_tot, 1088)
    tn = 1024
    tpb = l_tot // tm
    hw = MLA_NOPE + MLA_V
    width = MLA_NOPE + MLA_ROPE

    def epilogue(acc, extra, outs):
        kr_ref, g_ref, c_ref, s1_ref, s2_ref = extra
        kr_raw = kr_ref[...]
        ss_r = jnp.sum(kr_raw * kr_raw, axis=-1, keepdims=True)
        kr_rot = _rope3(kr_raw * g_ref[:, LANES:], c_ref[...], s1_ref[...], s2_ref[...])
        g_nope = g_ref[:, :LANES]
        for hh in range(tn // hw):
            kn = acc[:, hh * hw:hh * hw + MLA_NOPE]
            inv = lax.rsqrt((jnp.sum(kn * kn, axis=-1, keepdims=True) + ss_r) * (1.0 / width) + EPS)
            outs[0][:, hh * hw:hh * hw + LANES] = (kn * inv * g_nope).astype(BF16)
            outs[0][:, hh * hw + LANES:(hh + 1) * hw] = (kr_rot * inv).astype(BF16)
            outs[1][:, hh * MLA_V:(hh + 1) * MLA_V] = acc[:, hh * hw + MLA_NOPE:(hh + 1) * hw].astype(BF16)

    tspec = pl.BlockSpec((tm, LANES), lambda i, j: (lax.rem(i, tpb), 0))
    return _matmul(kvn, w_kv, [kr, k_gain_pad] + list(tabs),
                   [pl.BlockSpec((tm, LANES), lambda i, j: (i, 0)),
                    pl.BlockSpec((1, 2 * LANES), lambda i, j: (0, 0)), tspec, tspec, tspec],
                   [jax.ShapeDtypeStruct((m, n), BF16), jax.ShapeDtypeStruct((m, n // 2), BF16)],
                   [pl.BlockSpec((tm, tn), lambda i, j: (i, j)),
                    pl.BlockSpec((tm, tn // 2), lambda i, j: (i, j))],
                   epilogue, tm=tm, tn=tn, name="mla_kv_proj")


def _softmax_pv(q, k, v):
    s = lax.dot_general(q, k, (((1,), (1,)), ((), ())), preferred_element_type=F32)
    p = jnp.exp2(s - jnp.max(s, axis=-1, keepdims=True))
    den = jnp.sum(p, axis=-1, keepdims=True)
    return jnp.dot(p.astype(BF16), v, preferred_element_type=F32) / den


def _mla_attn_body(q_ref, k_ref, v_ref, o_ref, *, lc, tq, n_q):
    o_ref[0, 0:lc, :] = _softmax_pv(q_ref[0, 0:lc, :], k_ref[0, 0:lc, :], v_ref[0, 0:lc, :]).astype(BF16)

    def step(t, carry):
        rows = pl.ds(pl.multiple_of(lc + t * tq, LANES), tq)
        o_ref[0, rows, :] = _softmax_pv(q_ref[0, rows, :], k_ref[0], v_ref[0]).astype(BF16)
        return carry

    lax.fori_loop(0, n_q, step, 0, unroll=4)


def _mla_attention(q, k, v, *, b, l_tot, lc):
    hw = 2 * LANES
    tq = 256
    assert lc % LANES == 0 and (l_tot - lc) % tq == 0
    q3 = q.reshape(b, l_tot, MLA_HEADS * hw)
    k3 = k.reshape(b, l_tot, MLA_HEADS * hw)
    v3 = v.reshape(b, l_tot, MLA_HEADS * MLA_V)
    body = functools.partial(_mla_attn_body, lc=lc, tq=tq, n_q=(l_tot - lc) // tq)
    return pl.pallas_call(
        body,
        grid=(b, MLA_HEADS),
        in_specs=[pl.BlockSpec((1, l_tot, hw), lambda bi, hi: (bi, 0, hi)),
                  pl.BlockSpec((1, l_tot, hw), lambda bi, hi: (bi, 0, hi)),
                  pl.BlockSpec((1, l_tot, MLA_V), lambda bi, hi: (bi, 0, hi))],
        out_specs=pl.BlockSpec((1, l_tot, MLA_V), lambda bi, hi: (bi, 0, hi)),
        out_shape=jax.ShapeDtypeStruct((b, l_tot, MLA_HEADS * MLA_V), BF16),
        compiler_params=_params(2),
        name="mla_attention",
    )(q3, k3, v3)


def _mla_layer(x, h, cv, lv, w_in, qa_gain, kva_gain, w_q_b, w_kv_b, q_gain, k_gain, w_out, tabs,
               *, b, l_tot, lc):
    m, d = x.shape
    width = MLA_NOPE + MLA_ROPE
    hw = 2 * LANES
    w_in_pad = jnp.pad(w_in, ((0, 0), (0, 2 * MLA_RANK + LANES - w_in.shape[1])))
    wq_pad = jnp.pad(w_q_b.reshape(MLA_RANK, MLA_HEADS, width),
                     ((0, 0), (0, 0), (0, hw - width))).reshape(MLA_RANK, MLA_HEADS * hw)
    q_gain_pad = jnp.pad(q_gain, (0, hw - width)).reshape(1, hw)
    k_gain_pad = jnp.pad(k_gain, (0, hw - width)).reshape(1, hw)
    qn, kvn, kr = _mla_in_proj(h, w_in_pad, qa_gain, kva_gain, l_tot=l_tot)
    q = _mla_q_proj(qn, wq_pad, q_gain_pad, tabs, l_tot=l_tot)
    k, v = _mla_kv_proj(kvn, w_kv_b, kr, k_gain_pad, tabs, l_tot=l_tot)
    o = _mla_attention(q, k, v, b=b, l_tot=l_tot, lc=lc).reshape(m, MLA_HEADS * MLA_V)
    return _residual_matmul(o, w_out, jnp.zeros((d,), F32), x, cv, lv, l_tot=l_tot, lc=lc,
                            r_gate=2, tm=_row_tile(l_tot, 2176), tn=512, name="mla_out_proj")


def _conv_gate_body(bg_ref, cg_ref, u_ref, w_ref, cb_ref, o_ref, z_ref, *, l_tot, lc):
    pad = 8
    tn = o_ref.shape[-1]
    z_ref[0:pad, :] = jnp.zeros((pad, tn), F32)
    z_ref[pad + l_tot:, :] = jnp.zeros((pad, tn), F32)
    z_ref[pad:pad + l_tot, :] = cg_ref[0].astype(F32) * u_ref[0].astype(F32)
    rows = lax.broadcasted_iota(jnp.int32, (l_tot, 1), 0)
    z_prev = jnp.where(rows == lc, 0.0, z_ref[pad - 1:pad - 1 + l_tot, :])
    z_next = jnp.where(rows == lc - 1, 0.0, z_ref[pad + 1:pad + 1 + l_tot, :])
    z_mid = z_ref[pad:pad + l_tot, :]
    conv = z_prev * w_ref[0:1, :] + z_mid * w_ref[1:2, :] + z_next * w_ref[2:3, :] + cb_ref[...]
    o_ref[0] = (bg_ref[0].astype(F32) * conv).astype(BF16)


def _conv_gate(proj, conv_w, conv_b, *, b, l_tot, lc, d):
    tn = 256
    nb = d // tn
    proj3 = proj.reshape(b, l_tot, 3 * d)
    body = functools.partial(_conv_gate_body, l_tot=l_tot, lc=lc)
    return pl.pallas_call(
        body,
        grid=(b, nb),
        in_specs=[pl.BlockSpec((1, l_tot, tn), lambda bi, j: (bi, 0, j)),
                  pl.BlockSpec((1, l_tot, tn), lambda bi, j: (bi, 0, nb + j)),
                  pl.BlockSpec((1, l_tot, tn), lambda bi, j: (bi, 0, 2 * nb + j)),
                  pl.BlockSpec((3, tn), lambda bi, j: (0, j)),
                  pl.BlockSpec((1, tn), lambda bi, j: (0, j))],
        out_specs=pl.BlockSpec((1, l_tot, tn), lambda bi, j: (bi, 0, j)),
        out_shape=jax.ShapeDtypeStruct((b, l_tot, d), BF16),
        scratch_shapes=[pltpu.VMEM((l_tot + 16, tn), F32)],
        compiler_params=_params(2),
        name="conv_gate",
    )(proj3, proj3, proj3, conv_w, conv_b.reshape(1, d))


def _shortconv_layer(x, h, cv, lv, w_in, conv_w, conv_b, w_out, *, b, l_tot, lc):
    m, d = x.shape
    tm = _row_tile(l_tot, 2176)
    proj = _matmul(h, w_in, [], [], [jax.ShapeDtypeStruct((m, 3 * d), BF16)],
                   [pl.BlockSpec((tm, 512), lambda i, j: (i, j))],
                   _plain_epilogue(BF16), tm=tm, tn=512, name="sc_in_proj")[0]
    a = _conv_gate(proj, conv_w, conv_b, b=b, l_tot=l_tot, lc=lc, d=d).reshape(m, d)
    return _residual_matmul(a, w_out, jnp.zeros((d,), F32), x, cv, lv, l_tot=l_tot, lc=lc,
                            r_gate=2, tm=tm, tn=512, name="sc_out_proj")


def _split_hi_lo(x):
    hi = x.astype(BF16)
    lo = (x - hi.astype(F32)).astype(BF16)
    return hi, lo


def _head_rms_scale(x, seg_ref, segt_ref):
    hi, lo = _split_hi_lo(x * x)
    seg = seg_ref[...]
    ss = jnp.dot(hi, seg, preferred_element_type=F32) + jnp.dot(lo, seg, preferred_element_type=F32)
    inv = lax.rsqrt(ss * (1.0 / GQA_HEAD_DIM) + EPS)
    ihi, ilo = _split_hi_lo(inv)
    segt = segt_ref[...]
    return jnp.dot(ihi, segt, preferred_element_type=F32) + jnp.dot(ilo, segt, preferred_element_type=F32)


def _gqa_in_proj(h, w_in, b_in, qg_t, kg_t, seg, segt, tabs, *, l_tot):
    m = h.shape[0]
    n = w_in.shape[1]
    tm = _row_tile(l_tot, 1088)
    tn = 512
    tpb = l_tot // tm
    n_q_tiles = GQA_HEADS * GQA_HEAD_DIM // tn
    kv_w = GQA_KV_HEADS * GQA_HEAD_DIM
    assert n == n_q_tiles * tn + 2 * kv_w and 2 * kv_w == tn
    scale = GQA_HEAD_DIM ** -0.5 * LOG2_E

    def norm_rope(xs, g, extra, mult):
        seg_ref, segt_ref, c_ref, s1_ref, s2_ref = extra[3:8]
        y = xs * _head_rms_scale(xs, seg_ref, segt_ref) * g
        c, s1, s2 = c_ref[...], s1_ref[...], s2_ref[...]
        return [_rope3(y[:, a * LANES:(a + 1) * LANES], c, s1, s2) * mult
                for a in range(xs.shape[1] // LANES)]

    def dup_heads(slab, lo_mask):
        swapped = pltpu.roll(slab, GQA_HEAD_DIM, axis=1)
        return jnp.where(lo_mask, slab, swapped), jnp.where(lo_mask, swapped, slab)

    def epilogue(acc, extra, outs):
        b_ref, qg_ref, kg_ref = extra[0:3]
        j = pl.program_id(1)
        acc = acc + b_ref[...]

        @pl.when(j < n_q_tiles)
        def _():
            for a, slab in enumerate(norm_rope(acc, qg_ref[...], extra, scale)):
                outs[0][:, a * LANES:(a + 1) * LANES] = slab.astype(BF16)

        @pl.when(j == n_q_tiles)
        def _():
            lo_mask = lax.broadcasted_iota(jnp.int32, (tm, LANES), 1) < GQA_HEAD_DIM
            kpad = jnp.concatenate([acc[:, :kv_w], acc[:, :kv_w]], axis=1)
            k_slabs = norm_rope(kpad, kg_ref[...], extra, 1.0)[:kv_w // LANES]
            for a in range(kv_w // LANES):
                ka, kb = dup_heads(k_slabs[a], lo_mask)
                outs[1][:, (2 * a) * LANES:(2 * a + 1) * LANES] = ka.astype(BF16)
                outs[1][:, (2 * a + 1) * LANES:(2 * a + 2) * LANES] = kb.astype(BF16)
                va, vb = dup_heads(acc[:, kv_w + a * LANES:kv_w + (a + 1) * LANES], lo_mask)
                outs[2][:, (2 * a) * LANES:(2 * a + 1) * LANES] = va.astype(BF16)
                outs[2][:, (2 * a + 1) * LANES:(2 * a + 2) * LANES] = vb.astype(BF16)

    tspec = pl.BlockSpec((tm, LANES), lambda i, j: (lax.rem(i, tpb), 0))
    full = lambda shape: pl.BlockSpec(shape, lambda i, j: (0, 0))
    return _matmul(
        h, w_in, [b_in.reshape(1, n), qg_t, kg_t, seg, segt] + list(tabs),
        [pl.BlockSpec((1, tn), lambda i, j: (0, j)), full((1, tn)), full((1, tn)),
         full((tn, LANES)), full((LANES, tn)), tspec, tspec, tspec],
        [jax.ShapeDtypeStruct((m, n_q_tiles * tn), BF16),
         jax.ShapeDtypeStruct((m, GQA_KV_HEADS * LANES), BF16),
         jax.ShapeDtypeStruct((m, GQA_KV_HEADS * LANES), BF16)],
        [pl.BlockSpec((tm, tn), lambda i, j: (i, jnp.minimum(j, n_q_tiles - 1))),
         pl.BlockSpec((tm, GQA_KV_HEADS * LANES), lambda i, j: (i, 0)),
         pl.BlockSpec((tm, GQA_KV_HEADS * LANES), lambda i, j: (i, 0))],
        epilogue, tm=tm, tn=tn, name="gqa_in_proj")


def _gqa_attn_body(sink_ref, q_ref, k_ref, v_ref, o_ref, *, lc, nb):
    g = pl.program_id(1)
    r = WINDOW
    n_slab = GQA_GROUP // 2
    rows_all = GQA_GROUP * r
    row_head = lax.broadcasted_iota(jnp.int32, (rows_all, 1), 0) // r
    sink = jnp.zeros((rows_all, 1), F32)
    for hh in range(GQA_GROUP):
        sink = jnp.where(row_head == hh, sink_ref[g * GQA_GROUP + hh] * LOG2_E, sink)
    lo_mask = lax.broadcasted_iota(jnp.int32, (r, LANES), 1) < GQA_HEAD_DIM
    ii = lax.rem(lax.broadcasted_iota(jnp.int32, (rows_all, r), 0), r)
    jj = lax.broadcasted_iota(jnp.int32, (rows_all, r), 1)
    k_ctx = k_ref[0, 0:lc, :]
    v_ctx = v_ref[0, 0:lc, :]

    def stack_q(rows):
        parts = []
        for a in range(n_slab):
            qa = q_ref[0, rows, a * LANES:(a + 1) * LANES]
            zero = jnp.zeros_like(qa)
            parts += [jnp.where(lo_mask, qa, zero), jnp.where(lo_mask, zero, qa)]
        return jnp.concatenate(parts, axis=0)

    def attend(rows, blocks):
        qs = stack_q(rows)
        scores = []
        for kb, _, mask in blocks:
            s = lax.dot_general(qs, kb, (((1,), (1,)), ((), ())), preferred_element_type=F32)
            scores.append(s if mask is None else jnp.where(mask, s, NEG_INF))
        slabs = lambda a: [a[:, c * LANES:(c + 1) * LANES] for c in range(a.shape[1] // LANES)]
        mx = functools.reduce(jnp.maximum, [sl for s in scores for sl in slabs(s)])
        mx = jnp.maximum(sink, jnp.max(mx, axis=-1, keepdims=True))
        psum = jnp.zeros((rows_all, LANES), F32)
        acc = jnp.zeros((rows_all, LANES), F32)
        for s, (_, vb, _) in zip(scores, blocks):
            p = jnp.exp2(s - mx)
            psum = functools.reduce(jnp.add, slabs(p), psum)
            acc = acc + jnp.dot(p.astype(BF16), vb, preferred_element_type=F32)
        o = acc / (jnp.exp2(sink - mx) + jnp.sum(psum, axis=-1, keepdims=True))
        for a in range(n_slab):
            oa = jnp.where(lo_mask, o[(2 * a) * r:(2 * a + 1) * r], o[(2 * a + 1) * r:(2 * a + 2) * r])
            o_ref[0, rows, a * LANES:(a + 1) * LANES] = oa.astype(BF16)

    for cb in range(lc // r):
        attend(pl.ds(cb * r, r), [(k_ctx, v_ctx, None)])

    def step(t, carry):
        r_cur = pl.multiple_of(lc + t * r, r)
        r_prev = pl.multiple_of(lc + jnp.maximum(t - 1, 0) * r, r)
        r_next = pl.multiple_of(lc + jnp.minimum(t + 1, nb - 1) * r, r)
        blk = lambda r0: (k_ref[0, pl.ds(r0, r), :], v_ref[0, pl.ds(r0, r), :])
        m_prev = jnp.logical_and(jj >= ii, t > 0)
        m_next = jnp.logical_and(jj <= ii, t < nb - 1)
        attend(pl.ds(r_cur, r), [blk(r_prev) + (m_prev,), blk(r_cur) + (None,),
                                 blk(r_next) + (m_next,), (k_ctx, v_ctx, None)])
        return carry

    lax.fori_loop(0, nb, step, 0, unroll=2)


def _gqa_attention(q, k, v, sink, *, b, l_tot, lc):
    gw = GQA_GROUP * GQA_HEAD_DIM
    q3 = q.reshape(b, l_tot, GQA_HEADS * GQA_HEAD_DIM)
    k3 = k.reshape(b, l_tot, GQA_KV_HEADS * LANES)
    v3 = v.reshape(b, l_tot, GQA_KV_HEADS * LANES)
    body = functools.partial(_gqa_attn_body, lc=lc, nb=(l_tot - lc) // WINDOW)
    return pl.pallas_call(
        body,
        grid=(b, GQA_KV_HEADS),
        in_specs=[pl.BlockSpec(memory_space=pltpu.SMEM),
                  pl.BlockSpec((1, l_tot, gw), lambda bi, gi: (bi, 0, gi)),
                  pl.BlockSpec((1, l_tot, LANES), lambda bi, gi: (bi, 0, gi)),
                  pl.BlockSpec((1, l_tot, LANES), lambda bi, gi: (bi, 0, gi))],
        out_specs=pl.BlockSpec((1, l_tot, gw), lambda bi, gi: (bi, 0, gi)),
        out_shape=jax.ShapeDtypeStruct((b, l_tot, GQA_HEADS * GQA_HEAD_DIM), BF16),
        compiler_params=_params(2),
        name="gqa_attention",
    )(sink, q3, k3, v3)


def _gqa_layer(x, h, cv, lv, w_in, b_in, q_gain, k_gain, sink, w_out, b_out, tabs, *, b, l_tot, lc):
    m, d = x.shape
    tn = 512
    heads_per_tile = tn // GQA_HEAD_DIM
    col = jnp.arange(tn)
    seg = (col[:, None] // GQA_HEAD_DIM == jnp.arange(LANES)[None, :]).astype(BF16)
    qg_t = jnp.tile(q_gain, heads_per_tile).reshape(1, tn)
    kg_t = jnp.tile(k_gain, heads_per_tile).reshape(1, tn)
    q, k, v = _gqa_in_proj(h, w_in, b_in, qg_t, kg_t, seg, seg.T, tabs, l_tot=l_tot)
    o = _gqa_attention(q, k, v, sink.astype(F32), b=b, l_tot=l_tot, lc=lc).reshape(m, d)
    return _residual_matmul(o, w_out, b_out, x, cv, lv, l_tot=l_tot, lc=lc, r_gate=2,
                            tm=_row_tile(l_tot, 2176), tn=512, name="gqa_out_proj")


def _pack_halves(yb):
    w = yb.shape[1] // 2
    lo = lax.bitcast_convert_type(yb[:, :w].astype(F32), jnp.uint32)
    hi = lax.bitcast_convert_type(yb[:, w:].astype(F32), jnp.uint32)
    return lax.shift_right_logical(lo, jnp.uint32(16)) | hi


def _unpack_halves_f32(words):
    lo = lax.bitcast_convert_type(lax.shift_left(words, jnp.uint32(16)), F32)
    hi = lax.bitcast_convert_type(words & jnp.uint32(0xFFFF0000), F32)
    return lo, hi


def _unpack_halves(words):
    lo, hi = _unpack_halves_f32(words)
    return lo.astype(BF16), hi.astype(BF16)


def _ffn_route_body(x_ref, g_ref, cv_ref, lv_ref, rw_ref, rb_ref, fpk_ref, ri_ref, rg_ref, cnt_ref,
                    carry_ref, *, tm, tpb, lc, route_ctx, n_e):
    i = pl.program_id(0)

    @pl.when(i == 0)
    def _():
        carry_ref[...] = jnp.zeros_like(carry_ref)

    x = x_ref[...]
    y = x * lax.rsqrt(jnp.mean(x * x, axis=-1, keepdims=True) + EPS) * g_ref[...]
    is_ctx = _is_ctx_rows(i, tm, tpb, lc)
    y = y * (1.0 + _pick_mod(is_ctx, cv_ref, lv_ref, 4)) + _pick_mod(is_ctx, cv_ref, lv_ref, 3)
    yb = y.astype(BF16)
    fpk_ref[...] = _pack_halves(yb)

    lane = lax.broadcasted_iota(jnp.int32, (tm, LANES), 1)
    logits = jnp.dot(yb, rw_ref[...].astype(BF16), preferred_element_type=F32) + rb_ref[...]
    logits = jnp.where(lane < n_e, logits, NEG_INF)
    vals, idxs, hots = [], [], []
    for _ in range(TOP_K):
        mk = jnp.max(logits, axis=-1, keepdims=True)
        ik = jnp.min(jnp.where(logits == mk, lane, LANES), axis=-1, keepdims=True)
        hot = lane == ik
        vals.append(mk)
        idxs.append(ik)
        hots.append(hot)
        logits = jnp.where(hot, NEG_INF, logits)
    exps = [jnp.exp(v - vals[0]) for v in vals]
    den = exps[0] + exps[1] + exps[2] + exps[3]

    routed = jnp.ones_like(is_ctx) if route_ctx else jnp.logical_not(is_ctx)
    chosen = jnp.zeros((tm, LANES), F32)
    for hot in hots:
        chosen = chosen + jnp.where(jnp.logical_and(hot, routed), 1.0, 0.0)
    before = (lax.broadcasted_iota(jnp.int32, (tm, tm), 0)
              > lax.broadcasted_iota(jnp.int32, (tm, tm), 1)).astype(BF16)
    arrived = jnp.dot(before, chosen.astype(BF16), preferred_element_type=F32) + carry_ref[...]
    carry_ref[...] = carry_ref[...] + jnp.sum(chosen, axis=0, keepdims=True)
    cnt_ref[...] = carry_ref[...]

    ri = jnp.zeros((tm, LANES), jnp.int32)
    rg = jnp.zeros((tm, LANES), F32)
    for k in range(TOP_K):
        rank = jnp.sum(jnp.where(hots[k], arrived, 0.0), axis=-1, keepdims=True)
        ri = jnp.where(lane == k, idxs[k], ri)
        ri = jnp.where(lane == TOP_K + k, rank.astype(jnp.int32), ri)
        rg = jnp.where(lane == k, exps[k] / den, rg)
    ri_ref[...] = ri
    rg_ref[...] = rg


def _ffn_route(x, gain, cv, lv, router_w_pad, router_b_pad, layer, *, l_tot, lc, route_ctx, n_e):
    m, d = x.shape
    tm = _row_tile(l_tot, 544)
    tpb = l_tot // tm
    body = functools.partial(_ffn_route_body, tm=tm, tpb=tpb, lc=lc, route_ctx=route_ctx, n_e=n_e)
    return pl.pallas_call(
        body,
        grid=(m // tm,),
        in_specs=[pl.BlockSpec((tm, d), lambda i: (i, 0)),
                  pl.BlockSpec((1, d), lambda i: (0, 0)),
                  pl.BlockSpec((MOD_ROWS, d), lambda i: (0, 0)),
                  pl.BlockSpec((1, MOD_ROWS, d), lambda i: (i // tpb, 0, 0)),
                  pl.BlockSpec((None, d, LANES), lambda i: (layer, 0, 0)),
                  pl.BlockSpec((None, 1, LANES), lambda i: (layer, 0, 0))],
        out_specs=[pl.BlockSpec((tm, d // 2), lambda i: (i, 0)),
                   pl.BlockSpec((tm, LANES), lambda i: (i, 0)),
                   pl.BlockSpec((tm, LANES), lambda i: (i, 0)),
                   pl.BlockSpec((1, LANES), lambda i: (0, 0))],
        out_shape=[jax.ShapeDtypeStruct((m, d // 2), jnp.uint32),
                   jax.ShapeDtypeStruct((m, LANES), jnp.int32),
                   jax.ShapeDtypeStruct((m, LANES), F32),
                   jax.ShapeDtypeStruct((1, LANES), F32)],
        scratch_shapes=[pltpu.VMEM((1, LANES), F32)],
        compiler_params=_params(1),
        name="ffn_route",
    )(x, gain.reshape(1, d), cv, lv, router_w_pad, router_b_pad)


def _dispatch_body(dest_ref, f_ref, init_hbm, xr_hbm, sem, *, tm, tpb, ctx_tiles, route_ctx):
    del init_hbm
    i = pl.program_id(0)

    def scatter_rows():
        def start(r, carry):
            a0 = (i * tm + r) * TOP_K
            for k in range(TOP_K):
                pltpu.make_async_copy(f_ref.at[pl.ds(r, 1)], xr_hbm.at[pl.ds(dest_ref[a0 + k], 1)], sem).start()
            return carry

        lax.fori_loop(0, tm, start, 0, unroll=ISSUE_UNROLL)

        def wait(r, carry):
            for k in range(TOP_K):
                pltpu.make_async_copy(f_ref.at[pl.ds(0, 1)], xr_hbm.at[pl.ds(0, 1)], sem).wait()
            return carry

        lax.fori_loop(0, tm, wait, 0, unroll=ISSUE_UNROLL)

    if route_ctx:
        scatter_rows()
    else:
        pl.when(lax.rem(i, tpb) >= ctx_tiles)(scatter_rows)


def _dispatch(fpk, dest, n_rows, *, l_tot, lc, route_ctx):
    m, w = fpk.shape
    tm = 256
    assert lc % tm == 0 and l_tot % tm == 0
    body = functools.partial(_dispatch_body, tm=tm, tpb=l_tot // tm, ctx_tiles=lc // tm, route_ctx=route_ctx)
    return pl.pallas_call(
        body,
        grid_spec=pltpu.PrefetchScalarGridSpec(
            num_scalar_prefetch=1,
            grid=(m // tm,),
            in_specs=[pl.BlockSpec((tm, w), lambda i, dst: (i, 0)),
                      pl.BlockSpec(memory_space=pl.ANY)],
            out_specs=pl.BlockSpec(memory_space=pl.ANY),
            scratch_shapes=[pltpu.SemaphoreType.DMA(())]),
        out_shape=jax.ShapeDtypeStruct((n_rows, w), jnp.uint32),
        input_output_aliases={2: 0},
        compiler_params=_row_copy_params(),
        name="moe_dispatch",
    )(dest, fpk, jnp.zeros((n_rows, w), jnp.uint32))


def _moe_gu_body(be_ref, nu_ref, x_ref, wgu_ref, bgu_ref, o_ref, wgb_ref, wlb_ref):
    blk = pl.program_id(0)
    ff = wgb_ref.shape[1]
    prev = be_ref[jnp.maximum(blk - 1, 0)]

    @pl.when(jnp.logical_or(blk == 0, be_ref[blk] != prev))
    def _():
        wgb_ref[...] = wgu_ref[:, :ff].astype(BF16)
        wlb_ref[...] = wgu_ref[:, ff:].astype(BF16)

    @pl.when(blk < nu_ref[0])
    def _():
        x_lo, x_hi = _unpack_halves(x_ref[...])
        w = x_lo.shape[1]
        h_glu = (jnp.dot(x_lo, wgb_ref[0:w, :], preferred_element_type=F32)
                 + jnp.dot(x_hi, wgb_ref[w:, :], preferred_element_type=F32) + bgu_ref[:, :ff])
        h_lin = (jnp.dot(x_lo, wlb_ref[0:w, :], preferred_element_type=F32)
                 + jnp.dot(x_hi, wlb_ref[w:, :], preferred_element_type=F32) + bgu_ref[:, ff:])
        h_glu = jnp.minimum(h_glu, SWIGLU_LIMIT)
        h_lin = jnp.clip(h_lin, -SWIGLU_LIMIT, SWIGLU_LIMIT)
        o_ref[...] = (h_glu * _sigmoid(SWIGLU_ALPHA * h_glu) * (h_lin + 1.0)).astype(BF16)

    @pl.when(blk >= nu_ref[0])
    def _():
        o_ref[...] = jnp.zeros_like(o_ref)


def _moe_down_body(be_ref, nu_ref, a_ref, w_ref, b_ref, o_ref, wb_ref):
    blk = pl.program_id(0)
    prev = be_ref[jnp.maximum(blk - 1, 0)]

    @pl.when(jnp.logical_or(blk == 0, be_ref[blk] != prev))
    def _():
        wb_ref[...] = w_ref[...].astype(BF16)

    @pl.when(blk < nu_ref[0])
    def _():
        y = jnp.dot(a_ref[...], wb_ref[...], preferred_element_type=F32) + b_ref[...]
        o_ref[...] = _pack_halves(y.astype(BF16))

    @pl.when(blk >= nu_ref[0])
    def _():
        o_ref[...] = jnp.zeros_like(o_ref)


def _moe_experts(x_rows, block_expert, n_used, w_gu, b_gu, w_down, b_down, layer):
    n_rows, w = x_rows.shape
    n_blocks = n_rows // MOE_TM
    depth, n_e, ff, d = w_down.shape
    act = pl.pallas_call(
        _moe_gu_body,
        grid_spec=pltpu.PrefetchScalarGridSpec(
            num_scalar_prefetch=2,
            grid=(n_blocks,),
            in_specs=[pl.BlockSpec((MOE_TM, w), lambda i, be, nu: (i, 0)),
                      pl.BlockSpec((None, None, d, 2 * ff), lambda i, be, nu: (layer, be[i], 0, 0)),
                      pl.BlockSpec((None, None, 1, 2 * ff), lambda i, be, nu: (layer, be[i], 0, 0))],
            out_specs=pl.BlockSpec((MOE_TM, ff), lambda i, be, nu: (i, 0)),
            scratch_shapes=[pltpu.VMEM((d, ff), BF16), pltpu.VMEM((d, ff), BF16)]),
        out_shape=jax.ShapeDtypeStruct((n_rows, ff), BF16),
        compiler_params=_params(1),
        name="moe_gate_up",
    )(block_expert, n_used, x_rows, w_gu, b_gu.reshape(depth, n_e, 1, 2 * ff))
    return pl.pallas_call(
        _moe_down_body,
        grid_spec=pltpu.PrefetchScalarGridSpec(
            num_scalar_prefetch=2,
            grid=(n_blocks,),
            in_specs=[pl.BlockSpec((MOE_TM, ff), lambda i, be, nu: (i, 0)),
                      pl.BlockSpec((None, None, ff, d), lambda i, be, nu: (layer, be[i], 0, 0)),
                      pl.BlockSpec((None, None, 1, d), lambda i, be, nu: (layer, be[i], 0, 0))],
            out_specs=pl.BlockSpec((MOE_TM, d // 2), lambda i, be, nu: (i, 0)),
            scratch_shapes=[pltpu.VMEM((ff, d), BF16)]),
        out_shape=jax.ShapeDtypeStruct((n_rows, d // 2), jnp.uint32),
        compiler_params=_params(1),
        name="moe_down",
    )(block_expert, n_used, act, w_down, b_down.reshape(depth, n_e, 1, d))


def _combine_body(dest_ref, x_ref, rg_ref, cv_ref, lv_ref, y_hbm, o_ref, buf, sem,
                  *, tt, n_tiles, tpb, ctx_tiles, lc, route_ctx):
    i = pl.program_id(0)

    def routed(tile):
        return (tile >= 0) if route_ctx else (lax.rem(tile, tpb) >= ctx_tiles)

    def row_copy(src, slot, k, r):
        return pltpu.make_async_copy(y_hbm.at[pl.ds(src, 1)], buf.at[slot, k, pl.ds(r, 1)], sem.at[slot])

    def issue(tile, slot):
        def start(r, carry):
            a0 = (tile * tt + r) * TOP_K
            for k in range(TOP_K):
                row_copy(dest_ref[a0 + k], slot, k, r).start()
            return carry

        lax.fori_loop(0, tt, start, 0, unroll=ISSUE_UNROLL)

    @pl.when(jnp.logical_and(i == 0, routed(i)))
    def _():
        issue(i, 0)

    nxt = jnp.minimum(i + 1, n_tiles - 1)

    @pl.when(jnp.logical_and(i + 1 < n_tiles, routed(nxt)))
    def _():
        issue(nxt, lax.rem(i + 1, 2))

    slot = lax.rem(i, 2)

    @pl.when(routed(i))
    def _():
        def wait(r, carry):
            for k in range(TOP_K):
                row_copy(0, slot, k, 0).wait()
            return carry

        lax.fori_loop(0, tt, wait, 0, unroll=ISSUE_UNROLL)
        rg = rg_ref[...]
        w = buf.shape[-1]
        acc_lo = jnp.zeros((tt, w), F32)
        acc_hi = jnp.zeros((tt, w), F32)
        for k in range(TOP_K):
            y_lo, y_hi = _unpack_halves_f32(buf[slot, k])
            acc_lo = acc_lo + rg[:, k:k + 1] * y_lo
            acc_hi = acc_hi + rg[:, k:k + 1] * y_hi
        gate = _pick_mod(_is_ctx_rows(i, tt, tpb, lc), cv_ref, lv_ref, 5)
        o_ref[:, :w] = x_ref[:, :w] + gate[:, :w] * acc_lo
        o_ref[:, w:] = x_ref[:, w:] + gate[:, w:] * acc_hi

    @pl.when(jnp.logical_not(routed(i)))
    def _():
        o_ref[...] = x_ref[...]


def _combine(x, y_rows, rg, dest, cv, lv, *, l_tot, lc, route_ctx):
    m, d = x.shape
    tt = 128
    assert lc % tt == 0 and l_tot % tt == 0
    tpb = l_tot // tt
    n_tiles = m // tt
    body = functools.partial(_combine_body, tt=tt, n_tiles=n_tiles, tpb=tpb, ctx_tiles=lc // tt, lc=lc,
                             route_ctx=route_ctx)
    return pl.pallas_call(
        body,
        grid_spec=pltpu.PrefetchScalarGridSpec(
            num_scalar_prefetch=1,
            grid=(n_tiles,),
            in_specs=[pl.BlockSpec((tt, d), lambda i, dst: (i, 0)),
                      pl.BlockSpec((tt, LANES), lambda i, dst: (i, 0)),
                      pl.BlockSpec((MOD_ROWS, d), lambda i, dst: (0, 0)),
                      pl.BlockSpec((1, MOD_ROWS, d), lambda i, dst: (i // tpb, 0, 0)),
                      pl.BlockSpec(memory_space=pl.ANY)],
            out_specs=pl.BlockSpec((tt, d), lambda i, dst: (i, 0)),
            scratch_shapes=[pltpu.VMEM((2, TOP_K, tt, d // 2), jnp.uint32), pltpu.SemaphoreType.DMA((2,))]),
        out_shape=jax.ShapeDtypeStruct((m, d), F32),
        compiler_params=_row_copy_params(),
        name="moe_combine",
    )(dest, x, rg, cv, lv, y_rows)


def _moe_layer(x, gain, cv, lv, router_w_pad, router_b_pad, w_gu, b_gu, w_down, b_down, layer,
               *, l_tot, lc, route_ctx):
    m, d = x.shape
    n_e = w_gu.shape[1]
    fpk, ri, rg, cnt = _ffn_route(x, gain, cv, lv, router_w_pad, router_b_pad, layer,
                                  l_tot=l_tot, lc=lc, route_ctx=route_ctx, n_e=n_e)
    counts = cnt[0, :n_e].astype(jnp.int32)
    padded = (counts + MOE_TM - 1) // MOE_TM * MOE_TM
    pend = jnp.cumsum(padded)
    pstart = (pend - padded).astype(jnp.int32)
    n_blocks = -(-m * TOP_K // MOE_TM) + n_e
    n_rows = n_blocks * MOE_TM
    block_start = jnp.arange(n_blocks, dtype=jnp.int32) * MOE_TM
    block_expert = jnp.minimum(jnp.sum(block_start[:, None] >= pend[None, :], axis=1), n_e - 1).astype(jnp.int32)
    n_used = (pend[-1] // MOE_TM).astype(jnp.int32).reshape(1)
    dest = (pstart[ri[:, :TOP_K]] + ri[:, TOP_K:2 * TOP_K]).reshape(m * TOP_K)
    x_rows = _dispatch(fpk, dest, n_rows, l_tot=l_tot, lc=lc, route_ctx=route_ctx)
    y_rows = _moe_experts(x_rows, block_expert, n_used, w_gu, b_gu, w_down, b_down, layer)
    return _combine(x, y_rows, rg, dest, cv, lv, l_tot=l_tot, lc=lc, route_ctx=route_ctx)


def _rope_1d_tables(l_tot, half):
    inv = ROPE_BASE ** (-jnp.arange(half, dtype=F32) / half)
    ang = jnp.arange(l_tot, dtype=F32)[:, None] * inv
    return jnp.cos(ang), jnp.sin(ang)


def _axial_tables(l_tot, lc):
    n_freq = GQA_HEAD_DIM // 4
    t = jnp.maximum(jnp.arange(l_tot) - lc, 0)
    pos = jnp.stack([(t // GRID_W).astype(F32), (t % GRID_W).astype(F32)], axis=-1)
    inv_freq = ROPE_BASE ** (-jnp.arange(n_freq, dtype=F32) / n_freq)
    ang = pos[:, :, None] * inv_freq
    lane = jnp.arange(LANES)
    axis = (lane % GQA_HEAD_DIM) // (2 * n_freq)
    freq = lane % n_freq
    first_half = (lane % (2 * n_freq)) < n_freq
    is_lat = (jnp.arange(l_tot) >= lc)[:, None]
    cos = jnp.where(is_lat, jnp.cos(ang)[:, axis, freq], 1.0)
    sin = jnp.where(is_lat, jnp.sin(ang)[:, axis, freq], 0.0)
    return cos, jnp.where(first_half, -sin, 0.0), jnp.where(first_half, 0.0, sin)


def kernel(x, c, ctx, c_ctx, norm_mix, norm_ffn, mod_w, mod_b, ret_w_in, ret_decay, ret_gn, ret_w_out, mla_w_in, mla_q_a_norm, mla_kv_a_norm, mla_w_q_b, mla_w_kv_b, mla_q_norm, mla_k_norm, mla_w_out, sc_w_in, sc_conv_w, sc_conv_b, sc_w_out, gqa_w_in, gqa_b_in, gqa_q_norm, gqa_k_norm, gqa_sink, gqa_w_out, gqa_b_out, moe_router_w, moe_router_b, moe_w_gu, moe_b_gu, moe_w_down, moe_b_down):
    b, n_lat, d = x.shape
    lc = ctx.shape[1]
    l_tot = lc + n_lat
    m = b * l_tot
    depth = mod_w.shape[0]
    assert b + 1 <= MOD_ROWS and lc % 256 == 0 and n_lat % 256 == 0

    xs = jnp.concatenate([ctx, x], axis=1).reshape(m, d)
    cvec = jnp.zeros((MOD_ROWS, d), F32).at[:b].set(c).at[b].set(c_ctx)
    mod = _modulation(cvec, mod_w, mod_b).reshape(depth, MOD_ROWS, 6, d)
    pad_rows = ((0, 0), (0, MOD_ROWS - 6), (0, 0))
    ret_cos, ret_sin = _rope_1d_tables(l_tot, d // RET_HEADS // 2)
    tabs = _axial_tables(l_tot, lc)
    n_e = moe_router_w.shape[-1]
    router_w_pad = jnp.pad(moe_router_w, ((0, 0), (0, 0), (0, LANES - n_e)))
    router_b_pad = jnp.pad(moe_router_b, ((0, 0), (0, LANES - n_e))).reshape(depth, 1, LANES)

    for i in range(depth):
        kind, j = i % N_MIXERS, i // N_MIXERS
        cv = jnp.pad(mod[i, b], pad_rows[1:])
        lv = jnp.pad(mod[i, :b], pad_rows)
        lay = dict(b=b, l_tot=l_tot, lc=lc)
        h = _norm_mod(xs, norm_mix[i], cv, lv, l_tot=l_tot, lc=lc, r_shift=0, r_scale=1)
        if kind == 0:
            xs = _retention_layer(xs, h, cv, lv, ret_w_in[j], ret_decay[j], ret_gn[j], ret_w_out[j],
                                  ret_cos, ret_sin, **lay)
        elif kind == 1:
            xs = _mla_layer(xs, h, cv, lv, mla_w_in[j], mla_q_a_norm[j], mla_kv_a_norm[j], mla_w_q_b[j],
                            mla_w_kv_b[j], mla_q_norm[j], mla_k_norm[j], mla_w_out[j], tabs, **lay)
        elif kind == 2:
            xs = _shortconv_layer(xs, h, cv, lv, sc_w_in[j], sc_conv_w[j], sc_conv_b[j], sc_w_out[j], **lay)
        else:
            xs = _gqa_layer(xs, h, cv, lv, gqa_w_in[j], gqa_b_in[j], gqa_q_norm[j], gqa_k_norm[j],
                            gqa_sink[j], gqa_w_out[j], gqa_b_out[j], tabs, **lay)
        xs = _moe_layer(xs, norm_ffn[i], cv, lv, router_w_pad, router_b_pad, moe_w_gu, moe_b_gu,
                        moe_w_down, moe_b_down, i, l_tot=l_tot, lc=lc, route_ctx=i < depth - 1)
    return xs.reshape(b, l_tot, d)[:, lc:, :]
```

```python
import functools

import jax
import jax.numpy as jnp
from jax import lax
from jax.experimental import pallas as pl
from jax.experimental.pallas import tpu as pltpu

F32 = jnp.float32
BF16 = jnp.bfloat16

EPS = 1e-6
ROPE_BASE = 10000.0
NEG_INF = -1e30
LOG2_E = 1.4426950408889634
GRID_W = 64

RET_HEADS = 8
RET_CHUNK = 256
MLA_HEADS = 16
MLA_NOPE = 128
MLA_ROPE = 64
MLA_V = 128
MLA_RANK = 512
GQA_HEADS = 32
GQA_KV_HEADS = 4
GQA_GROUP = GQA_HEADS // GQA_KV_HEADS
GQA_HEAD_DIM = 64
WINDOW = 128
N_EXPERTS = 32
TOP_K = 4
SWIGLU_LIMIT = 7.0
SWIGLU_ALPHA = 1.702
N_MIXERS = 4

LANES = 128
VMEM_LIMIT = 56 * 1024 * 1024
MOE_TM = 256
MOD_ROWS = 8
ISSUE_UNROLL = 4
ROW_TILE = 8


def _params(n_axes, vmem=VMEM_LIMIT):
    return pltpu.CompilerParams(dimension_semantics=("arbitrary",) * n_axes,
                                vmem_limit_bytes=vmem)


def _row_copy_params():
    return pltpu.CompilerParams(dimension_semantics=("arbitrary",), vmem_limit_bytes=VMEM_LIMIT,
                                disable_bounds_checks=True)


def _row_tile(l_tot, cap):
    best = 16
    for t in range(16, cap + 1, 16):
        if l_tot % t == 0:
            best = t
    return best


def _is_ctx_rows(i, tm, tpb, lc):
    rows = lax.rem(i, tpb) * tm + lax.broadcasted_iota(jnp.int32, (tm, 1), 0)
    return rows < lc


def _pick_mod(is_ctx, cv_ref, lv_ref, r):
    return jnp.where(is_ctx, cv_ref[r:r + 1, :], lv_ref[0, r:r + 1, :])


def _sigmoid(x):
    return 1.0 / (1.0 + jnp.exp(-x))


def _rope3(x, c, s1, s2):
    return x * c + pltpu.roll(x, LANES - 16, axis=1) * s1 + pltpu.roll(x, 16, axis=1) * s2


def _mod_body(c_ref, w_ref, b_ref, o_ref):
    c = c_ref[...]
    s = (c * _sigmoid(c)).astype(BF16)
    o_ref[0] = jnp.dot(s, w_ref[0].astype(BF16), preferred_element_type=F32) + b_ref[0]


def _modulation(cvec, mod_w, mod_b):
    depth, d, n = mod_w.shape
    tn = 2048
    return pl.pallas_call(
        _mod_body,
        grid=(depth, n // tn),
        in_specs=[pl.BlockSpec((MOD_ROWS, d), lambda l, j: (0, 0)),
                  pl.BlockSpec((1, d, tn), lambda l, j: (l, 0, j)),
                  pl.BlockSpec((1, 1, tn), lambda l, j: (l, 0, j))],
        out_specs=pl.BlockSpec((1, MOD_ROWS, tn), lambda l, j: (l, 0, j)),
        out_shape=jax.ShapeDtypeStruct((depth, MOD_ROWS, n), F32),
        compiler_params=_params(2),
        name="modulation",
    )(cvec, mod_w, mod_b.reshape(depth, 1, n))


def _norm_mod_body(x_ref, g_ref, cv_ref, lv_ref, o_ref, *, tm, tpb, lc, r_shift, r_scale):
    x = x_ref[...]
    y = x * lax.rsqrt(jnp.mean(x * x, axis=-1, keepdims=True) + EPS) * g_ref[...]
    is_ctx = _is_ctx_rows(pl.program_id(0), tm, tpb, lc)
    shift = _pick_mod(is_ctx, cv_ref, lv_ref, r_shift)
    scale = _pick_mod(is_ctx, cv_ref, lv_ref, r_scale)
    o_ref[...] = (y * (1.0 + scale) + shift).astype(BF16)


def _norm_mod(x, gain, cv, lv, *, l_tot, lc, r_shift, r_scale):
    m, d = x.shape
    tm = _row_tile(l_tot, 544)
    tpb = l_tot // tm
    body = functools.partial(_norm_mod_body, tm=tm, tpb=tpb, lc=lc, r_shift=r_shift, r_scale=r_scale)
    return pl.pallas_call(
        body,
        grid=(m // tm,),
        in_specs=[pl.BlockSpec((tm, d), lambda i: (i, 0)),
                  pl.BlockSpec((1, d), lambda i: (0, 0)),
                  pl.BlockSpec((MOD_ROWS, d), lambda i: (0, 0)),
                  pl.BlockSpec((1, MOD_ROWS, d), lambda i: (i // tpb, 0, 0))],
        out_specs=pl.BlockSpec((tm, d), lambda i: (i, 0)),
        out_shape=jax.ShapeDtypeStruct((m, d), BF16),
        compiler_params=_params(1),
        name="norm_mod",
    )(x, gain.reshape(1, d), cv, lv)


def _mm_body(*refs, n_extra, n_out, epilogue, prologue):
    a_ref, w_ref = refs[0], refs[1]
    extra = refs[2:2 + n_extra]
    outs = refs[2 + n_extra:2 + n_extra + n_out]
    scratch = refs[2 + n_extra + n_out:]
    a = a_ref[...] if prologue is None else prologue(a_ref, extra, scratch)
    acc = jnp.dot(a, w_ref[...].astype(BF16), preferred_element_type=F32)
    epilogue(acc, extra, outs)


def _matmul(a, w, extras, extra_specs, out_shapes, out_specs, epilogue, *, tm, tn, name,
            prologue=None, scratch_shapes=()):
    m, k = a.shape
    n = w.shape[1]
    body = functools.partial(_mm_body, n_extra=len(extras), n_out=len(out_shapes),
                             epilogue=epilogue, prologue=prologue)
    return pl.pallas_call(
        body,
        grid=(m // tm, n // tn),
        in_specs=[pl.BlockSpec((tm, k), lambda i, j: (i, 0)),
                  pl.BlockSpec((k, tn), lambda i, j: (0, j))] + list(extra_specs),
        out_specs=list(out_specs),
        out_shape=list(out_shapes),
        scratch_shapes=list(scratch_shapes),
        compiler_params=_params(2),
        name=name,
    )(a, w, *extras)


def _plain_epilogue(out_dtype):
    def epilogue(acc, extra, outs):
        outs[0][...] = acc.astype(out_dtype)
    return epilogue


def _bias_epilogue(out_dtype):
    def epilogue(acc, extra, outs):
        outs[0][...] = (acc + extra[0][...]).astype(out_dtype)
    return epilogue


def _residual_matmul(a, w, bias, x_res, cv, lv, *, l_tot, lc, r_gate, tm, tn, name,
                     prologue=None, pro_extras=(), pro_specs=(), scratch_shapes=()):
    m = a.shape[0]
    n = w.shape[1]
    tpb = l_tot // tm
    n_pro = len(pro_extras)

    def epilogue(acc, extra, outs):
        b_ref, x_ref, cv_ref, lv_ref = extra[n_pro:n_pro + 4]
        is_ctx = _is_ctx_rows(pl.program_id(0), tm, tpb, lc)
        gate = _pick_mod(is_ctx, cv_ref, lv_ref, r_gate)
        outs[0][...] = x_ref[...] + gate * (acc + b_ref[...])

    extras = list(pro_extras) + [bias.reshape(1, n), x_res, cv, lv]
    specs = list(pro_specs) + [
        pl.BlockSpec((1, tn), lambda i, j: (0, j)),
        pl.BlockSpec((tm, tn), lambda i, j: (i, j)),
        pl.BlockSpec((MOD_ROWS, tn), lambda i, j: (0, j)),
        pl.BlockSpec((1, MOD_ROWS, tn), lambda i, j: (i // tpb, 0, j))]
    return _matmul(a, w, extras, specs,
                   [jax.ShapeDtypeStruct((m, n), F32)],
                   [pl.BlockSpec((tm, tn), lambda i, j: (i, j))],
                   epilogue, tm=tm, tn=tn, name=name, prologue=prologue,
                   scratch_shapes=scratch_shapes)[0]


def _ret_in_proj(h, w_in, cos, sin, *, l_tot, d_model):
    m = h.shape[0]
    n = w_in.shape[1]
    tm = _row_tile(l_tot, 2176)
    tn = 512
    tpb = l_tot // tm
    dk = d_model // RET_HEADS
    half = dk // 2
    n_q_tiles = d_model // tn
    k_scale = dk ** -0.5

    def epilogue(acc, extra, outs):
        cos_ref, sin_ref = extra
        j = pl.program_id(1)

        @pl.when(j < 2 * n_q_tiles)
        def _():
            c = cos_ref[...]
            s = sin_ref[...]
            mult = jnp.where(j >= n_q_tiles, k_scale, 1.0)
            for hh in range(tn // dk):
                x1 = acc[:, hh * dk:hh * dk + half]
                x2 = acc[:, hh * dk + half:(hh + 1) * dk]
                outs[0][:, hh * dk:hh * dk + half] = ((x1 * c - x2 * s) * mult).astype(BF16)
                outs[0][:, hh * dk + half:(hh + 1) * dk] = ((x2 * c + x1 * s) * mult).astype(BF16)

        @pl.when(j >= 2 * n_q_tiles)
        def _():
            outs[0][...] = acc.astype(BF16)

    tab_spec = pl.BlockSpec((tm, half), lambda i, j: (lax.rem(i, tpb), 0))
    return _matmul(h, w_in, [cos, sin], [tab_spec, tab_spec],
                   [jax.ShapeDtypeStruct((m, n), BF16)],
                   [pl.BlockSpec((tm, tn), lambda i, j: (i, j))],
                   epilogue, tm=tm, tn=tn, name="ret_in_proj")[0]


def _ret_body(lg_ref, q_ref, k_ref, v_ref, g_ref, gn_ref, o_ref, oacc_ref, sf_ref, sb_ref, *, nc_ctx, nc_tot):
    h = pl.program_id(1)
    ch = RET_CHUNK
    dv = v_ref.shape[-1]
    ii = lax.broadcasted_iota(jnp.int32, (ch, ch), 0).astype(F32)
    jj = lax.broadcasted_iota(jnp.int32, (ch, ch), 1).astype(F32)
    ri = lax.broadcasted_iota(jnp.int32, (ch, 1), 0).astype(F32)
    consts = []
    for direction in (0, 1):
        lg = lg_ref[direction, h]
        if direction == 0:
            diff = ii - jj
            q_decay = jnp.exp((ri + 1.0) * lg)
            k_decay = jnp.exp((ch - 1.0 - ri) * lg)
        else:
            diff = jj - ii
            q_decay = jnp.exp((ch - ri) * lg)
            k_decay = jnp.exp(ri * lg)
        intra = jnp.where(diff >= 0, jnp.exp(jnp.maximum(diff, 0.0) * lg), 0.0)
        chunk_decay = jnp.exp(jnp.full((1, dv), float(ch), F32) * lg)
        consts.append((intra, q_decay, k_decay, chunk_decay))
    sf_ref[...] = jnp.zeros_like(sf_ref)
    sb_ref[...] = jnp.zeros_like(sb_ref)
    oacc_ref[...] = jnp.zeros_like(oacc_ref)

    def chunk_rows(c):
        return pl.ds(pl.multiple_of(c * ch, ch), ch)

    def advance(c, s_ref, intra, q_decay, k_decay, chunk_decay):
        rows = chunk_rows(c)
        qc = q_ref[0, rows, :]
        kc = k_ref[0, rows, :]
        vc = v_ref[0, rows, :]
        scores = lax.dot_general(qc, kc, (((1,), (1,)), ((), ())), preferred_element_type=F32) * intra
        state = s_ref[...]
        out = (jnp.dot(scores.astype(BF16), vc, preferred_element_type=F32)
               + jnp.dot((qc.astype(F32) * q_decay).astype(BF16), state.astype(BF16),
                         preferred_element_type=F32))
        kt = (kc.astype(F32) * k_decay).T.astype(BF16)
        s_ref[...] = state * chunk_decay + jnp.dot(kt, vc, preferred_element_type=F32)
        return out

    def step(t, carry):
        c_fwd = t
        c_bwd = jnp.where(t < nc_ctx, nc_ctx - 1 - t, nc_tot - 1 - (t - nc_ctx))
        out_f = advance(c_fwd, sf_ref, *consts[0])
        out_b = advance(c_bwd, sb_ref, *consts[1])
        oacc_ref[chunk_rows(c_fwd), :] += out_f
        oacc_ref[chunk_rows(c_bwd), :] += out_b
        return carry

    lax.fori_loop(0, nc_tot, step, 0)

    def finish(c, carry):
        rows = chunk_rows(c)
        o = oacc_ref[rows, :]
        o = o * lax.rsqrt(jnp.mean(o * o, axis=-1, keepdims=True) + EPS) * gn_ref[...]
        g = g_ref[0, rows, :].astype(F32)
        o_ref[0, rows, :] = ((g * _sigmoid(g)) * o).astype(BF16)
        return carry

    lax.fori_loop(0, nc_tot, finish, 0)


def _retention(proj, log_gamma, gn, *, b, l_tot, lc, d_model):
    dk = d_model // RET_HEADS
    dv = 2 * d_model // RET_HEADS
    proj3 = proj.reshape(b, l_tot, proj.shape[-1])
    k_blk0 = d_model // dk
    v_blk0 = 2 * d_model // dv
    g_blk0 = 4 * d_model // dv
    body = functools.partial(_ret_body, nc_ctx=lc // RET_CHUNK, nc_tot=l_tot // RET_CHUNK)
    return pl.pallas_call(
        body,
        grid=(b, RET_HEADS),
        in_specs=[pl.BlockSpec(memory_space=pltpu.SMEM),
                  pl.BlockSpec((1, l_tot, dk), lambda bi, hi: (bi, 0, hi)),
                  pl.BlockSpec((1, l_tot, dk), lambda bi, hi: (bi, 0, k_blk0 + hi)),
                  pl.BlockSpec((1, l_tot, dv), lambda bi, hi: (bi, 0, v_blk0 + hi)),
                  pl.BlockSpec((1, l_tot, dv), lambda bi, hi: (bi, 0, g_blk0 + hi)),
                  pl.BlockSpec((1, dv), lambda bi, hi: (0, hi))],
        out_specs=pl.BlockSpec((1, l_tot, dv), lambda bi, hi: (bi, 0, hi)),
        out_shape=jax.ShapeDtypeStruct((b, l_tot, RET_HEADS * dv), BF16),
        scratch_shapes=[pltpu.VMEM((l_tot, dv), F32), pltpu.VMEM((dk, dv), F32), pltpu.VMEM((dk, dv), F32)],
        compiler_params=_params(2),
        name="retention",
    )(log_gamma, proj3, proj3, proj3, proj3, gn.reshape(1, RET_HEADS * dv))


def _retention_layer(x, h, cv, lv, w_in, decay, gn, w_out, cos, sin, *, b, l_tot, lc):
    m, d = x.shape
    proj = _ret_in_proj(h, w_in, cos, sin, l_tot=l_tot, d_model=d)
    log_gamma = jax.nn.log_sigmoid(decay.astype(F32))
    o = _retention(proj, log_gamma, gn, b=b, l_tot=l_tot, lc=lc, d_model=d).reshape(m, 2 * d)
    return _residual_matmul(o, w_out, jnp.zeros((d,), F32), x, cv, lv, l_tot=l_tot, lc=lc, r_gate=2,
                            tm=_row_tile(l_tot, 1088), tn=512, name="ret_out_proj")


def _mla_in_proj(h, w_in_pad, qa_gain, kva_gain, *, l_tot):
    m = h.shape[0]
    n = w_in_pad.shape[1]
    tm = _row_tile(l_tot, 1088)
    r = MLA_RANK

    def epilogue(acc, extra, outs):
        qg_ref, kg_ref = extra
        ql = acc[:, :r]
        kl = acc[:, r:2 * r]
        outs[0][...] = (ql * lax.rsqrt(jnp.mean(ql * ql, axis=-1, keepdims=True) + EPS)
                        * qg_ref[...]).astype(BF16)
        outs[1][...] = (kl * lax.rsqrt(jnp.mean(kl * kl, axis=-1, keepdims=True) + EPS)
                        * kg_ref[...]).astype(BF16)
        outs[2][...] = acc[:, 2 * r:]

    gspec = pl.BlockSpec((1, r), lambda i, j: (0, 0))
    return _matmul(h, w_in_pad, [qa_gain.reshape(1, r), kva_gain.reshape(1, r)], [gspec, gspec],
                   [jax.ShapeDtypeStruct((m, r), BF16), jax.ShapeDtypeStruct((m, r), BF16),
                    jax.ShapeDtypeStruct((m, LANES), F32)],
                   [pl.BlockSpec((tm, r), lambda i, j: (i, 0)),
                    pl.BlockSpec((tm, r), lambda i, j: (i, 0)),
                    pl.BlockSpec((tm, LANES), lambda i, j: (i, 0))],
                   epilogue, tm=tm, tn=n, name="mla_in_proj")


def _mla_q_proj(qn, wq_pad, q_gain_pad, tabs, *, l_tot):
    m = qn.shape[0]
    n = wq_pad.shape[1]
    tm = _row_tile(l_tot, 1088)
    tn = 1024
    tpb = l_tot // tm
    hw = 2 * LANES
    width = MLA_NOPE + MLA_ROPE
    scale = width ** -0.5 * LOG2_E

    def epilogue(acc, extra, outs):
        g_ref, c_ref, s1_ref, s2_ref = extra
        c, s1, s2 = c_ref[...], s1_ref[...], s2_ref[...]
        for hh in range(tn // hw):
            xs = acc[:, hh * hw:(hh + 1) * hw]
            inv = lax.rsqrt(jnp.sum(xs * xs, axis=-1, keepdims=True) * (1.0 / width) + EPS)
            y = xs * inv * g_ref[...] * scale
            outs[0][:, hh * hw:hh * hw + LANES] = y[:, :LANES].astype(BF16)
            outs[0][:, hh * hw + LANES:(hh + 1) * hw] = _rope3(y[:, LANES:], c, s1, s2).astype(BF16)

    tspec = pl.BlockSpec((tm, LANES), lambda i, j: (lax.rem(i, tpb), 0))
    return _matmul(qn, wq_pad, [q_gain_pad] + list(tabs),
                   [pl.BlockSpec((1, hw), lambda i, j: (0, 0)), tspec, tspec, tspec],
                   [jax.ShapeDtypeStruct((m, n), BF16)],
                   [pl.BlockSpec((tm, tn), lambda i, j: (i, j))],
                   epilogue, tm=tm, tn=tn, name="mla_q_proj")[0]


def _mla_kv_proj(kvn, w_kv, kr, k_gain_pad, tabs, *, l_tot):
    m = kvn.shape[0]
    n = w_kv.shape[1]
    tm = _row_tile(l_tot, 1088)
    tn = 1024
    tpb = l_tot // tm
    hw = MLA_NOPE + MLA_V
    width = MLA_NOPE + MLA_ROPE

    def epilogue(acc, extra, outs):
        kr_ref, g_ref, c_ref, s1_ref, s2_ref = extra
        kr_raw = kr_ref[...]
        ss_r = jnp.sum(kr_raw * kr_raw, axis=-1, keepdims=True)
        kr_rot = _rope3(kr_raw * g_ref[:, LANES:], c_ref[...], s1_ref[...], s2_ref[...])
        g_nope = g_ref[:, :LANES]
        for hh in range(tn // hw):
            kn = acc[:, hh * hw:hh * hw + MLA_NOPE]
            inv = lax.rsqrt((jnp.sum(kn * kn, axis=-1, keepdims=True) + ss_r) * (1.0 / width) + EPS)
            outs[0][:, hh * hw:hh * hw + LANES] = (kn * inv * g_nope).astype(BF16)
            outs[0][:, hh * hw + LANES:(hh + 1) * hw] = (kr_rot * inv).astype(BF16)
            outs[1][:, hh * MLA_V:(hh + 1) * MLA_V] = acc[:, hh * hw + MLA_NOPE:(hh + 1) * hw].astype(BF16)

    tspec = pl.BlockSpec((tm, LANES), lambda i, j: (lax.rem(i, tpb), 0))
    return _matmul(kvn, w_kv, [kr, k_gain_pad] + list(tabs),
                   [pl.BlockSpec((tm, LANES), lambda i, j: (i, 0)),
                    pl.BlockSpec((1, 2 * LANES), lambda i, j: (0, 0)), tspec, tspec, tspec],
                   [jax.ShapeDtypeStruct((m, n), BF16), jax.ShapeDtypeStruct((m, n // 2), BF16)],
                   [pl.BlockSpec((tm, tn), lambda i, j: (i, j)),
                    pl.BlockSpec((tm, tn // 2), lambda i, j: (i, j))],
                   epilogue, tm=tm, tn=tn, name="mla_kv_proj")


def _softmax_pv(q, k, v):
    s = lax.dot_general(q, k, (((1,), (1,)), ((), ())), preferred_element_type=F32)
    p = jnp.exp2(s - jnp.max(s, axis=-1, keepdims=True))
    den = jnp.sum(p, axis=-1, keepdims=True)
    return jnp.dot(p.astype(BF16), v, preferred_element_type=F32) / den


def _mla_attn_body(q_ref, k_ref, v_ref, o_ref, *, lc, tq, n_q):
    o_ref[0, 0:lc, :] = _softmax_pv(q_ref[0, 0:lc, :], k_ref[0, 0:lc, :], v_ref[0, 0:lc, :]).astype(BF16)

    def step(t, carry):
        rows = pl.ds(pl.multiple_of(lc + t * tq, LANES), tq)
        o_ref[0, rows, :] = _softmax_pv(q_ref[0, rows, :], k_ref[0], v_ref[0]).astype(BF16)
        return carry

    lax.fori_loop(0, n_q, step, 0, unroll=4)


def _mla_attention(q, k, v, *, b, l_tot, lc):
    hw = 2 * LANES
    tq = 256
    assert lc % LANES == 0 and (l_tot - lc) % tq == 0
    q3 = q.reshape(b, l_tot, MLA_HEADS * hw)
    k3 = k.reshape(b, l_tot, MLA_HEADS * hw)
    v3 = v.reshape(b, l_tot, MLA_HEADS * MLA_V)
    body = functools.partial(_mla_attn_body, lc=lc, tq=tq, n_q=(l_tot - lc) // tq)
    return pl.pallas_call(
        body,
        grid=(b, MLA_HEADS),
        in_specs=[pl.BlockSpec((1, l_tot, hw), lambda bi, hi: (bi, 0, hi)),
                  pl.BlockSpec((1, l_tot, hw), lambda bi, hi: (bi, 0, hi)),
                  pl.BlockSpec((1, l_tot, MLA_V), lambda bi, hi: (bi, 0, hi))],
        out_specs=pl.BlockSpec((1, l_tot, MLA_V), lambda bi, hi: (bi, 0, hi)),
        out_shape=jax.ShapeDtypeStruct((b, l_tot, MLA_HEADS * MLA_V), BF16),
        compiler_params=_params(2),
        name="mla_attention",
    )(q3, k3, v3)


def _mla_layer(x, h, cv, lv, w_in, qa_gain, kva_gain, w_q_b, w_kv_b, q_gain, k_gain, w_out, tabs,
               *, b, l_tot, lc):
    m, d = x.shape
    width = MLA_NOPE + MLA_ROPE
    hw = 2 * LANES
    w_in_pad = jnp.pad(w_in, ((0, 0), (0, 2 * MLA_RANK + LANES - w_in.shape[1])))
    wq_pad = jnp.pad(w_q_b.reshape(MLA_RANK, MLA_HEADS, width),
                     ((0, 0), (0, 0), (0, hw - width))).reshape(MLA_RANK, MLA_HEADS * hw)
    q_gain_pad = jnp.pad(q_gain, (0, hw - width)).reshape(1, hw)
    k_gain_pad = jnp.pad(k_gain, (0, hw - width)).reshape(1, hw)
    qn, kvn, kr = _mla_in_proj(h, w_in_pad, qa_gain, kva_gain, l_tot=l_tot)
    q = _mla_q_proj(qn, wq_pad, q_gain_pad, tabs, l_tot=l_tot)
    k, v = _mla_kv_proj(kvn, w_kv_b, kr, k_gain_pad, tabs, l_tot=l_tot)
    o = _mla_attention(q, k, v, b=b, l_tot=l_tot, lc=lc).reshape(m, MLA_HEADS * MLA_V)
    return _residual_matmul(o, w_out, jnp.zeros((d,), F32), x, cv, lv, l_tot=l_tot, lc=lc,
                            r_gate=2, tm=_row_tile(l_tot, 2176), tn=512, name="mla_out_proj")


def _conv_gate_body(bg_ref, cg_ref, u_ref, w_ref, cb_ref, o_ref, z_ref, *, l_tot, lc):
    pad = 8
    tn = o_ref.shape[-1]
    z_ref[0:pad, :] = jnp.zeros((pad, tn), F32)
    z_ref[pad + l_tot:, :] = jnp.zeros((pad, tn), F32)
    z_ref[pad:pad + l_tot, :] = cg_ref[0].astype(F32) * u_ref[0].astype(F32)
    rows = lax.broadcasted_iota(jnp.int32, (l_tot, 1), 0)
    z_prev = jnp.where(rows == lc, 0.0, z_ref[pad - 1:pad - 1 + l_tot, :])
    z_next = jnp.where(rows == lc - 1, 0.0, z_ref[pad + 1:pad + 1 + l_tot, :])
    z_mid = z_ref[pad:pad + l_tot, :]
    conv = z_prev * w_ref[0:1, :] + z_mid * w_ref[1:2, :] + z_next * w_ref[2:3, :] + cb_ref[...]
    o_ref[0] = (bg_ref[0].astype(F32) * conv).astype(BF16)


def _conv_gate(proj, conv_w, conv_b, *, b, l_tot, lc, d):
    tn = 256
    nb = d // tn
    proj3 = proj.reshape(b, l_tot, 3 * d)
    body = functools.partial(_conv_gate_body, l_tot=l_tot, lc=lc)
    return pl.pallas_call(
        body,
        grid=(b, nb),
        in_specs=[pl.BlockSpec((1, l_tot, tn), lambda bi, j: (bi, 0, j)),
                  pl.BlockSpec((1, l_tot, tn), lambda bi, j: (bi, 0, nb + j)),
                  pl.BlockSpec((1, l_tot, tn), lambda bi, j: (bi, 0, 2 * nb + j)),
                  pl.BlockSpec((3, tn), lambda bi, j: (0, j)),
                  pl.BlockSpec((1, tn), lambda bi, j: (0, j))],
        out_specs=pl.BlockSpec((1, l_tot, tn), lambda bi, j: (bi, 0, j)),
        out_shape=jax.ShapeDtypeStruct((b, l_tot, d), BF16),
        scratch_shapes=[pltpu.VMEM((l_tot + 16, tn), F32)],
        compiler_params=_params(2),
        name="conv_gate",
    )(proj3, proj3, proj3, conv_w, conv_b.reshape(1, d))


def _shortconv_layer(x, h, cv, lv, w_in, conv_w, conv_b, w_out, *, b, l_tot, lc):
    m, d = x.shape
    tm = _row_tile(l_tot, 2176)
    proj = _matmul(h, w_in, [], [], [jax.ShapeDtypeStruct((m, 3 * d), BF16)],
                   [pl.BlockSpec((tm, 512), lambda i, j: (i, j))],
                   _plain_epilogue(BF16), tm=tm, tn=512, name="sc_in_proj")[0]
    a = _conv_gate(proj, conv_w, conv_b, b=b, l_tot=l_tot, lc=lc, d=d).reshape(m, d)
    return _residual_matmul(a, w_out, jnp.zeros((d,), F32), x, cv, lv, l_tot=l_tot, lc=lc,
                            r_gate=2, tm=tm, tn=512, name="sc_out_proj")


def _split_hi_lo(x):
    hi = x.astype(BF16)
    lo = (x - hi.astype(F32)).astype(BF16)
    return hi, lo


def _head_rms_scale(x, seg_ref, segt_ref):
    hi, lo = _split_hi_lo(x * x)
    seg = seg_ref[...]
    ss = jnp.dot(hi, seg, preferred_element_type=F32) + jnp.dot(lo, seg, preferred_element_type=F32)
    inv = lax.rsqrt(ss * (1.0 / GQA_HEAD_DIM) + EPS)
    ihi, ilo = _split_hi_lo(inv)
    segt = segt_ref[...]
    return jnp.dot(ihi, segt, preferred_element_type=F32) + jnp.dot(ilo, segt, preferred_element_type=F32)


def _gqa_in_proj(h, w_in, b_in, qg_t, kg_t, seg, segt, tabs, *, l_tot):
    m = h.shape[0]
    n = w_in.shape[1]
    tm = _row_tile(l_tot, 1088)
    tn = 512
    tpb = l_tot // tm
    n_q_tiles = GQA_HEADS * GQA_HEAD_DIM // tn
    kv_w = GQA_KV_HEADS * GQA_HEAD_DIM
    assert n == n_q_tiles * tn + 2 * kv_w and 2 * kv_w == tn
    scale = GQA_HEAD_DIM ** -0.5 * LOG2_E

    def norm_rope(xs, g, extra, mult):
        seg_ref, segt_ref, c_ref, s1_ref, s2_ref = extra[3:8]
        y = xs * _head_rms_scale(xs, seg_ref, segt_ref) * g
        c, s1, s2 = c_ref[...], s1_ref[...], s2_ref[...]
        return [_rope3(y[:, a * LANES:(a + 1) * LANES], c, s1, s2) * mult
                for a in range(xs.shape[1] // LANES)]

    def dup_heads(slab, lo_mask):
        swapped = pltpu.roll(slab, GQA_HEAD_DIM, axis=1)
        return jnp.where(lo_mask, slab, swapped), jnp.where(lo_mask, swapped, slab)

    def epilogue(acc, extra, outs):
        b_ref, qg_ref, kg_ref = extra[0:3]
        j = pl.program_id(1)
        acc = acc + b_ref[...]

        @pl.when(j < n_q_tiles)
        def _():
            for a, slab in enumerate(norm_rope(acc, qg_ref[...], extra, scale)):
                outs[0][:, a * LANES:(a + 1) * LANES] = slab.astype(BF16)

        @pl.when(j == n_q_tiles)
        def _():
            lo_mask = lax.broadcasted_iota(jnp.int32, (tm, LANES), 1) < GQA_HEAD_DIM
            kpad = jnp.concatenate([acc[:, :kv_w], acc[:, :kv_w]], axis=1)
            k_slabs = norm_rope(kpad, kg_ref[...], extra, 1.0)[:kv_w // LANES]
            for a in range(kv_w // LANES):
                ka, kb = dup_heads(k_slabs[a], lo_mask)
                outs[1][:, (2 * a) * LANES:(2 * a + 1) * LANES] = ka.astype(BF16)
                outs[1][:, (2 * a + 1) * LANES:(2 * a + 2) * LANES] = kb.astype(BF16)
                va, vb = dup_heads(acc[:, kv_w + a * LANES:kv_w + (a + 1) * LANES], lo_mask)
                outs[2][:, (2 * a) * LANES:(2 * a + 1) * LANES] = va.astype(BF16)
                outs[2][:, (2 * a + 1) * LANES:(2 * a + 2) * LANES] = vb.astype(BF16)

    tspec = pl.BlockSpec((tm, LANES), lambda i, j: (lax.rem(i, tpb), 0))
    full = lambda shape: pl.BlockSpec(shape, lambda i, j: (0, 0))
    return _matmul(
        h, w_in, [b_in.reshape(1, n), qg_t, kg_t, seg, segt] + list(tabs),
        [pl.BlockSpec((1, tn), lambda i, j: (0, j)), full((1, tn)), full((1, tn)),
         full((tn, LANES)), full((LANES, tn)), tspec, tspec, tspec],
        [jax.ShapeDtypeStruct((m, n_q_tiles * tn), BF16),
         jax.ShapeDtypeStruct((m, GQA_KV_HEADS * LANES), BF16),
         jax.ShapeDtypeStruct((m, GQA_KV_HEADS * LANES), BF16)],
        [pl.BlockSpec((tm, tn), lambda i, j: (i, jnp.minimum(j, n_q_tiles - 1))),
         pl.BlockSpec((tm, GQA_KV_HEADS * LANES), lambda i, j: (i, 0)),
         pl.BlockSpec((tm, GQA_KV_HEADS * LANES), lambda i, j: (i, 0))],
        epilogue, tm=tm, tn=tn, name="gqa_in_proj")


def _gqa_attn_body(sink_ref, q_ref, k_ref, v_ref, o_ref, *, lc, nb):
    g = pl.program_id(1)
    r = WINDOW
    n_slab = GQA_GROUP // 2
    rows_all = GQA_GROUP * r
    row_head = lax.broadcasted_iota(jnp.int32, (rows_all, 1), 0) // r
    sink = jnp.zeros((rows_all, 1), F32)
    for hh in range(GQA_GROUP):
        sink = jnp.where(row_head == hh, sink_ref[g * GQA_GROUP + hh] * LOG2_E, sink)
    lo_mask = lax.broadcasted_iota(jnp.int32, (r, LANES), 1) < GQA_HEAD_DIM
    ii = lax.rem(lax.broadcasted_iota(jnp.int32, (rows_all, r), 0), r)
    jj = lax.broadcasted_iota(jnp.int32, (rows_all, r), 1)
    k_ctx = k_ref[0, 0:lc, :]
    v_ctx = v_ref[0, 0:lc, :]

    def stack_q(rows):
        parts = []
        for a in range(n_slab):
            qa = q_ref[0, rows, a * LANES:(a + 1) * LANES]
            zero = jnp.zeros_like(qa)
            parts += [jnp.where(lo_mask, qa, zero), jnp.where(lo_mask, zero, qa)]
        return jnp.concatenate(parts, axis=0)

    def attend(rows, blocks):
        qs = stack_q(rows)
        scores = []
        for kb, _, mask in blocks:
            s = lax.dot_general(qs, kb, (((1,), (1,)), ((), ())), preferred_element_type=F32)
            scores.append(s if mask is None else jnp.where(mask, s, NEG_INF))
        slabs = lambda a: [a[:, c * LANES:(c + 1) * LANES] for c in range(a.shape[1] // LANES)]
        mx = functools.reduce(jnp.maximum, [sl for s in scores for sl in slabs(s)])
        mx = jnp.maximum(sink, jnp.max(mx, axis=-1, keepdims=True))
        psum = jnp.zeros((rows_all, LANES), F32)
        acc = jnp.zeros((rows_all, LANES), F32)
        for s, (_, vb, _) in zip(scores, blocks):
            p = jnp.exp2(s - mx)
            psum = functools.reduce(jnp.add, slabs(p), psum)
            acc = acc + jnp.dot(p.astype(BF16), vb, preferred_element_type=F32)
        o = acc / (jnp.exp2(sink - mx) + jnp.sum(psum, axis=-1, keepdims=True))
        for a in range(n_slab):
            oa = jnp.where(lo_mask, o[(2 * a) * r:(2 * a + 1) * r], o[(2 * a + 1) * r:(2 * a + 2) * r])
            o_ref[0, rows, a * LANES:(a + 1) * LANES] = oa.astype(BF16)

    for cb in range(lc // r):
        attend(pl.ds(cb * r, r), [(k_ctx, v_ctx, None)])

    def step(t, carry):
        r_cur = pl.multiple_of(lc + t * r, r)
        r_prev = pl.multiple_of(lc + jnp.maximum(t - 1, 0) * r, r)
        r_next = pl.multiple_of(lc + jnp.minimum(t + 1, nb - 1) * r, r)
        blk = lambda r0: (k_ref[0, pl.ds(r0, r), :], v_ref[0, pl.ds(r0, r), :])
        m_prev = jnp.logical_and(jj >= ii, t > 0)
        m_next = jnp.logical_and(jj <= ii, t < nb - 1)
        attend(pl.ds(r_cur, r), [blk(r_prev) + (m_prev,), blk(r_cur) + (None,),
                                 blk(r_next) + (m_next,), (k_ctx, v_ctx, None)])
        return carry

    lax.fori_loop(0, nb, step, 0, unroll=2)


def _gqa_attention(q, k, v, sink, *, b, l_tot, lc):
    gw = GQA_GROUP * GQA_HEAD_DIM
    q3 = q.reshape(b, l_tot, GQA_HEADS * GQA_HEAD_DIM)
    k3 = k.reshape(b, l_tot, GQA_KV_HEADS * LANES)
    v3 = v.reshape(b, l_tot, GQA_KV_HEADS * LANES)
    body = functools.partial(_gqa_attn_body, lc=lc, nb=(l_tot - lc) // WINDOW)
    return pl.pallas_call(
        body,
        grid=(b, GQA_KV_HEADS),
        in_specs=[pl.BlockSpec(memory_space=pltpu.SMEM),
                  pl.BlockSpec((1, l_tot, gw), lambda bi, gi: (bi, 0, gi)),
                  pl.BlockSpec((1, l_tot, LANES), lambda bi, gi: (bi, 0, gi)),
                  pl.BlockSpec((1, l_tot, LANES), lambda bi, gi: (bi, 0, gi))],
        out_specs=pl.BlockSpec((1, l_tot, gw), lambda bi, gi: (bi, 0, gi)),
        out_shape=jax.ShapeDtypeStruct((b, l_tot, GQA_HEADS * GQA_HEAD_DIM), BF16),
        compiler_params=_params(2),
        name="gqa_attention",
    )(sink, q3, k3, v3)


def _gqa_layer(x, h, cv, lv, w_in, b_in, q_gain, k_gain, sink, w_out, b_out, tabs, *, b, l_tot, lc):
    m, d = x.shape
    tn = 512
    heads_per_tile = tn // GQA_HEAD_DIM
    col = jnp.arange(tn)
    seg = (col[:, None] // GQA_HEAD_DIM == jnp.arange(LANES)[None, :]).astype(BF16)
    qg_t = jnp.tile(q_gain, heads_per_tile).reshape(1, tn)
    kg_t = jnp.tile(k_gain, heads_per_tile).reshape(1, tn)
    q, k, v = _gqa_in_proj(h, w_in, b_in, qg_t, kg_t, seg, seg.T, tabs, l_tot=l_tot)
    o = _gqa_attention(q, k, v, sink.astype(F32), b=b, l_tot=l_tot, lc=lc).reshape(m, d)
    return _residual_matmul(o, w_out, b_out, x, cv, lv, l_tot=l_tot, lc=lc, r_gate=2,
                            tm=_row_tile(l_tot, 2176), tn=512, name="gqa_out_proj")


def _pack_halves(yb):
    w = yb.shape[1] // 2
    lo = lax.bitcast_convert_type(yb[:, :w].astype(F32), jnp.uint32)
    hi = lax.bitcast_convert_type(yb[:, w:].astype(F32), jnp.uint32)
    return lax.shift_right_logical(lo, jnp.uint32(16)) | hi


def _store_row_tiles(ref, words):
    n = words.shape[0]
    for s in range(ROW_TILE):
        ref[pl.ds(s, n, stride=ROW_TILE), :] = words[:, s * LANES:(s + 1) * LANES]


def _load_row_tiles(ref, n):
    return jnp.concatenate([ref[pl.ds(s, n, stride=ROW_TILE), :] for s in range(ROW_TILE)], axis=1)


def _tile_of_row(r):
    return pl.ds(pl.multiple_of(r * ROW_TILE, ROW_TILE), ROW_TILE)


def _unpack_halves_f32(words):
    lo = lax.bitcast_convert_type(lax.shift_left(words, jnp.uint32(16)), F32)
    hi = lax.bitcast_convert_type(words & jnp.uint32(0xFFFF0000), F32)
    return lo, hi


def _unpack_halves(words):
    lo, hi = _unpack_halves_f32(words)
    return lo.astype(BF16), hi.astype(BF16)


def _ffn_route_body(x_ref, g_ref, cv_ref, lv_ref, rw_ref, rb_ref, fpk_ref, ri_ref, rg_ref, cnt_ref,
                    carry_ref, *, tm, tpb, lc, route_ctx, n_e):
    i = pl.program_id(0)

    @pl.when(i == 0)
    def _():
        carry_ref[...] = jnp.zeros_like(carry_ref)

    x = x_ref[...]
    y = x * lax.rsqrt(jnp.mean(x * x, axis=-1, keepdims=True) + EPS) * g_ref[...]
    is_ctx = _is_ctx_rows(i, tm, tpb, lc)
    y = y * (1.0 + _pick_mod(is_ctx, cv_ref, lv_ref, 4)) + _pick_mod(is_ctx, cv_ref, lv_ref, 3)
    yb = y.astype(BF16)
    _store_row_tiles(fpk_ref, _pack_halves(yb))

    lane = lax.broadcasted_iota(jnp.int32, (tm, LANES), 1)
    logits = jnp.dot(yb, rw_ref[...].astype(BF16), preferred_element_type=F32) + rb_ref[...]
    logits = jnp.where(lane < n_e, logits, NEG_INF)
    vals, idxs, hots = [], [], []
    for _ in range(TOP_K):
        mk = jnp.max(logits, axis=-1, keepdims=True)
        ik = jnp.min(jnp.where(logits == mk, lane, LANES), axis=-1, keepdims=True)
        hot = lane == ik
        vals.append(mk)
        idxs.append(ik)
        hots.append(hot)
        logits = jnp.where(hot, NEG_INF, logits)
    exps = [jnp.exp(v - vals[0]) for v in vals]
    den = exps[0] + exps[1] + exps[2] + exps[3]

    routed = jnp.ones_like(is_ctx) if route_ctx else jnp.logical_not(is_ctx)
    chosen = jnp.zeros((tm, LANES), F32)
    for hot in hots:
        chosen = chosen + jnp.where(jnp.logical_and(hot, routed), 1.0, 0.0)
    before = (lax.broadcasted_iota(jnp.int32, (tm, tm), 0)
              > lax.broadcasted_iota(jnp.int32, (tm, tm), 1)).astype(BF16)
    arrived = jnp.dot(before, chosen.astype(BF16), preferred_element_type=F32) + carry_ref[...]
    carry_ref[...] = carry_ref[...] + jnp.sum(chosen, axis=0, keepdims=True)
    cnt_ref[...] = carry_ref[...]

    ri = jnp.zeros((tm, LANES), jnp.int32)
    rg = jnp.zeros((tm, LANES), F32)
    for k in range(TOP_K):
        rank = jnp.sum(jnp.where(hots[k], arrived, 0.0), axis=-1, keepdims=True)
        ri = jnp.where(lane == k, idxs[k], ri)
        ri = jnp.where(lane == TOP_K + k, rank.astype(jnp.int32), ri)
        rg = jnp.where(lane == k, exps[k] / den, rg)
    ri_ref[...] = ri
    rg_ref[...] = rg


def _ffn_route(x, gain, cv, lv, router_w_pad, router_b_pad, layer, *, l_tot, lc, route_ctx, n_e):
    m, d = x.shape
    tm = _row_tile(l_tot, 544)
    tpb = l_tot // tm
    body = functools.partial(_ffn_route_body, tm=tm, tpb=tpb, lc=lc, route_ctx=route_ctx, n_e=n_e)
    return pl.pallas_call(
        body,
        grid=(m // tm,),
        in_specs=[pl.BlockSpec((tm, d), lambda i: (i, 0)),
                  pl.BlockSpec((1, d), lambda i: (0, 0)),
                  pl.BlockSpec((MOD_ROWS, d), lambda i: (0, 0)),
                  pl.BlockSpec((1, MOD_ROWS, d), lambda i: (i // tpb, 0, 0)),
                  pl.BlockSpec((None, d, LANES), lambda i: (layer, 0, 0)),
                  pl.BlockSpec((None, 1, LANES), lambda i: (layer, 0, 0))],
        out_specs=[pl.BlockSpec((tm * ROW_TILE, LANES), lambda i: (i, 0)),
                   pl.BlockSpec((tm, LANES), lambda i: (i, 0)),
                   pl.BlockSpec((tm, LANES), lambda i: (i, 0)),
                   pl.BlockSpec((1, LANES), lambda i: (0, 0))],
        out_shape=[jax.ShapeDtypeStruct((m * ROW_TILE, LANES), jnp.uint32),
                   jax.ShapeDtypeStruct((m, LANES), jnp.int32),
                   jax.ShapeDtypeStruct((m, LANES), F32),
                   jax.ShapeDtypeStruct((1, LANES), F32)],
        scratch_shapes=[pltpu.VMEM((1, LANES), F32)],
        compiler_params=_params(1),
        name="ffn_route",
    )(x, gain.reshape(1, d), cv, lv, router_w_pad, router_b_pad)


def _dispatch_body(dest_ref, f_ref, init_hbm, xr_hbm, sem, *, tm, tpb, ctx_tiles, route_ctx):
    del init_hbm
    i = pl.program_id(0)

    def scatter_rows():
        def start(r, carry):
            a0 = (i * tm + r) * TOP_K
            for k in range(TOP_K):
                pltpu.make_async_copy(f_ref.at[_tile_of_row(r)], xr_hbm.at[_tile_of_row(dest_ref[a0 + k])],
                                      sem).start()
            return carry

        lax.fori_loop(0, tm, start, 0, unroll=ISSUE_UNROLL)

        def wait(r, carry):
            for k in range(TOP_K):
                pltpu.make_async_copy(f_ref.at[_tile_of_row(0)], xr_hbm.at[_tile_of_row(0)], sem).wait()
            return carry

        lax.fori_loop(0, tm, wait, 0, unroll=ISSUE_UNROLL)

    if route_ctx:
        scatter_rows()
    else:
        pl.when(lax.rem(i, tpb) >= ctx_tiles)(scatter_rows)


def _dispatch(fpk, dest, n_rows, *, l_tot, lc, route_ctx):
    m = fpk.shape[0] // ROW_TILE
    tm = 256
    assert lc % tm == 0 and l_tot % tm == 0
    body = functools.partial(_dispatch_body, tm=tm, tpb=l_tot // tm, ctx_tiles=lc // tm, route_ctx=route_ctx)
    return pl.pallas_call(
        body,
        grid_spec=pltpu.PrefetchScalarGridSpec(
            num_scalar_prefetch=1,
            grid=(m // tm,),
            in_specs=[pl.BlockSpec((tm * ROW_TILE, LANES), lambda i, dst: (i, 0)),
                      pl.BlockSpec(memory_space=pl.ANY)],
            out_specs=pl.BlockSpec(memory_space=pl.ANY),
            scratch_shapes=[pltpu.SemaphoreType.DMA(())]),
        out_shape=jax.ShapeDtypeStruct((n_rows * ROW_TILE, LANES), jnp.uint32),
        input_output_aliases={2: 0},
        compiler_params=_row_copy_params(),
        name="moe_dispatch",
    )(dest, fpk, jnp.zeros((n_rows * ROW_TILE, LANES), jnp.uint32))


def _moe_gu_body(be_ref, nu_ref, x_ref, wgu_ref, bgu_ref, o_ref, wgb_ref, wlb_ref):
    blk = pl.program_id(0)
    ff = wgb_ref.shape[1]
    prev = be_ref[jnp.maximum(blk - 1, 0)]

    @pl.when(jnp.logical_or(blk == 0, be_ref[blk] != prev))
    def _():
        wgb_ref[...] = wgu_ref[:, :ff].astype(BF16)
        wlb_ref[...] = wgu_ref[:, ff:].astype(BF16)

    @pl.when(blk < nu_ref[0])
    def _():
        x_lo, x_hi = _unpack_halves(_load_row_tiles(x_ref, MOE_TM))
        w = x_lo.shape[1]
        h_glu = (jnp.dot(x_lo, wgb_ref[0:w, :], preferred_element_type=F32)
                 + jnp.dot(x_hi, wgb_ref[w:, :], preferred_element_type=F32) + bgu_ref[:, :ff])
        h_lin = (jnp.dot(x_lo, wlb_ref[0:w, :], preferred_element_type=F32)
                 + jnp.dot(x_hi, wlb_ref[w:, :], preferred_element_type=F32) + bgu_ref[:, ff:])
        h_glu = jnp.minimum(h_glu, SWIGLU_LIMIT)
        h_lin = jnp.clip(h_lin, -SWIGLU_LIMIT, SWIGLU_LIMIT)
        o_ref[...] = (h_glu * _sigmoid(SWIGLU_ALPHA * h_glu) * (h_lin + 1.0)).astype(BF16)

    @pl.when(blk >= nu_ref[0])
    def _():
        o_ref[...] = jnp.zeros_like(o_ref)


def _moe_down_body(be_ref, nu_ref, a_ref, w_ref, b_ref, o_ref, wb_ref):
    blk = pl.program_id(0)
    prev = be_ref[jnp.maximum(blk - 1, 0)]

    @pl.when(jnp.logical_or(blk == 0, be_ref[blk] != prev))
    def _():
        wb_ref[...] = w_ref[...].astype(BF16)

    @pl.when(blk < nu_ref[0])
    def _():
        y = jnp.dot(a_ref[...], wb_ref[...], preferred_element_type=F32) + b_ref[...]
        _store_row_tiles(o_ref, _pack_halves(y.astype(BF16)))

    @pl.when(blk >= nu_ref[0])
    def _():
        o_ref[...] = jnp.zeros_like(o_ref)


def _moe_experts(x_rows, block_expert, n_used, w_gu, b_gu, w_down, b_down, layer):
    n_rows = x_rows.shape[0] // ROW_TILE
    n_blocks = n_rows // MOE_TM
    tile_rows = pl.BlockSpec((MOE_TM * ROW_TILE, LANES), lambda i, be, nu: (i, 0))
    depth, n_e, ff, d = w_down.shape
    act = pl.pallas_call(
        _moe_gu_body,
        grid_spec=pltpu.PrefetchScalarGridSpec(
            num_scalar_prefetch=2,
            grid=(n_blocks,),
            in_specs=[tile_rows,
                      pl.BlockSpec((None, None, d, 2 * ff), lambda i, be, nu: (layer, be[i], 0, 0)),
                      pl.BlockSpec((None, None, 1, 2 * ff), lambda i, be, nu: (layer, be[i], 0, 0))],
            out_specs=pl.BlockSpec((MOE_TM, ff), lambda i, be, nu: (i, 0)),
            scratch_shapes=[pltpu.VMEM((d, ff), BF16), pltpu.VMEM((d, ff), BF16)]),
        out_shape=jax.ShapeDtypeStruct((n_rows, ff), BF16),
        compiler_params=_params(1),
        name="moe_gate_up",
    )(block_expert, n_used, x_rows, w_gu, b_gu.reshape(depth, n_e, 1, 2 * ff))
    return pl.pallas_call(
        _moe_down_body,
        grid_spec=pltpu.PrefetchScalarGridSpec(
            num_scalar_prefetch=2,
            grid=(n_blocks,),
            in_specs=[pl.BlockSpec((MOE_TM, ff), lambda i, be, nu: (i, 0)),
                      pl.BlockSpec((None, None, ff, d), lambda i, be, nu: (layer, be[i], 0, 0)),
                      pl.BlockSpec((None, None, 1, d), lambda i, be, nu: (layer, be[i], 0, 0))],
            out_specs=tile_rows,
            scratch_shapes=[pltpu.VMEM((ff, d), BF16)]),
        out_shape=jax.ShapeDtypeStruct((n_rows * ROW_TILE, LANES), jnp.uint32),
        compiler_params=_params(1),
        name="moe_down",
    )(block_expert, n_used, act, w_down, b_down.reshape(depth, n_e, 1, d))


def _combine_body(dest_ref, x_ref, rg_ref, cv_ref, lv_ref, y_hbm, o_ref, buf, sem,
                  *, tt, n_tiles, tpb, ctx_tiles, lc, route_ctx):
    i = pl.program_id(0)

    def routed(tile):
        return (tile >= 0) if route_ctx else (lax.rem(tile, tpb) >= ctx_tiles)

    def row_copy(src, slot, k, r):
        return pltpu.make_async_copy(y_hbm.at[_tile_of_row(src)], buf.at[slot, k, _tile_of_row(r)], sem.at[slot])

    def issue(tile, slot):
        def start(r, carry):
            a0 = (tile * tt + r) * TOP_K
            for k in range(TOP_K):
                row_copy(dest_ref[a0 + k], slot, k, r).start()
            return carry

        lax.fori_loop(0, tt, start, 0, unroll=ISSUE_UNROLL)

    @pl.when(jnp.logical_and(i == 0, routed(i)))
    def _():
        issue(i, 0)

    nxt = jnp.minimum(i + 1, n_tiles - 1)

    @pl.when(jnp.logical_and(i + 1 < n_tiles, routed(nxt)))
    def _():
        issue(nxt, lax.rem(i + 1, 2))

    slot = lax.rem(i, 2)

    @pl.when(routed(i))
    def _():
        def wait(r, carry):
            for k in range(TOP_K):
                row_copy(0, slot, k, 0).wait()
            return carry

        lax.fori_loop(0, tt, wait, 0, unroll=ISSUE_UNROLL)
        rg = rg_ref[...]
        w = ROW_TILE * LANES
        acc_lo = jnp.zeros((tt, w), F32)
        acc_hi = jnp.zeros((tt, w), F32)
        for k in range(TOP_K):
            y_lo, y_hi = _unpack_halves_f32(_load_row_tiles(buf.at[slot, k], tt))
            acc_lo = acc_lo + rg[:, k:k + 1] * y_lo
            acc_hi = acc_hi + rg[:, k:k + 1] * y_hi
        gate = _pick_mod(_is_ctx_rows(i, tt, tpb, lc), cv_ref, lv_ref, 5)
        o_ref[:, :w] = x_ref[:, :w] + gate[:, :w] * acc_lo
        o_ref[:, w:] = x_ref[:, w:] + gate[:, w:] * acc_hi

    @pl.when(jnp.logical_not(routed(i)))
    def _():
        o_ref[...] = x_ref[...]


def _combine(x, y_rows, rg, dest, cv, lv, *, l_tot, lc, route_ctx):
    m, d = x.shape
    tt = 128
    assert lc % tt == 0 and l_tot % tt == 0
    tpb = l_tot // tt
    n_tiles = m // tt
    body = functools.partial(_combine_body, tt=tt, n_tiles=n_tiles, tpb=tpb, ctx_tiles=lc // tt, lc=lc,
                             route_ctx=route_ctx)
    return pl.pallas_call(
        body,
        grid_spec=pltpu.PrefetchScalarGridSpec(
            num_scalar_prefetch=1,
            grid=(n_tiles,),
            in_specs=[pl.BlockSpec((tt, d), lambda i, dst: (i, 0)),
                      pl.BlockSpec((tt, LANES), lambda i, dst: (i, 0)),
                      pl.BlockSpec((MOD_ROWS, d), lambda i, dst: (0, 0)),
                      pl.BlockSpec((1, MOD_ROWS, d), lambda i, dst: (i // tpb, 0, 0)),
                      pl.BlockSpec(memory_space=pl.ANY)],
            out_specs=pl.BlockSpec((tt, d), lambda i, dst: (i, 0)),
            scratch_shapes=[pltpu.VMEM((2, TOP_K, tt * ROW_TILE, LANES), jnp.uint32),
                            pltpu.SemaphoreType.DMA((2,))]),
        out_shape=jax.ShapeDtypeStruct((m, d), F32),
        compiler_params=_row_copy_params(),
        name="moe_combine",
    )(dest, x, rg, cv, lv, y_rows)


def _moe_layer(x, gain, cv, lv, router_w_pad, router_b_pad, w_gu, b_gu, w_down, b_down, layer,
               *, l_tot, lc, route_ctx):
    m, d = x.shape
    n_e = w_gu.shape[1]
    fpk, ri, rg, cnt = _ffn_route(x, gain, cv, lv, router_w_pad, router_b_pad, layer,
                                  l_tot=l_tot, lc=lc, route_ctx=route_ctx, n_e=n_e)
    counts = cnt[0, :n_e].astype(jnp.int32)
    padded = (counts + MOE_TM - 1) // MOE_TM * MOE_TM
    pend = jnp.cumsum(padded)
    pstart = (pend - padded).astype(jnp.int32)
    n_blocks = -(-m * TOP_K // MOE_TM) + n_e
    n_rows = n_blocks * MOE_TM
    block_start = jnp.arange(n_blocks, dtype=jnp.int32) * MOE_TM
    block_expert = jnp.minimum(jnp.sum(block_start[:, None] >= pend[None, :], axis=1), n_e - 1).astype(jnp.int32)
    n_used = (pend[-1] // MOE_TM).astype(jnp.int32).reshape(1)
    dest = (pstart[ri[:, :TOP_K]] + ri[:, TOP_K:2 * TOP_K]).reshape(m * TOP_K)
    x_rows = _dispatch(fpk, dest, n_rows, l_tot=l_tot, lc=lc, route_ctx=route_ctx)
    y_rows = _moe_experts(x_rows, block_expert, n_used, w_gu, b_gu, w_down, b_down, layer)
    return _combine(x, y_rows, rg, dest, cv, lv, l_tot=l_tot, lc=lc, route_ctx=route_ctx)


def _rope_1d_tables(l_tot, half):
    inv = ROPE_BASE ** (-jnp.arange(half, dtype=F32) / half)
    ang = jnp.arange(l_tot, dtype=F32)[:, None] * inv
    return jnp.cos(ang), jnp.sin(ang)


def _axial_tables(l_tot, lc):
    n_freq = GQA_HEAD_DIM // 4
    t = jnp.maximum(jnp.arange(l_tot) - lc, 0)
    pos = jnp.stack([(t // GRID_W).astype(F32), (t % GRID_W).astype(F32)], axis=-1)
    inv_freq = ROPE_BASE ** (-jnp.arange(n_freq, dtype=F32) / n_freq)
    ang = pos[:, :, None] * inv_freq
    lane = jnp.arange(LANES)
    axis = (lane % GQA_HEAD_DIM) // (2 * n_freq)
    freq = lane % n_freq
    first_half = (lane % (2 * n_freq)) < n_freq
    is_lat = (jnp.arange(l_tot) >= lc)[:, None]
    cos = jnp.where(is_lat, jnp.cos(ang)[:, axis, freq], 1.0)
    sin = jnp.where(is_lat, jnp.sin(ang)[:, axis, freq], 0.0)
    return cos, jnp.where(first_half, -sin, 0.0), jnp.where(first_half, 0.0, sin)


def kernel(x, c, ctx, c_ctx, norm_mix, norm_ffn, mod_w, mod_b, ret_w_in, ret_decay, ret_gn, ret_w_out, mla_w_in, mla_q_a_norm, mla_kv_a_norm, mla_w_q_b, mla_w_kv_b, mla_q_norm, mla_k_norm, mla_w_out, sc_w_in, sc_conv_w, sc_conv_b, sc_w_out, gqa_w_in, gqa_b_in, gqa_q_norm, gqa_k_norm, gqa_sink, gqa_w_out, gqa_b_out, moe_router_w, moe_router_b, moe_w_gu, moe_b_gu, moe_w_down, moe_b_down):
    b, n_lat, d = x.shape
    lc = ctx.shape[1]
    l_tot = lc + n_lat
    m = b * l_tot
    depth = mod_w.shape[0]
    assert b + 1 <= MOD_ROWS and lc % 256 == 0 and n_lat % 256 == 0

    xs = jnp.concatenate([ctx, x], axis=1).reshape(m, d)
    cvec = jnp.zeros((MOD_ROWS, d), F32).at[:b].set(c).at[b].set(c_ctx)
    mod = _modulation(cvec, mod_w, mod_b).reshape(depth, MOD_ROWS, 6, d)
    pad_rows = ((0, 0), (0, MOD_ROWS - 6), (0, 0))
    ret_cos, ret_sin = _rope_1d_tables(l_tot, d // RET_HEADS // 2)
    tabs = _axial_tables(l_tot, lc)
    n_e = moe_router_w.shape[-1]
    router_w_pad = jnp.pad(moe_router_w, ((0, 0), (0, 0), (0, LANES - n_e)))
    router_b_pad = jnp.pad(moe_router_b, ((0, 0), (0, LANES - n_e))).reshape(depth, 1, LANES)

    for i in range(depth):
        kind, j = i % N_MIXERS, i // N_MIXERS
        cv = jnp.pad(mod[i, b], pad_rows[1:])
        lv = jnp.pad(mod[i, :b], pad_rows)
        lay = dict(b=b, l_tot=l_tot, lc=lc)
        h = _norm_mod(xs, norm_mix[i], cv, lv, l_tot=l_tot, lc=lc, r_shift=0, r_scale=1)
        if kind == 0:
            xs = _retention_layer(xs, h, cv, lv, ret_w_in[j], ret_decay[j], ret_gn[j], ret_w_out[j],
                                  ret_cos, ret_sin, **lay)
        elif kind == 1:
            xs = _mla_layer(xs, h, cv, lv, mla_w_in[j], mla_q_a_norm[j], mla_kv_a_norm[j], mla_w_q_b[j],
                            mla_w_kv_b[j], mla_q_norm[j], mla_k_norm[j], mla_w_out[j], tabs, **lay)
        elif kind == 2:
            xs = _shortconv_layer(xs, h, cv, lv, sc_w_in[j], sc_conv_w[j], sc_conv_b[j], sc_w_out[j], **lay)
        else:
            xs = _gqa_layer(xs, h, cv, lv, gqa_w_in[j], gqa_b_in[j], gqa_q_norm[j], gqa_k_norm[j],
                            gqa_sink[j], gqa_w_out[j], gqa_b_out[j], tabs, **lay)
        xs = _moe_layer(xs, norm_ffn[i], cv, lv, router_w_pad, router_b_pad, moe_w_gu, moe_b_gu,
                        moe_w_down, moe_b_down, i, l_tot=l_tot, lc=lc, route_ctx=i < depth - 1)
    return xs.reshape(b, l_tot, d)[:, lc:, :]
```

```python
import functools

import jax
import jax.numpy as jnp
from jax import lax
from jax.experimental import pallas as pl
from jax.experimental.pallas import tpu as pltpu

F32 = jnp.float32
BF16 = jnp.bfloat16

EPS = 1e-6
ROPE_BASE = 10000.0
NEG_INF = -1e30
LOG2_E = 1.4426950408889634
GRID_W = 64

RET_HEADS = 8
RET_CHUNK = 256
MLA_HEADS = 16
MLA_NOPE = 128
MLA_ROPE = 64
MLA_V = 128
MLA_RANK = 512
GQA_HEADS = 32
GQA_KV_HEADS = 4
GQA_GROUP = GQA_HEADS // GQA_KV_HEADS
GQA_HEAD_DIM = 64
WINDOW = 128
N_EXPERTS = 32
TOP_K = 4
SWIGLU_LIMIT = 7.0
SWIGLU_ALPHA = 1.702
N_MIXERS = 4

LANES = 128
VMEM_LIMIT = 56 * 1024 * 1024
MOE_TM = 256
MOD_ROWS = 8
ISSUE_UNROLL = 4
ROW_TILE = 8


def _params(n_axes, vmem=VMEM_LIMIT):
    return pltpu.CompilerParams(dimension_semantics=("arbitrary",) * n_axes,
                                vmem_limit_bytes=vmem)


def _row_copy_params():
    return pltpu.CompilerParams(dimension_semantics=("arbitrary",), vmem_limit_bytes=VMEM_LIMIT,
                                disable_bounds_checks=True)


def _row_tile(l_tot, cap):
    best = 16
    for t in range(16, cap + 1, 16):
        if l_tot % t == 0:
            best = t
    return best


def _is_ctx_rows(i, tm, tpb, lc):
    rows = lax.rem(i, tpb) * tm + lax.broadcasted_iota(jnp.int32, (tm, 1), 0)
    return rows < lc


def _pick_mod(is_ctx, cv_ref, lv_ref, r):
    return jnp.where(is_ctx, cv_ref[r:r + 1, :], lv_ref[0, r:r + 1, :])


def _sigmoid(x):
    return 1.0 / (1.0 + jnp.exp(-x))


def _rope3(x, c, s1, s2):
    return x * c + pltpu.roll(x, LANES - 16, axis=1) * s1 + pltpu.roll(x, 16, axis=1) * s2


def _mod_body(c_ref, w_ref, b_ref, o_ref):
    c = c_ref[...]
    s = (c * _sigmoid(c)).astype(BF16)
    o_ref[0] = jnp.dot(s, w_ref[0].astype(BF16), preferred_element_type=F32) + b_ref[0]


def _modulation(cvec, mod_w, mod_b):
    depth, d, n = mod_w.shape
    tn = 2048
    return pl.pallas_call(
        _mod_body,
        grid=(depth, n // tn),
        in_specs=[pl.BlockSpec((MOD_ROWS, d), lambda l, j: (0, 0)),
                  pl.BlockSpec((1, d, tn), lambda l, j: (l, 0, j)),
                  pl.BlockSpec((1, 1, tn), lambda l, j: (l, 0, j))],
        out_specs=pl.BlockSpec((1, MOD_ROWS, tn), lambda l, j: (l, 0, j)),
        out_shape=jax.ShapeDtypeStruct((depth, MOD_ROWS, n), F32),
        compiler_params=_params(2),
        name="modulation",
    )(cvec, mod_w, mod_b.reshape(depth, 1, n))


def _norm_mod_body(x_ref, g_ref, cv_ref, lv_ref, o_ref, *, tm, tpb, lc, r_shift, r_scale):
    x = x_ref[...]
    y = x * lax.rsqrt(jnp.mean(x * x, axis=-1, keepdims=True) + EPS) * g_ref[...]
    is_ctx = _is_ctx_rows(pl.program_id(0), tm, tpb, lc)
    shift = _pick_mod(is_ctx, cv_ref, lv_ref, r_shift)
    scale = _pick_mod(is_ctx, cv_ref, lv_ref, r_scale)
    o_ref[...] = (y * (1.0 + scale) + shift).astype(BF16)


def _norm_mod(x, gain, cv, lv, *, l_tot, lc, r_shift, r_scale):
    m, d = x.shape
    tm = _row_tile(l_tot, 544)
    tpb = l_tot // tm
    body = functools.partial(_norm_mod_body, tm=tm, tpb=tpb, lc=lc, r_shift=r_shift, r_scale=r_scale)
    return pl.pallas_call(
        body,
        grid=(m // tm,),
        in_specs=[pl.BlockSpec((tm, d), lambda i: (i, 0)),
                  pl.BlockSpec((1, d), lambda i: (0, 0)),
                  pl.BlockSpec((MOD_ROWS, d), lambda i: (0, 0)),
                  pl.BlockSpec((1, MOD_ROWS, d), lambda i: (i // tpb, 0, 0))],
        out_specs=pl.BlockSpec((tm, d), lambda i: (i, 0)),
        out_shape=jax.ShapeDtypeStruct((m, d), BF16),
        compiler_params=_params(1),
        name="norm_mod",
    )(x, gain.reshape(1, d), cv, lv)


def _mm_body(*refs, n_extra, n_out, epilogue, prologue):
    a_ref, w_ref = refs[0], refs[1]
    extra = refs[2:2 + n_extra]
    outs = refs[2 + n_extra:2 + n_extra + n_out]
    scratch = refs[2 + n_extra + n_out:]
    a = a_ref[...] if prologue is None else prologue(a_ref, extra, scratch)
    acc = jnp.dot(a, w_ref[...].astype(BF16), preferred_element_type=F32)
    epilogue(acc, extra, outs)


def _matmul(a, w, extras, extra_specs, out_shapes, out_specs, epilogue, *, tm, tn, name,
            prologue=None, scratch_shapes=()):
    m, k = a.shape
    n = w.shape[1]
    body = functools.partial(_mm_body, n_extra=len(extras), n_out=len(out_shapes),
                             epilogue=epilogue, prologue=prologue)
    return pl.pallas_call(
        body,
        grid=(m // tm, n // tn),
        in_specs=[pl.BlockSpec((tm, k), lambda i, j: (i, 0)),
                  pl.BlockSpec((k, tn), lambda i, j: (0, j))] + list(extra_specs),
        out_specs=list(out_specs),
        out_shape=list(out_shapes),
        scratch_shapes=list(scratch_shapes),
        compiler_params=_params(2),
        name=name,
    )(a, w, *extras)


def _plain_epilogue(out_dtype):
    def epilogue(acc, extra, outs):
        outs[0][...] = acc.astype(out_dtype)
    return epilogue


def _bias_epilogue(out_dtype):
    def epilogue(acc, extra, outs):
        outs[0][...] = (acc + extra[0][...]).astype(out_dtype)
    return epilogue


def _residual_matmul(a, w, bias, x_res, cv, lv, *, l_tot, lc, r_gate, tm, tn, name,
                     prologue=None, pro_extras=(), pro_specs=(), scratch_shapes=()):
    m = a.shape[0]
    n = w.shape[1]
    tpb = l_tot // tm
    n_pro = len(pro_extras)

    def epilogue(acc, extra, outs):
        b_ref, x_ref, cv_ref, lv_ref = extra[n_pro:n_pro + 4]
        is_ctx = _is_ctx_rows(pl.program_id(0), tm, tpb, lc)
        gate = _pick_mod(is_ctx, cv_ref, lv_ref, r_gate)
        outs[0][...] = x_ref[...] + gate * (acc + b_ref[...])

    extras = list(pro_extras) + [bias.reshape(1, n), x_res, cv, lv]
    specs = list(pro_specs) + [
        pl.BlockSpec((1, tn), lambda i, j: (0, j)),
        pl.BlockSpec((tm, tn), lambda i, j: (i, j)),
        pl.BlockSpec((MOD_ROWS, tn), lambda i, j: (0, j)),
        pl.BlockSpec((1, MOD_ROWS, tn), lambda i, j: (i // tpb, 0, j))]
    return _matmul(a, w, extras, specs,
                   [jax.ShapeDtypeStruct((m, n), F32)],
                   [pl.BlockSpec((tm, tn), lambda i, j: (i, j))],
                   epilogue, tm=tm, tn=tn, name=name, prologue=prologue,
                   scratch_shapes=scratch_shapes)[0]


def _ret_in_proj(h, w_in, cos, sin, *, l_tot, d_model):
    m = h.shape[0]
    n = w_in.shape[1]
    tm = _row_tile(l_tot, 2176)
    tn = 512
    tpb = l_tot // tm
    dk = d_model // RET_HEADS
    half = dk // 2
    n_q_tiles = d_model // tn
    k_scale = dk ** -0.5

    def epilogue(acc, extra, outs):
        cos_ref, sin_ref = extra
        j = pl.program_id(1)

        @pl.when(j < 2 * n_q_tiles)
        def _():
            c = cos_ref[...]
            s = sin_ref[...]
            mult = jnp.where(j >= n_q_tiles, k_scale, 1.0)
            for hh in range(tn // dk):
                x1 = acc[:, hh * dk:hh * dk + half]
                x2 = acc[:, hh * dk + half:(hh + 1) * dk]
                outs[0][:, hh * dk:hh * dk + half] = ((x1 * c - x2 * s) * mult).astype(BF16)
                outs[0][:, hh * dk + half:(hh + 1) * dk] = ((x2 * c + x1 * s) * mult).astype(BF16)

        @pl.when(j >= 2 * n_q_tiles)
        def _():
            outs[0][...] = acc.astype(BF16)

    tab_spec = pl.BlockSpec((tm, half), lambda i, j: (lax.rem(i, tpb), 0))
    return _matmul(h, w_in, [cos, sin], [tab_spec, tab_spec],
                   [jax.ShapeDtypeStruct((m, n), BF16)],
                   [pl.BlockSpec((tm, tn), lambda i, j: (i, j))],
                   epilogue, tm=tm, tn=tn, name="ret_in_proj")[0]


def _ret_body(lg_ref, q_ref, k_ref, v_ref, g_ref, gn_ref, o_ref, oacc_ref, sf_ref, sb_ref, *, nc_ctx, nc_tot):
    h = pl.program_id(1)
    ch = RET_CHUNK
    dv = v_ref.shape[-1]
    ii = lax.broadcasted_iota(jnp.int32, (ch, ch), 0).astype(F32)
    jj = lax.broadcasted_iota(jnp.int32, (ch, ch), 1).astype(F32)
    ri = lax.broadcasted_iota(jnp.int32, (ch, 1), 0).astype(F32)
    consts = []
    for direction in (0, 1):
        lg = lg_ref[direction, h]
        if direction == 0:
            diff = ii - jj
            q_decay = jnp.exp((ri + 1.0) * lg)
            k_decay = jnp.exp((ch - 1.0 - ri) * lg)
        else:
            diff = jj - ii
            q_decay = jnp.exp((ch - ri) * lg)
            k_decay = jnp.exp(ri * lg)
        intra = jnp.where(diff >= 0, jnp.exp(jnp.maximum(diff, 0.0) * lg), 0.0)
        chunk_decay = jnp.exp(jnp.full((1, dv), float(ch), F32) * lg)
        consts.append((intra, q_decay, k_decay, chunk_decay))
    sf_ref[...] = jnp.zeros_like(sf_ref)
    sb_ref[...] = jnp.zeros_like(sb_ref)
    oacc_ref[...] = jnp.zeros_like(oacc_ref)

    def chunk_rows(c):
        return pl.ds(pl.multiple_of(c * ch, ch), ch)

    def advance(c, s_ref, intra, q_decay, k_decay, chunk_decay):
        rows = chunk_rows(c)
        qc = q_ref[0, rows, :]
        kc = k_ref[0, rows, :]
        vc = v_ref[0, rows, :]
        scores = lax.dot_general(qc, kc, (((1,), (1,)), ((), ())), preferred_element_type=F32) * intra
        state = s_ref[...]
        out = (jnp.dot(scores.astype(BF16), vc, preferred_element_type=F32)
               + jnp.dot((qc.astype(F32) * q_decay).astype(BF16), state.astype(BF16),
                         preferred_element_type=F32))
        kt = (kc.astype(F32) * k_decay).T.astype(BF16)
        s_ref[...] = state * chunk_decay + jnp.dot(kt, vc, preferred_element_type=F32)
        return out

    def step(t, carry):
        c_fwd = t
        c_bwd = jnp.where(t < nc_ctx, nc_ctx - 1 - t, nc_tot - 1 - (t - nc_ctx))
        out_f = advance(c_fwd, sf_ref, *consts[0])
        out_b = advance(c_bwd, sb_ref, *consts[1])
        oacc_ref[chunk_rows(c_fwd), :] += out_f
        oacc_ref[chunk_rows(c_bwd), :] += out_b
        return carry

    lax.fori_loop(0, nc_tot, step, 0)

    def finish(c, carry):
        rows = chunk_rows(c)
        o = oacc_ref[rows, :]
        o = o * lax.rsqrt(jnp.mean(o * o, axis=-1, keepdims=True) + EPS) * gn_ref[...]
        g = g_ref[0, rows, :].astype(F32)
        o_ref[0, rows, :] = ((g * _sigmoid(g)) * o).astype(BF16)
        return carry

    lax.fori_loop(0, nc_tot, finish, 0)


def _retention(proj, log_gamma, gn, *, b, l_tot, lc, d_model):
    dk = d_model // RET_HEADS
    dv = 2 * d_model // RET_HEADS
    proj3 = proj.reshape(b, l_tot, proj.shape[-1])
    k_blk0 = d_model // dk
    v_blk0 = 2 * d_model // dv
    g_blk0 = 4 * d_model // dv
    body = functools.partial(_ret_body, nc_ctx=lc // RET_CHUNK, nc_tot=l_tot // RET_CHUNK)
    return pl.pallas_call(
        body,
        grid=(b, RET_HEADS),
        in_specs=[pl.BlockSpec(memory_space=pltpu.SMEM),
                  pl.BlockSpec((1, l_tot, dk), lambda bi, hi: (bi, 0, hi)),
                  pl.BlockSpec((1, l_tot, dk), lambda bi, hi: (bi, 0, k_blk0 + hi)),
                  pl.BlockSpec((1, l_tot, dv), lambda bi, hi: (bi, 0, v_blk0 + hi)),
                  pl.BlockSpec((1, l_tot, dv), lambda bi, hi: (bi, 0, g_blk0 + hi)),
                  pl.BlockSpec((1, dv), lambda bi, hi: (0, hi))],
        out_specs=pl.BlockSpec((1, l_tot, dv), lambda bi, hi: (bi, 0, hi)),
        out_shape=jax.ShapeDtypeStruct((b, l_tot, RET_HEADS * dv), BF16),
        scratch_shapes=[pltpu.VMEM((l_tot, dv), F32), pltpu.VMEM((dk, dv), F32), pltpu.VMEM((dk, dv), F32)],
        compiler_params=_params(2),
        name="retention",
    )(log_gamma, proj3, proj3, proj3, proj3, gn.reshape(1, RET_HEADS * dv))


def _retention_layer(x, h, cv, lv, w_in, decay, gn, w_out, cos, sin, *, b, l_tot, lc):
    m, d = x.shape
    proj = _ret_in_proj(h, w_in, cos, sin, l_tot=l_tot, d_model=d)
    log_gamma = jax.nn.log_sigmoid(decay.astype(F32))
    o = _retention(proj, log_gamma, gn, b=b, l_tot=l_tot, lc=lc, d_model=d).reshape(m, 2 * d)
    return _residual_matmul(o, w_out, jnp.zeros((d,), F32), x, cv, lv, l_tot=l_tot, lc=lc, r_gate=2,
                            tm=_row_tile(l_tot, 1088), tn=512, name="ret_out_proj")


def _mla_in_proj(h, w_in_pad, qa_gain, kva_gain, *, l_tot):
    m = h.shape[0]
    n = w_in_pad.shape[1]
    tm = _row_tile(l_tot, 1088)
    r = MLA_RANK

    def epilogue(acc, extra, outs):
        qg_ref, kg_ref = extra
        ql = acc[:, :r]
        kl = acc[:, r:2 * r]
        outs[0][...] = (ql * lax.rsqrt(jnp.mean(ql * ql, axis=-1, keepdims=True) + EPS)
                        * qg_ref[...]).astype(BF16)
        outs[1][...] = (kl * lax.rsqrt(jnp.mean(kl * kl, axis=-1, keepdims=True) + EPS)
                        * kg_ref[...]).astype(BF16)
        outs[2][...] = acc[:, 2 * r:]

    gspec = pl.BlockSpec((1, r), lambda i, j: (0, 0))
    return _matmul(h, w_in_pad, [qa_gain.reshape(1, r), kva_gain.reshape(1, r)], [gspec, gspec],
                   [jax.ShapeDtypeStruct((m, r), BF16), jax.ShapeDtypeStruct((m, r), BF16),
                    jax.ShapeDtypeStruct((m, LANES), F32)],
                   [pl.BlockSpec((tm, r), lambda i, j: (i, 0)),
                    pl.BlockSpec((tm, r), lambda i, j: (i, 0)),
                    pl.BlockSpec((tm, LANES), lambda i, j: (i, 0))],
                   epilogue, tm=tm, tn=n, name="mla_in_proj")


def _mla_q_proj(qn, wq_pad, q_gain_pad, tabs, *, l_tot):
    m = qn.shape[0]
    n = wq_pad.shape[1]
    tm = _row_tile(l_tot, 1088)
    tn = 1024
    tpb = l_tot // tm
    hw = 2 * LANES
    width = MLA_NOPE + MLA_ROPE
    scale = width ** -0.5 * LOG2_E

    def epilogue(acc, extra, outs):
        g_ref, c_ref, s1_ref, s2_ref = extra
        c, s1, s2 = c_ref[...], s1_ref[...], s2_ref[...]
        for hh in range(tn // hw):
            xs = acc[:, hh * hw:(hh + 1) * hw]
            inv = lax.rsqrt(jnp.sum(xs * xs, axis=-1, keepdims=True) * (1.0 / width) + EPS)
            y = xs * inv * g_ref[...] * scale
            outs[0][:, hh * hw:hh * hw + LANES] = y[:, :LANES].astype(BF16)
            outs[0][:, hh * hw + LANES:(hh + 1) * hw] = _rope3(y[:, LANES:], c, s1, s2).astype(BF16)

    tspec = pl.BlockSpec((tm, LANES), lambda i, j: (lax.rem(i, tpb), 0))
    return _matmul(qn, wq_pad, [q_gain_pad] + list(tabs),
                   [pl.BlockSpec((1, hw), lambda i, j: (0, 0)), tspec, tspec, tspec],
                   [jax.ShapeDtypeStruct((m, n), BF16)],
                   [pl.BlockSpec((tm, tn), lambda i, j: (i, j))],
                   epilogue, tm=tm, tn=tn, name="mla_q_proj")[0]


def _mla_kv_proj(kvn, w_kv, kr, k_gain_pad, tabs, *, l_tot):
    m = kvn.shape[0]
    n = w_kv.shape[1]
    tm = _row_tile(l_tot, 1088)
    tn = 1024
    tpb = l_tot // tm
    hw = MLA_NOPE + MLA_V
    width = MLA_NOPE + MLA_ROPE

    def epilogue(acc, extra, outs):
        kr_ref, g_ref, c_ref, s1_ref, s2_ref = extra
        kr_raw = kr_ref[...]
        ss_r = jnp.sum(kr_raw * kr_raw, axis=-1, keepdims=True)
        kr_rot = _rope3(kr_raw * g_ref[:, LANES:], c_ref[...], s1_ref[...], s2_ref[...])
        g_nope = g_ref[:, :LANES]
        for hh in range(tn // hw):
            kn = acc[:, hh * hw:hh * hw + MLA_NOPE]
            inv = lax.rsqrt((jnp.sum(kn * kn, axis=-1, keepdims=True) + ss_r) * (1.0 / width) + EPS)
            outs[0][:, hh * hw:hh * hw + LANES] = (kn * inv * g_nope).astype(BF16)
            outs[0][:, hh * hw + LANES:(hh + 1) * hw] = (kr_rot * inv).astype(BF16)
            outs[1][:, hh * MLA_V:(hh + 1) * MLA_V] = acc[:, hh * hw + MLA_NOPE:(hh + 1) * hw].astype(BF16)

    tspec = pl.BlockSpec((tm, LANES), lambda i, j: (lax.rem(i, tpb), 0))
    return _matmul(kvn, w_kv, [kr, k_gain_pad] + list(tabs),
                   [pl.BlockSpec((tm, LANES), lambda i, j: (i, 0)),
                    pl.BlockSpec((1, 2 * LANES), lambda i, j: (0, 0)), tspec, tspec, tspec],
                   [jax.ShapeDtypeStruct((m, n), BF16), jax.ShapeDtypeStruct((m, n // 2), BF16)],
                   [pl.BlockSpec((tm, tn), lambda i, j: (i, j)),
                    pl.BlockSpec((tm, tn // 2), lambda i, j: (i, j))],
                   epilogue, tm=tm, tn=tn, name="mla_kv_proj")


def _softmax_pv(q, k, v):
    s = lax.dot_general(q, k, (((1,), (1,)), ((), ())), preferred_element_type=F32)
    p = jnp.exp2(s - jnp.max(s, axis=-1, keepdims=True))
    den = jnp.sum(p, axis=-1, keepdims=True)
    return jnp.dot(p.astype(BF16), v, preferred_element_type=F32) / den


def _mla_attn_body(q_ref, k_ref, v_ref, o_ref, *, lc, tq, n_q):
    o_ref[0, 0:lc, :] = _softmax_pv(q_ref[0, 0:lc, :], k_ref[0, 0:lc, :], v_ref[0, 0:lc, :]).astype(BF16)

    def step(t, carry):
        rows = pl.ds(pl.multiple_of(lc + t * tq, LANES), tq)
        o_ref[0, rows, :] = _softmax_pv(q_ref[0, rows, :], k_ref[0], v_ref[0]).astype(BF16)
        return carry

    lax.fori_loop(0, n_q, step, 0, unroll=4)


def _mla_attention(q, k, v, *, b, l_tot, lc):
    hw = 2 * LANES
    tq = 256
    assert lc % LANES == 0 and (l_tot - lc) % tq == 0
    q3 = q.reshape(b, l_tot, MLA_HEADS * hw)
    k3 = k.reshape(b, l_tot, MLA_HEADS * hw)
    v3 = v.reshape(b, l_tot, MLA_HEADS * MLA_V)
    body = functools.partial(_mla_attn_body, lc=lc, tq=tq, n_q=(l_tot - lc) // tq)
    return pl.pallas_call(
        body,
        grid=(b, MLA_HEADS),
        in_specs=[pl.BlockSpec((1, l_tot, hw), lambda bi, hi: (bi, 0, hi)),
                  pl.BlockSpec((1, l_tot, hw), lambda bi, hi: (bi, 0, hi)),
                  pl.BlockSpec((1, l_tot, MLA_V), lambda bi, hi: (bi, 0, hi))],
        out_specs=pl.BlockSpec((1, l_tot, MLA_V), lambda bi, hi: (bi, 0, hi)),
        out_shape=jax.ShapeDtypeStruct((b, l_tot, MLA_HEADS * MLA_V), BF16),
        compiler_params=_params(2),
        name="mla_attention",
    )(q3, k3, v3)


def _mla_layer(x, h, cv, lv, w_in, qa_gain, kva_gain, w_q_b, w_kv_b, q_gain, k_gain, w_out, tabs,
               *, b, l_tot, lc):
    m, d = x.shape
    width = MLA_NOPE + MLA_ROPE
    hw = 2 * LANES
    w_in_pad = jnp.pad(w_in, ((0, 0), (0, 2 * MLA_RANK + LANES - w_in.shape[1])))
    wq_pad = jnp.pad(w_q_b.reshape(MLA_RANK, MLA_HEADS, width),
                     ((0, 0), (0, 0), (0, hw - width))).reshape(MLA_RANK, MLA_HEADS * hw)
    q_gain_pad = jnp.pad(q_gain, (0, hw - width)).reshape(1, hw)
    k_gain_pad = jnp.pad(k_gain, (0, hw - width)).reshape(1, hw)
    qn, kvn, kr = _mla_in_proj(h, w_in_pad, qa_gain, kva_gain, l_tot=l_tot)
    q = _mla_q_proj(qn, wq_pad, q_gain_pad, tabs, l_tot=l_tot)
    k, v = _mla_kv_proj(kvn, w_kv_b, kr, k_gain_pad, tabs, l_tot=l_tot)
    o = _mla_attention(q, k, v, b=b, l_tot=l_tot, lc=lc).reshape(m, MLA_HEADS * MLA_V)
    return _residual_matmul(o, w_out, jnp.zeros((d,), F32), x, cv, lv, l_tot=l_tot, lc=lc,
                            r_gate=2, tm=_row_tile(l_tot, 2176), tn=512, name="mla_out_proj")


def _conv_gate_body(bg_ref, cg_ref, u_ref, w_ref, cb_ref, o_ref, z_ref, *, l_tot, lc):
    pad = 8
    tn = o_ref.shape[-1]
    z_ref[0:pad, :] = jnp.zeros((pad, tn), F32)
    z_ref[pad + l_tot:, :] = jnp.zeros((pad, tn), F32)
    z_ref[pad:pad + l_tot, :] = cg_ref[0].astype(F32) * u_ref[0].astype(F32)
    rows = lax.broadcasted_iota(jnp.int32, (l_tot, 1), 0)
    z_prev = jnp.where(rows == lc, 0.0, z_ref[pad - 1:pad - 1 + l_tot, :])
    z_next = jnp.where(rows == lc - 1, 0.0, z_ref[pad + 1:pad + 1 + l_tot, :])
    z_mid = z_ref[pad:pad + l_tot, :]
    conv = z_prev * w_ref[0:1, :] + z_mid * w_ref[1:2, :] + z_next * w_ref[2:3, :] + cb_ref[...]
    o_ref[0] = (bg_ref[0].astype(F32) * conv).astype(BF16)


def _conv_gate(proj, conv_w, conv_b, *, b, l_tot, lc, d):
    tn = 256
    nb = d // tn
    proj3 = proj.reshape(b, l_tot, 3 * d)
    body = functools.partial(_conv_gate_body, l_tot=l_tot, lc=lc)
    return pl.pallas_call(
        body,
        grid=(b, nb),
        in_specs=[pl.BlockSpec((1, l_tot, tn), lambda bi, j: (bi, 0, j)),
                  pl.BlockSpec((1, l_tot, tn), lambda bi, j: (bi, 0, nb + j)),
                  pl.BlockSpec((1, l_tot, tn), lambda bi, j: (bi, 0, 2 * nb + j)),
                  pl.BlockSpec((3, tn), lambda bi, j: (0, j)),
                  pl.BlockSpec((1, tn), lambda bi, j: (0, j))],
        out_specs=pl.BlockSpec((1, l_tot, tn), lambda bi, j: (bi, 0, j)),
        out_shape=jax.ShapeDtypeStruct((b, l_tot, d), BF16),
        scratch_shapes=[pltpu.VMEM((l_tot + 16, tn), F32)],
        compiler_params=_params(2),
        name="conv_gate",
    )(proj3, proj3, proj3, conv_w, conv_b.reshape(1, d))


def _shortconv_layer(x, h, cv, lv, w_in, conv_w, conv_b, w_out, *, b, l_tot, lc):
    m, d = x.shape
    tm = _row_tile(l_tot, 2176)
    proj = _matmul(h, w_in, [], [], [jax.ShapeDtypeStruct((m, 3 * d), BF16)],
                   [pl.BlockSpec((tm, 512), lambda i, j: (i, j))],
                   _plain_epilogue(BF16), tm=tm, tn=512, name="sc_in_proj")[0]
    a = _conv_gate(proj, conv_w, conv_b, b=b, l_tot=l_tot, lc=lc, d=d).reshape(m, d)
    return _residual_matmul(a, w_out, jnp.zeros((d,), F32), x, cv, lv, l_tot=l_tot, lc=lc,
                            r_gate=2, tm=tm, tn=512, name="sc_out_proj")


def _split_hi_lo(x):
    hi = x.astype(BF16)
    lo = (x - hi.astype(F32)).astype(BF16)
    return hi, lo


def _head_rms_scale(x, seg_ref, segt_ref):
    hi, lo = _split_hi_lo(x * x)
    seg = seg_ref[...]
    ss = jnp.dot(hi, seg, preferred_element_type=F32) + jnp.dot(lo, seg, preferred_element_type=F32)
    inv = lax.rsqrt(ss * (1.0 / GQA_HEAD_DIM) + EPS)
    ihi, ilo = _split_hi_lo(inv)
    segt = segt_ref[...]
    return jnp.dot(ihi, segt, preferred_element_type=F32) + jnp.dot(ilo, segt, preferred_element_type=F32)


def _gqa_in_proj(h, w_in, b_in, qg_t, kg_t, seg, segt, tabs, *, l_tot):
    m = h.shape[0]
    n = w_in.shape[1]
    tm = _row_tile(l_tot, 1088)
    tn = 512
    tpb = l_tot // tm
    n_q_tiles = GQA_HEADS * GQA_HEAD_DIM // tn
    kv_w = GQA_KV_HEADS * GQA_HEAD_DIM
    assert n == n_q_tiles * tn + 2 * kv_w and 2 * kv_w == tn
    scale = GQA_HEAD_DIM ** -0.5 * LOG2_E

    def norm_rope(xs, g, extra, mult):
        seg_ref, segt_ref, c_ref, s1_ref, s2_ref = extra[3:8]
        y = xs * _head_rms_scale(xs, seg_ref, segt_ref) * g
        c, s1, s2 = c_ref[...], s1_ref[...], s2_ref[...]
        return [_rope3(y[:, a * LANES:(a + 1) * LANES], c, s1, s2) * mult
                for a in range(xs.shape[1] // LANES)]

    def dup_heads(slab, lo_mask):
        swapped = pltpu.roll(slab, GQA_HEAD_DIM, axis=1)
        return jnp.where(lo_mask, slab, swapped), jnp.where(lo_mask, swapped, slab)

    def epilogue(acc, extra, outs):
        b_ref, qg_ref, kg_ref = extra[0:3]
        j = pl.program_id(1)
        acc = acc + b_ref[...]

        @pl.when(j < n_q_tiles)
        def _():
            for a, slab in enumerate(norm_rope(acc, qg_ref[...], extra, scale)):
                outs[0][:, a * LANES:(a + 1) * LANES] = slab.astype(BF16)

        @pl.when(j == n_q_tiles)
        def _():
            lo_mask = lax.broadcasted_iota(jnp.int32, (tm, LANES), 1) < GQA_HEAD_DIM
            kpad = jnp.concatenate([acc[:, :kv_w], acc[:, :kv_w]], axis=1)
            k_slabs = norm_rope(kpad, kg_ref[...], extra, 1.0)[:kv_w // LANES]
            for a in range(kv_w // LANES):
                ka, kb = dup_heads(k_slabs[a], lo_mask)
                outs[1][:, (2 * a) * LANES:(2 * a + 1) * LANES] = ka.astype(BF16)
                outs[1][:, (2 * a + 1) * LANES:(2 * a + 2) * LANES] = kb.astype(BF16)
                va, vb = dup_heads(acc[:, kv_w + a * LANES:kv_w + (a + 1) * LANES], lo_mask)
                outs[2][:, (2 * a) * LANES:(2 * a + 1) * LANES] = va.astype(BF16)
                outs[2][:, (2 * a + 1) * LANES:(2 * a + 2) * LANES] = vb.astype(BF16)

    tspec = pl.BlockSpec((tm, LANES), lambda i, j: (lax.rem(i, tpb), 0))
    full = lambda shape: pl.BlockSpec(shape, lambda i, j: (0, 0))
    return _matmul(
        h, w_in, [b_in.reshape(1, n), qg_t, kg_t, seg, segt] + list(tabs),
        [pl.BlockSpec((1, tn), lambda i, j: (0, j)), full((1, tn)), full((1, tn)),
         full((tn, LANES)), full((LANES, tn)), tspec, tspec, tspec],
        [jax.ShapeDtypeStruct((m, n_q_tiles * tn), BF16),
         jax.ShapeDtypeStruct((m, GQA_KV_HEADS * LANES), BF16),
         jax.ShapeDtypeStruct((m, GQA_KV_HEADS * LANES), BF16)],
        [pl.BlockSpec((tm, tn), lambda i, j: (i, jnp.minimum(j, n_q_tiles - 1))),
         pl.BlockSpec((tm, GQA_KV_HEADS * LANES), lambda i, j: (i, 0)),
         pl.BlockSpec((tm, GQA_KV_HEADS * LANES), lambda i, j: (i, 0))],
        epilogue, tm=tm, tn=tn, name="gqa_in_proj")


def _gqa_attn_body(sink_ref, q_ref, k_ref, v_ref, o_ref, *, lc, nb):
    g = pl.program_id(1)
    r = WINDOW
    n_slab = GQA_GROUP // 2
    rows_all = GQA_GROUP * r
    row_head = lax.broadcasted_iota(jnp.int32, (rows_all, 1), 0) // r
    sink = jnp.zeros((rows_all, 1), F32)
    for hh in range(GQA_GROUP):
        sink = jnp.where(row_head == hh, sink_ref[g * GQA_GROUP + hh] * LOG2_E, sink)
    lo_mask = lax.broadcasted_iota(jnp.int32, (r, LANES), 1) < GQA_HEAD_DIM
    ii = lax.rem(lax.broadcasted_iota(jnp.int32, (rows_all, r), 0), r)
    jj = lax.broadcasted_iota(jnp.int32, (rows_all, r), 1)
    k_ctx = k_ref[0, 0:lc, :]
    v_ctx = v_ref[0, 0:lc, :]

    def stack_q(rows):
        parts = []
        for a in range(n_slab):
            qa = q_ref[0, rows, a * LANES:(a + 1) * LANES]
            zero = jnp.zeros_like(qa)
            parts += [jnp.where(lo_mask, qa, zero), jnp.where(lo_mask, zero, qa)]
        return jnp.concatenate(parts, axis=0)

    def attend(rows, blocks):
        qs = stack_q(rows)
        scores = []
        for kb, _, mask in blocks:
            s = lax.dot_general(qs, kb, (((1,), (1,)), ((), ())), preferred_element_type=F32)
            scores.append(s if mask is None else jnp.where(mask, s, NEG_INF))
        slabs = lambda a: [a[:, c * LANES:(c + 1) * LANES] for c in range(a.shape[1] // LANES)]
        mx = functools.reduce(jnp.maximum, [sl for s in scores for sl in slabs(s)])
        mx = jnp.maximum(sink, jnp.max(mx, axis=-1, keepdims=True))
        psum = jnp.zeros((rows_all, LANES), F32)
        acc = jnp.zeros((rows_all, LANES), F32)
        for s, (_, vb, _) in zip(scores, blocks):
            p = jnp.exp2(s - mx)
            psum = functools.reduce(jnp.add, slabs(p), psum)
            acc = acc + jnp.dot(p.astype(BF16), vb, preferred_element_type=F32)
        o = acc / (jnp.exp2(sink - mx) + jnp.sum(psum, axis=-1, keepdims=True))
        for a in range(n_slab):
            oa = jnp.where(lo_mask, o[(2 * a) * r:(2 * a + 1) * r], o[(2 * a + 1) * r:(2 * a + 2) * r])
            o_ref[0, rows, a * LANES:(a + 1) * LANES] = oa.astype(BF16)

    for cb in range(lc // r):
        attend(pl.ds(cb * r, r), [(k_ctx, v_ctx, None)])

    def step(t, carry):
        r_cur = pl.multiple_of(lc + t * r, r)
        r_prev = pl.multiple_of(lc + jnp.maximum(t - 1, 0) * r, r)
        r_next = pl.multiple_of(lc + jnp.minimum(t + 1, nb - 1) * r, r)
        blk = lambda r0: (k_ref[0, pl.ds(r0, r), :], v_ref[0, pl.ds(r0, r), :])
        m_prev = jnp.logical_and(jj >= ii, t > 0)
        m_next = jnp.logical_and(jj <= ii, t < nb - 1)
        attend(pl.ds(r_cur, r), [blk(r_prev) + (m_prev,), blk(r_cur) + (None,),
                                 blk(r_next) + (m_next,), (k_ctx, v_ctx, None)])
        return carry

    lax.fori_loop(0, nb, step, 0, unroll=2)


def _gqa_attention(q, k, v, sink, *, b, l_tot, lc):
    gw = GQA_GROUP * GQA_HEAD_DIM
    q3 = q.reshape(b, l_tot, GQA_HEADS * GQA_HEAD_DIM)
    k3 = k.reshape(b, l_tot, GQA_KV_HEADS * LANES)
    v3 = v.reshape(b, l_tot, GQA_KV_HEADS * LANES)
    body = functools.partial(_gqa_attn_body, lc=lc, nb=(l_tot - lc) // WINDOW)
    return pl.pallas_call(
        body,
        grid=(b, GQA_KV_HEADS),
        in_specs=[pl.BlockSpec(memory_space=pltpu.SMEM),
                  pl.BlockSpec((1, l_tot, gw), lambda bi, gi: (bi, 0, gi)),
                  pl.BlockSpec((1, l_tot, LANES), lambda bi, gi: (bi, 0, gi)),
                  pl.BlockSpec((1, l_tot, LANES), lambda bi, gi: (bi, 0, gi))],
        out_specs=pl.BlockSpec((1, l_tot, gw), lambda bi, gi: (bi, 0, gi)),
        out_shape=jax.ShapeDtypeStruct((b, l_tot, GQA_HEADS * GQA_HEAD_DIM), BF16),
        compiler_params=_params(2),
        name="gqa_attention",
    )(sink, q3, k3, v3)


def _gqa_layer(x, h, cv, lv, w_in, b_in, q_gain, k_gain, sink, w_out, b_out, tabs, *, b, l_tot, lc):
    m, d = x.shape
    tn = 512
    heads_per_tile = tn // GQA_HEAD_DIM
    col = jnp.arange(tn)
    seg = (col[:, None] // GQA_HEAD_DIM == jnp.arange(LANES)[None, :]).astype(BF16)
    qg_t = jnp.tile(q_gain, heads_per_tile).reshape(1, tn)
    kg_t = jnp.tile(k_gain, heads_per_tile).reshape(1, tn)
    q, k, v = _gqa_in_proj(h, w_in, b_in, qg_t, kg_t, seg, seg.T, tabs, l_tot=l_tot)
    o = _gqa_attention(q, k, v, sink.astype(F32), b=b, l_tot=l_tot, lc=lc).reshape(m, d)
    return _residual_matmul(o, w_out, b_out, x, cv, lv, l_tot=l_tot, lc=lc, r_gate=2,
                            tm=_row_tile(l_tot, 2176), tn=512, name="gqa_out_proj")


def _pack_halves(yb):
    w = yb.shape[1] // 2
    lo = lax.bitcast_convert_type(yb[:, :w].astype(F32), jnp.uint32)
    hi = lax.bitcast_convert_type(yb[:, w:].astype(F32), jnp.uint32)
    return lax.shift_right_logical(lo, jnp.uint32(16)) | hi


def _store_row_tiles(ref, words):
    n = words.shape[0]
    for s in range(ROW_TILE):
        ref[pl.ds(s, n, stride=ROW_TILE), :] = words[:, s * LANES:(s + 1) * LANES]


def _load_row_tiles(ref, n):
    return jnp.concatenate([ref[pl.ds(s, n, stride=ROW_TILE), :] for s in range(ROW_TILE)], axis=1)


def _tile_of_row(r):
    return pl.ds(pl.multiple_of(r * ROW_TILE, ROW_TILE), ROW_TILE)


def _unpack_halves_f32(words):
    lo = lax.bitcast_convert_type(lax.shift_left(words, jnp.uint32(16)), F32)
    hi = lax.bitcast_convert_type(words & jnp.uint32(0xFFFF0000), F32)
    return lo, hi


def _unpack_halves(words):
    lo, hi = _unpack_halves_f32(words)
    return lo.astype(BF16), hi.astype(BF16)


def _ffn_route_body(x_ref, g_ref, cv_ref, lv_ref, rw_ref, rb_ref, fpk_ref, ri_ref, rg_ref, cnt_ref,
                    carry_ref, *, tm, tpb, lc, route_ctx, n_e):
    i = pl.program_id(0)

    @pl.when(i == 0)
    def _():
        carry_ref[...] = jnp.zeros_like(carry_ref)

    x = x_ref[...]
    y = x * lax.rsqrt(jnp.mean(x * x, axis=-1, keepdims=True) + EPS) * g_ref[...]
    is_ctx = _is_ctx_rows(i, tm, tpb, lc)
    y = y * (1.0 + _pick_mod(is_ctx, cv_ref, lv_ref, 4)) + _pick_mod(is_ctx, cv_ref, lv_ref, 3)
    yb = y.astype(BF16)
    _store_row_tiles(fpk_ref, _pack_halves(yb))

    lane = lax.broadcasted_iota(jnp.int32, (tm, LANES), 1)
    logits = jnp.dot(yb, rw_ref[...].astype(BF16), preferred_element_type=F32) + rb_ref[...]
    logits = jnp.where(lane < n_e, logits, NEG_INF)
    vals, idxs, hots = [], [], []
    for _ in range(TOP_K):
        mk = jnp.max(logits, axis=-1, keepdims=True)
        ik = jnp.min(jnp.where(logits == mk, lane, LANES), axis=-1, keepdims=True)
        hot = lane == ik
        vals.append(mk)
        idxs.append(ik)
        hots.append(hot)
        logits = jnp.where(hot, NEG_INF, logits)
    exps = [jnp.exp(v - vals[0]) for v in vals]
    den = exps[0] + exps[1] + exps[2] + exps[3]

    routed = jnp.ones_like(is_ctx) if route_ctx else jnp.logical_not(is_ctx)
    chosen = jnp.zeros((tm, LANES), F32)
    for hot in hots:
        chosen = chosen + jnp.where(jnp.logical_and(hot, routed), 1.0, 0.0)
    before = (lax.broadcasted_iota(jnp.int32, (tm, tm), 0)
              > lax.broadcasted_iota(jnp.int32, (tm, tm), 1)).astype(BF16)
    arrived = jnp.dot(before, chosen.astype(BF16), preferred_element_type=F32) + carry_ref[...]
    carry_ref[...] = carry_ref[...] + jnp.sum(chosen, axis=0, keepdims=True)
    cnt_ref[...] = carry_ref[...]

    ri = jnp.zeros((tm, LANES), jnp.int32)
    rg = jnp.zeros((tm, LANES), F32)
    for k in range(TOP_K):
        rank = jnp.sum(jnp.where(hots[k], arrived, 0.0), axis=-1, keepdims=True)
        ri = jnp.where(lane == k, idxs[k], ri)
        ri = jnp.where(lane == TOP_K + k, rank.astype(jnp.int32), ri)
        rg = jnp.where(lane == k, exps[k] / den, rg)
    ri_ref[...] = ri
    rg_ref[...] = rg


def _ffn_route(x, gain, cv, lv, router_w_pad, router_b_pad, layer, *, l_tot, lc, route_ctx, n_e):
    m, d = x.shape
    tm = _row_tile(l_tot, 544)
    tpb = l_tot // tm
    body = functools.partial(_ffn_route_body, tm=tm, tpb=tpb, lc=lc, route_ctx=route_ctx, n_e=n_e)
    return pl.pallas_call(
        body,
        grid=(m // tm,),
        in_specs=[pl.BlockSpec((tm, d), lambda i: (i, 0)),
                  pl.BlockSpec((1, d), lambda i: (0, 0)),
                  pl.BlockSpec((MOD_ROWS, d), lambda i: (0, 0)),
                  pl.BlockSpec((1, MOD_ROWS, d), lambda i: (i // tpb, 0, 0)),
                  pl.BlockSpec((None, d, LANES), lambda i: (layer, 0, 0)),
                  pl.BlockSpec((None, 1, LANES), lambda i: (layer, 0, 0))],
        out_specs=[pl.BlockSpec((tm * ROW_TILE, LANES), lambda i: (i, 0)),
                   pl.BlockSpec((tm, LANES), lambda i: (i, 0)),
                   pl.BlockSpec((tm, LANES), lambda i: (i, 0)),
                   pl.BlockSpec((1, LANES), lambda i: (0, 0))],
        out_shape=[jax.ShapeDtypeStruct((m * ROW_TILE, LANES), jnp.uint32),
                   jax.ShapeDtypeStruct((m, LANES), jnp.int32),
                   jax.ShapeDtypeStruct((m, LANES), F32),
                   jax.ShapeDtypeStruct((1, LANES), F32)],
        scratch_shapes=[pltpu.VMEM((1, LANES), F32)],
        compiler_params=_params(1),
        name="ffn_route",
    )(x, gain.reshape(1, d), cv, lv, router_w_pad, router_b_pad)


def _dispatch_body(dest_ref, zb_ref, nz_ref, f_ref, xr_hbm, sem, zbuf, zsem, *, tm, tpb, ctx_tiles, route_ctx):
    i = pl.program_id(0)

    @pl.when(i == 0)
    def _():
        zbuf[...] = jnp.zeros_like(zbuf)
        blk_rows = MOE_TM * ROW_TILE

        def zero_copy(j):
            rows = pl.ds(pl.multiple_of(zb_ref[j] * blk_rows, blk_rows), blk_rows)
            return pltpu.make_async_copy(zbuf, xr_hbm.at[rows], zsem)

        def zstart(j, carry):
            zero_copy(j).start()
            return carry

        def zwait(j, carry):
            zero_copy(j).wait()
            return carry

        lax.fori_loop(0, nz_ref[0], zstart, 0)
        lax.fori_loop(0, nz_ref[0], zwait, 0)

    def scatter_rows():
        def start(r, carry):
            a0 = (i * tm + r) * TOP_K
            for k in range(TOP_K):
                pltpu.make_async_copy(f_ref.at[_tile_of_row(r)], xr_hbm.at[_tile_of_row(dest_ref[a0 + k])],
                                      sem).start(priority=k % 2)
            return carry

        lax.fori_loop(0, tm, start, 0, unroll=ISSUE_UNROLL)

        def wait(r, carry):
            for k in range(TOP_K):
                pltpu.make_async_copy(f_ref.at[_tile_of_row(0)], xr_hbm.at[_tile_of_row(0)], sem).wait()
            return carry

        lax.fori_loop(0, tm, wait, 0, unroll=ISSUE_UNROLL)

    if route_ctx:
        scatter_rows()
    else:
        pl.when(lax.rem(i, tpb) >= ctx_tiles)(scatter_rows)


def _dispatch(fpk, dest, zero_blocks, n_zero, n_rows, *, l_tot, lc, route_ctx):
    m = fpk.shape[0] // ROW_TILE
    tm = 256
    assert lc % tm == 0 and l_tot % tm == 0
    body = functools.partial(_dispatch_body, tm=tm, tpb=l_tot // tm, ctx_tiles=lc // tm, route_ctx=route_ctx)
    return pl.pallas_call(
        body,
        grid_spec=pltpu.PrefetchScalarGridSpec(
            num_scalar_prefetch=3,
            grid=(m // tm,),
            in_specs=[pl.BlockSpec((tm * ROW_TILE, LANES), lambda i, dst, zb, nz: (i, 0))],
            out_specs=pl.BlockSpec(memory_space=pl.ANY),
            scratch_shapes=[pltpu.SemaphoreType.DMA(()),
                            pltpu.VMEM((MOE_TM * ROW_TILE, LANES), jnp.uint32),
                            pltpu.SemaphoreType.DMA(())]),
        out_shape=jax.ShapeDtypeStruct((n_rows * ROW_TILE, LANES), jnp.uint32),
        compiler_params=_row_copy_params(),
        name="moe_dispatch",
    )(dest, zero_blocks, n_zero, fpk)


def _moe_gu_body(be_ref, nu_ref, x_ref, wgu_ref, bgu_ref, o_ref, wgb_ref, wlb_ref):
    blk = pl.program_id(0)
    ff = wgb_ref.shape[1]
    prev = be_ref[jnp.maximum(blk - 1, 0)]

    @pl.when(jnp.logical_or(blk == 0, be_ref[blk] != prev))
    def _():
        wgb_ref[...] = wgu_ref[:, :ff].astype(BF16)
        wlb_ref[...] = wgu_ref[:, ff:].astype(BF16)

    @pl.when(blk < nu_ref[0])
    def _():
        x_lo, x_hi = _unpack_halves(_load_row_tiles(x_ref, MOE_TM))
        w = x_lo.shape[1]
        h_glu = (jnp.dot(x_lo, wgb_ref[0:w, :], preferred_element_type=F32)
                 + jnp.dot(x_hi, wgb_ref[w:, :], preferred_element_type=F32) + bgu_ref[:, :ff])
        h_lin = (jnp.dot(x_lo, wlb_ref[0:w, :], preferred_element_type=F32)
                 + jnp.dot(x_hi, wlb_ref[w:, :], preferred_element_type=F32) + bgu_ref[:, ff:])
        h_glu = jnp.minimum(h_glu, SWIGLU_LIMIT)
        h_lin = jnp.clip(h_lin, -SWIGLU_LIMIT, SWIGLU_LIMIT)
        o_ref[...] = (h_glu * _sigmoid(SWIGLU_ALPHA * h_glu) * (h_lin + 1.0)).astype(BF16)

    @pl.when(blk >= nu_ref[0])
    def _():
        o_ref[...] = jnp.zeros_like(o_ref)


def _moe_down_body(be_ref, nu_ref, a_ref, w_ref, b_ref, o_ref, wb_ref):
    blk = pl.program_id(0)
    prev = be_ref[jnp.maximum(blk - 1, 0)]

    @pl.when(jnp.logical_or(blk == 0, be_ref[blk] != prev))
    def _():
        wb_ref[...] = w_ref[...].astype(BF16)

    @pl.when(blk < nu_ref[0])
    def _():
        y = jnp.dot(a_ref[...], wb_ref[...], preferred_element_type=F32) + b_ref[...]
        _store_row_tiles(o_ref, _pack_halves(y.astype(BF16)))

    @pl.when(blk >= nu_ref[0])
    def _():
        o_ref[...] = jnp.zeros_like(o_ref)


def _moe_experts(x_rows, block_expert, n_used, w_gu, b_gu, w_down, b_down, layer):
    n_rows = x_rows.shape[0] // ROW_TILE
    n_blocks = n_rows // MOE_TM
    tile_rows = pl.BlockSpec((MOE_TM * ROW_TILE, LANES), lambda i, be, nu: (i, 0))
    depth, n_e, ff, d = w_down.shape
    act = pl.pallas_call(
        _moe_gu_body,
        grid_spec=pltpu.PrefetchScalarGridSpec(
            num_scalar_prefetch=2,
            grid=(n_blocks,),
            in_specs=[tile_rows,
                      pl.BlockSpec((None, None, d, 2 * ff), lambda i, be, nu: (layer, be[i], 0, 0)),
                      pl.BlockSpec((None, None, 1, 2 * ff), lambda i, be, nu: (layer, be[i], 0, 0))],
            out_specs=pl.BlockSpec((MOE_TM, ff), lambda i, be, nu: (i, 0)),
            scratch_shapes=[pltpu.VMEM((d, ff), BF16), pltpu.VMEM((d, ff), BF16)]),
        out_shape=jax.ShapeDtypeStruct((n_rows, ff), BF16),
        compiler_params=_params(1),
        name="moe_gate_up",
    )(block_expert, n_used, x_rows, w_gu, b_gu.reshape(depth, n_e, 1, 2 * ff))
    return pl.pallas_call(
        _moe_down_body,
        grid_spec=pltpu.PrefetchScalarGridSpec(
            num_scalar_prefetch=2,
            grid=(n_blocks,),
            in_specs=[pl.BlockSpec((MOE_TM, ff), lambda i, be, nu: (i, 0)),
                      pl.BlockSpec((None, None, ff, d), lambda i, be, nu: (layer, be[i], 0, 0)),
                      pl.BlockSpec((None, None, 1, d), lambda i, be, nu: (layer, be[i], 0, 0))],
            out_specs=tile_rows,
            scratch_shapes=[pltpu.VMEM((ff, d), BF16)]),
        out_shape=jax.ShapeDtypeStruct((n_rows * ROW_TILE, LANES), jnp.uint32),
        compiler_params=_params(1),
        name="moe_down",
    )(block_expert, n_used, act, w_down, b_down.reshape(depth, n_e, 1, d))


def _combine_body(dest_ref, x_ref, rg_ref, cv_ref, lv_ref, y_hbm, o_ref, buf, sem,
                  *, tt, n_tiles, tpb, ctx_tiles, lc, route_ctx):
    i = pl.program_id(0)

    def routed(tile):
        return (tile >= 0) if route_ctx else (lax.rem(tile, tpb) >= ctx_tiles)

    def row_copy(src, slot, k, r):
        return pltpu.make_async_copy(y_hbm.at[_tile_of_row(src)], buf.at[slot, k, _tile_of_row(r)], sem.at[slot])

    def issue(tile, slot):
        def start(r, carry):
            a0 = (tile * tt + r) * TOP_K
            for k in range(TOP_K):
                row_copy(dest_ref[a0 + k], slot, k, r).start(priority=k % 2)
            return carry

        lax.fori_loop(0, tt, start, 0, unroll=ISSUE_UNROLL)

    @pl.when(jnp.logical_and(i == 0, routed(i)))
    def _():
        issue(i, 0)

    nxt = jnp.minimum(i + 1, n_tiles - 1)

    @pl.when(jnp.logical_and(i + 1 < n_tiles, routed(nxt)))
    def _():
        issue(nxt, lax.rem(i + 1, 2))

    slot = lax.rem(i, 2)

    @pl.when(routed(i))
    def _():
        def wait(r, carry):
            for k in range(TOP_K):
                row_copy(0, slot, k, 0).wait()
            return carry

        lax.fori_loop(0, tt, wait, 0, unroll=ISSUE_UNROLL)
        rg = rg_ref[...]
        w = ROW_TILE * LANES
        acc_lo = jnp.zeros((tt, w), F32)
        acc_hi = jnp.zeros((tt, w), F32)
        for k in range(TOP_K):
            y_lo, y_hi = _unpack_halves_f32(_load_row_tiles(buf.at[slot, k], tt))
            acc_lo = acc_lo + rg[:, k:k + 1] * y_lo
            acc_hi = acc_hi + rg[:, k:k + 1] * y_hi
        gate = _pick_mod(_is_ctx_rows(i, tt, tpb, lc), cv_ref, lv_ref, 5)
        o_ref[:, :w] = x_ref[:, :w] + gate[:, :w] * acc_lo
        o_ref[:, w:] = x_ref[:, w:] + gate[:, w:] * acc_hi

    @pl.when(jnp.logical_not(routed(i)))
    def _():
        o_ref[...] = x_ref[...]


def _combine(x, y_rows, rg, dest, cv, lv, *, l_tot, lc, route_ctx):
    m, d = x.shape
    tt = 128
    assert lc % tt == 0 and l_tot % tt == 0
    tpb = l_tot // tt
    n_tiles = m // tt
    body = functools.partial(_combine_body, tt=tt, n_tiles=n_tiles, tpb=tpb, ctx_tiles=lc // tt, lc=lc,
                             route_ctx=route_ctx)
    return pl.pallas_call(
        body,
        grid_spec=pltpu.PrefetchScalarGridSpec(
            num_scalar_prefetch=1,
            grid=(n_tiles,),
            in_specs=[pl.BlockSpec((tt, d), lambda i, dst: (i, 0)),
                      pl.BlockSpec((tt, LANES), lambda i, dst: (i, 0)),
                      pl.BlockSpec((MOD_ROWS, d), lambda i, dst: (0, 0)),
                      pl.BlockSpec((1, MOD_ROWS, d), lambda i, dst: (i // tpb, 0, 0)),
                      pl.BlockSpec(memory_space=pl.ANY)],
            out_specs=pl.BlockSpec((tt, d), lambda i, dst: (i, 0)),
            scratch_shapes=[pltpu.VMEM((2, TOP_K, tt * ROW_TILE, LANES), jnp.uint32),
                            pltpu.SemaphoreType.DMA((2,))]),
        out_shape=jax.ShapeDtypeStruct((m, d), F32),
        compiler_params=_row_copy_params(),
        name="moe_combine",
    )(dest, x, rg, cv, lv, y_rows)


def _moe_layer(x, gain, cv, lv, router_w_pad, router_b_pad, w_gu, b_gu, w_down, b_down, layer,
               *, l_tot, lc, route_ctx):
    m, d = x.shape
    n_e = w_gu.shape[1]
    fpk, ri, rg, cnt = _ffn_route(x, gain, cv, lv, router_w_pad, router_b_pad, layer,
                                  l_tot=l_tot, lc=lc, route_ctx=route_ctx, n_e=n_e)
    counts = cnt[0, :n_e].astype(jnp.int32)
    padded = (counts + MOE_TM - 1) // MOE_TM * MOE_TM
    pend = jnp.cumsum(padded)
    pstart = (pend - padded).astype(jnp.int32)
    n_blocks = -(-m * TOP_K // MOE_TM) + n_e
    n_rows = n_blocks * MOE_TM
    block_start = jnp.arange(n_blocks, dtype=jnp.int32) * MOE_TM
    block_expert = jnp.minimum(jnp.sum(block_start[:, None] >= pend[None, :], axis=1), n_e - 1).astype(jnp.int32)
    n_used = (pend[-1] // MOE_TM).astype(jnp.int32).reshape(1)
    dest = (pstart[ri[:, :TOP_K]] + ri[:, TOP_K:2 * TOP_K]).reshape(m * TOP_K)
    needs_zero = jnp.logical_or(block_start >= pend[-1],
                                jnp.any((block_start + MOE_TM)[:, None] == pend[None, :], axis=1))
    zero_blocks = jnp.nonzero(needs_zero, size=n_blocks, fill_value=0)[0].astype(jnp.int32)
    n_zero = jnp.sum(needs_zero).astype(jnp.int32).reshape(1)
    x_rows = _dispatch(fpk, dest, zero_blocks, n_zero, n_rows, l_tot=l_tot, lc=lc, route_ctx=route_ctx)
    y_rows = _moe_experts(x_rows, block_expert, n_used, w_gu, b_gu, w_down, b_down, layer)
    return _combine(x, y_rows, rg, dest, cv, lv, l_tot=l_tot, lc=lc, route_ctx=route_ctx)


def _rope_1d_tables(l_tot, half):
    inv = ROPE_BASE ** (-jnp.arange(half, dtype=F32) / half)
    ang = jnp.arange(l_tot, dtype=F32)[:, None] * inv
    return jnp.cos(ang), jnp.sin(ang)


def _axial_tables(l_tot, lc):
    n_freq = GQA_HEAD_DIM // 4
    t = jnp.maximum(jnp.arange(l_tot) - lc, 0)
    pos = jnp.stack([(t // GRID_W).astype(F32), (t % GRID_W).astype(F32)], axis=-1)
    inv_freq = ROPE_BASE ** (-jnp.arange(n_freq, dtype=F32) / n_freq)
    ang = pos[:, :, None] * inv_freq
    lane = jnp.arange(LANES)
    axis = (lane % GQA_HEAD_DIM) // (2 * n_freq)
    freq = lane % n_freq
    first_half = (lane % (2 * n_freq)) < n_freq
    is_lat = (jnp.arange(l_tot) >= lc)[:, None]
    cos = jnp.where(is_lat, jnp.cos(ang)[:, axis, freq], 1.0)
    sin = jnp.where(is_lat, jnp.sin(ang)[:, axis, freq], 0.0)
    return cos, jnp.where(first_half, -sin, 0.0), jnp.where(first_half, 0.0, sin)


def kernel(x, c, ctx, c_ctx, norm_mix, norm_ffn, mod_w, mod_b, ret_w_in, ret_decay, ret_gn, ret_w_out, mla_w_in, mla_q_a_norm, mla_kv_a_norm, mla_w_q_b, mla_w_kv_b, mla_q_norm, mla_k_norm, mla_w_out, sc_w_in, sc_conv_w, sc_conv_b, sc_w_out, gqa_w_in, gqa_b_in, gqa_q_norm, gqa_k_norm, gqa_sink, gqa_w_out, gqa_b_out, moe_router_w, moe_router_b, moe_w_gu, moe_b_gu, moe_w_down, moe_b_down):
    b, n_lat, d = x.shape
    lc = ctx.shape[1]
    l_tot = lc + n_lat
    m = b * l_tot
    depth = mod_w.shape[0]
    assert b + 1 <= MOD_ROWS and lc % 256 == 0 and n_lat % 256 == 0

    xs = jnp.concatenate([ctx, x], axis=1).reshape(m, d)
    cvec = jnp.zeros((MOD_ROWS, d), F32).at[:b].set(c).at[b].set(c_ctx)
    mod = _modulation(cvec, mod_w, mod_b).reshape(depth, MOD_ROWS, 6, d)
    pad_rows = ((0, 0), (0, MOD_ROWS - 6), (0, 0))
    ret_cos, ret_sin = _rope_1d_tables(l_tot, d // RET_HEADS // 2)
    tabs = _axial_tables(l_tot, lc)
    n_e = moe_router_w.shape[-1]
    router_w_pad = jnp.pad(moe_router_w, ((0, 0), (0, 0), (0, LANES - n_e)))
    router_b_pad = jnp.pad(moe_router_b, ((0, 0), (0, LANES - n_e))).reshape(depth, 1, LANES)

    for i in range(depth):
        kind, j = i % N_MIXERS, i // N_MIXERS
        cv = jnp.pad(mod[i, b], pad_rows[1:])
        lv = jnp.pad(mod[i, :b], pad_rows)
        lay = dict(b=b, l_tot=l_tot, lc=lc)
        h = _norm_mod(xs, norm_mix[i], cv, lv, l_tot=l_tot, lc=lc, r_shift=0, r_scale=1)
        if kind == 0:
            xs = _retention_layer(xs, h, cv, lv, ret_w_in[j], ret_decay[j], ret_gn[j], ret_w_out[j],
                                  ret_cos, ret_sin, **lay)
        elif kind == 1:
            xs = _mla_layer(xs, h, cv, lv, mla_w_in[j], mla_q_a_norm[j], mla_kv_a_norm[j], mla_w_q_b[j],
                            mla_w_kv_b[j], mla_q_norm[j], mla_k_norm[j], mla_w_out[j], tabs, **lay)
        elif kind == 2:
            xs = _shortconv_layer(xs, h, cv, lv, sc_w_in[j], sc_conv_w[j], sc_conv_b[j], sc_w_out[j], **lay)
        else:
            xs = _gqa_layer(xs, h, cv, lv, gqa_w_in[j], gqa_b_in[j], gqa_q_norm[j], gqa_k_norm[j],
                            gqa_sink[j], gqa_w_out[j], gqa_b_out[j], tabs, **lay)
        xs = _moe_layer(xs, norm_ffn[i], cv, lv, router_w_pad, router_b_pad, moe_w_gu, moe_b_gu,
                        moe_w_down, moe_b_down, i, l_tot=l_tot, lc=lc, route_ctx=i < depth - 1)
    return xs.reshape(b, l_tot, d)[:, lc:, :]
```

```python
import functools

import jax
import jax.numpy as jnp
from jax import lax
from jax.experimental import pallas as pl
from jax.experimental.pallas import tpu as pltpu

F32 = jnp.float32
BF16 = jnp.bfloat16

EPS = 1e-6
ROPE_BASE = 10000.0
NEG_INF = -1e30
LOG2_E = 1.4426950408889634
GRID_W = 64

RET_HEADS = 8
RET_CHUNK = 256
MLA_HEADS = 16
MLA_NOPE = 128
MLA_ROPE = 64
MLA_V = 128
MLA_RANK = 512
GQA_HEADS = 32
GQA_KV_HEADS = 4
GQA_GROUP = GQA_HEADS // GQA_KV_HEADS
GQA_HEAD_DIM = 64
WINDOW = 128
N_EXPERTS = 32
TOP_K = 4
SWIGLU_LIMIT = 7.0
SWIGLU_ALPHA = 1.702
N_MIXERS = 4

LANES = 128
VMEM_LIMIT = 56 * 1024 * 1024
MOE_TM = 256
MOD_ROWS = 8
ISSUE_UNROLL = 4
ROW_TILE = 8


def _params(n_axes, vmem=VMEM_LIMIT):
    return pltpu.CompilerParams(dimension_semantics=("arbitrary",) * n_axes,
                                vmem_limit_bytes=vmem)


def _row_copy_params():
    return pltpu.CompilerParams(dimension_semantics=("arbitrary",), vmem_limit_bytes=VMEM_LIMIT,
                                disable_bounds_checks=True)


def _row_tile(l_tot, cap):
    best = 16
    for t in range(16, cap + 1, 16):
        if l_tot % t == 0:
            best = t
    return best


def _is_ctx_rows(i, tm, tpb, lc):
    rows = lax.rem(i, tpb) * tm + lax.broadcasted_iota(jnp.int32, (tm, 1), 0)
    return rows < lc


def _pick_mod(is_ctx, cv_ref, lv_ref, r):
    return jnp.where(is_ctx, cv_ref[r:r + 1, :], lv_ref[0, r:r + 1, :])


def _sigmoid(x):
    return 1.0 / (1.0 + jnp.exp(-x))


def _rope3(x, c, s1, s2):
    return x * c + pltpu.roll(x, LANES - 16, axis=1) * s1 + pltpu.roll(x, 16, axis=1) * s2


def _mod_body(c_ref, w_ref, b_ref, o_ref):
    c = c_ref[...]
    s = (c * _sigmoid(c)).astype(BF16)
    o_ref[0] = jnp.dot(s, w_ref[0].astype(BF16), preferred_element_type=F32) + b_ref[0]


def _modulation(cvec, mod_w, mod_b):
    depth, d, n = mod_w.shape
    tn = 2048
    return pl.pallas_call(
        _mod_body,
        grid=(depth, n // tn),
        in_specs=[pl.BlockSpec((MOD_ROWS, d), lambda l, j: (0, 0)),
                  pl.BlockSpec((1, d, tn), lambda l, j: (l, 0, j)),
                  pl.BlockSpec((1, 1, tn), lambda l, j: (l, 0, j))],
        out_specs=pl.BlockSpec((1, MOD_ROWS, tn), lambda l, j: (l, 0, j)),
        out_shape=jax.ShapeDtypeStruct((depth, MOD_ROWS, n), F32),
        compiler_params=_params(2),
        name="modulation",
    )(cvec, mod_w, mod_b.reshape(depth, 1, n))


def _norm_mod_body(x_ref, g_ref, cv_ref, lv_ref, o_ref, *, tm, tpb, lc, r_shift, r_scale):
    x = x_ref[...]
    y = x * lax.rsqrt(jnp.mean(x * x, axis=-1, keepdims=True) + EPS) * g_ref[...]
    is_ctx = _is_ctx_rows(pl.program_id(0), tm, tpb, lc)
    shift = _pick_mod(is_ctx, cv_ref, lv_ref, r_shift)
    scale = _pick_mod(is_ctx, cv_ref, lv_ref, r_scale)
    o_ref[...] = (y * (1.0 + scale) + shift).astype(BF16)


def _norm_mod(x, gain, cv, lv, *, l_tot, lc, r_shift, r_scale):
    m, d = x.shape
    tm = _row_tile(l_tot, 544)
    tpb = l_tot // tm
    body = functools.partial(_norm_mod_body, tm=tm, tpb=tpb, lc=lc, r_shift=r_shift, r_scale=r_scale)
    return pl.pallas_call(
        body,
        grid=(m // tm,),
        in_specs=[pl.BlockSpec((tm, d), lambda i: (i, 0)),
                  pl.BlockSpec((1, d), lambda i: (0, 0)),
                  pl.BlockSpec((MOD_ROWS, d), lambda i: (0, 0)),
                  pl.BlockSpec((1, MOD_ROWS, d), lambda i: (i // tpb, 0, 0))],
        out_specs=pl.BlockSpec((tm, d), lambda i: (i, 0)),
        out_shape=jax.ShapeDtypeStruct((m, d), BF16),
        compiler_params=_params(1),
        name="norm_mod",
    )(x, gain.reshape(1, d), cv, lv)


def _mm_body(*refs, n_extra, epilogue):
    a_ref, w_ref = refs[0], refs[1]
    extra = refs[2:2 + n_extra]
    outs = refs[2 + n_extra:]
    acc = jnp.dot(a_ref[...], w_ref[...].astype(BF16), preferred_element_type=F32)
    epilogue(acc, extra, outs)


def _matmul(a, w, extras, extra_specs, out_shapes, out_specs, epilogue, *, tm, tn, name):
    m, k = a.shape
    n = w.shape[1]
    body = functools.partial(_mm_body, n_extra=len(extras), epilogue=epilogue)
    return pl.pallas_call(
        body,
        grid=(m // tm, n // tn),
        in_specs=[pl.BlockSpec((tm, k), lambda i, j: (i, 0)),
                  pl.BlockSpec((k, tn), lambda i, j: (0, j))] + list(extra_specs),
        out_specs=list(out_specs),
        out_shape=list(out_shapes),
        compiler_params=_params(2),
        name=name,
    )(a, w, *extras)


def _plain_epilogue(out_dtype):
    def epilogue(acc, extra, outs):
        outs[0][...] = acc.astype(out_dtype)
    return epilogue


def _residual_matmul(a, w, bias, x_res, cv, lv, *, l_tot, lc, r_gate, tm, tn, name):
    m = a.shape[0]
    n = w.shape[1]
    tpb = l_tot // tm

    def epilogue(acc, extra, outs):
        b_ref, x_ref, cv_ref, lv_ref = extra
        is_ctx = _is_ctx_rows(pl.program_id(0), tm, tpb, lc)
        gate = _pick_mod(is_ctx, cv_ref, lv_ref, r_gate)
        outs[0][...] = x_ref[...] + gate * (acc + b_ref[...])

    specs = [pl.BlockSpec((1, tn), lambda i, j: (0, j)),
             pl.BlockSpec((tm, tn), lambda i, j: (i, j)),
             pl.BlockSpec((MOD_ROWS, tn), lambda i, j: (0, j)),
             pl.BlockSpec((1, MOD_ROWS, tn), lambda i, j: (i // tpb, 0, j))]
    return _matmul(a, w, [bias.reshape(1, n), x_res, cv, lv], specs,
                   [jax.ShapeDtypeStruct((m, n), F32)],
                   [pl.BlockSpec((tm, tn), lambda i, j: (i, j))],
                   epilogue, tm=tm, tn=tn, name=name)[0]


def _ret_in_proj(h, w_in, cos, sin, *, l_tot, d_model):
    m = h.shape[0]
    n = w_in.shape[1]
    tm = _row_tile(l_tot, 2176)
    tn = 512
    tpb = l_tot // tm
    dk = d_model // RET_HEADS
    half = dk // 2
    n_q_tiles = d_model // tn
    k_scale = dk ** -0.5

    def epilogue(acc, extra, outs):
        cos_ref, sin_ref = extra
        j = pl.program_id(1)

        @pl.when(j < 2 * n_q_tiles)
        def _():
            c = cos_ref[...]
            s = sin_ref[...]
            mult = jnp.where(j >= n_q_tiles, k_scale, 1.0)
            for hh in range(tn // dk):
                x1 = acc[:, hh * dk:hh * dk + half]
                x2 = acc[:, hh * dk + half:(hh + 1) * dk]
                outs[0][:, hh * dk:hh * dk + half] = ((x1 * c - x2 * s) * mult).astype(BF16)
                outs[0][:, hh * dk + half:(hh + 1) * dk] = ((x2 * c + x1 * s) * mult).astype(BF16)

        @pl.when(j >= 2 * n_q_tiles)
        def _():
            outs[0][...] = acc.astype(BF16)

    tab_spec = pl.BlockSpec((tm, half), lambda i, j: (lax.rem(i, tpb), 0))
    return _matmul(h, w_in, [cos, sin], [tab_spec, tab_spec],
                   [jax.ShapeDtypeStruct((m, n), BF16)],
                   [pl.BlockSpec((tm, tn), lambda i, j: (i, j))],
                   epilogue, tm=tm, tn=tn, name="ret_in_proj")[0]


def _ret_body(lg_ref, q_ref, k_ref, v_ref, g_ref, gn_ref, o_ref, oacc_ref, sf_ref, sb_ref, *, nc_ctx, nc_tot):
    h = pl.program_id(1)
    ch = RET_CHUNK
    dv = v_ref.shape[-1]
    ii = lax.broadcasted_iota(jnp.int32, (ch, ch), 0).astype(F32)
    jj = lax.broadcasted_iota(jnp.int32, (ch, ch), 1).astype(F32)
    ri = lax.broadcasted_iota(jnp.int32, (ch, 1), 0).astype(F32)
    consts = []
    for direction in (0, 1):
        lg = lg_ref[direction, h]
        if direction == 0:
            diff = ii - jj
            q_decay = jnp.exp((ri + 1.0) * lg)
            k_decay = jnp.exp((ch - 1.0 - ri) * lg)
        else:
            diff = jj - ii
            q_decay = jnp.exp((ch - ri) * lg)
            k_decay = jnp.exp(ri * lg)
        intra = jnp.where(diff >= 0, jnp.exp(jnp.maximum(diff, 0.0) * lg), 0.0)
        chunk_decay = jnp.exp(jnp.full((1, dv), float(ch), F32) * lg)
        consts.append((intra, q_decay, k_decay, chunk_decay))
    sf_ref[...] = jnp.zeros_like(sf_ref)
    sb_ref[...] = jnp.zeros_like(sb_ref)
    oacc_ref[...] = jnp.zeros_like(oacc_ref)

    def chunk_rows(c):
        return pl.ds(pl.multiple_of(c * ch, ch), ch)

    def advance(c, s_ref, intra, q_decay, k_decay, chunk_decay):
        rows = chunk_rows(c)
        qc = q_ref[0, rows, :]
        kc = k_ref[0, rows, :]
        vc = v_ref[0, rows, :]
        scores = lax.dot_general(qc, kc, (((1,), (1,)), ((), ())), preferred_element_type=F32) * intra
        state = s_ref[...]
        out = (jnp.dot(scores.astype(BF16), vc, preferred_element_type=F32)
               + jnp.dot((qc.astype(F32) * q_decay).astype(BF16), state.astype(BF16),
                         preferred_element_type=F32))
        kt = (kc.astype(F32) * k_decay).T.astype(BF16)
        s_ref[...] = state * chunk_decay + jnp.dot(kt, vc, preferred_element_type=F32)
        return out

    def step(t, carry):
        c_fwd = t
        c_bwd = jnp.where(t < nc_ctx, nc_ctx - 1 - t, nc_tot - 1 - (t - nc_ctx))
        out_f = advance(c_fwd, sf_ref, *consts[0])
        out_b = advance(c_bwd, sb_ref, *consts[1])
        oacc_ref[chunk_rows(c_fwd), :] += out_f
        oacc_ref[chunk_rows(c_bwd), :] += out_b
        return carry

    lax.fori_loop(0, nc_tot, step, 0)

    def finish(c, carry):
        rows = chunk_rows(c)
        o = oacc_ref[rows, :]
        o = o * lax.rsqrt(jnp.mean(o * o, axis=-1, keepdims=True) + EPS) * gn_ref[...]
        g = g_ref[0, rows, :].astype(F32)
        o_ref[0, rows, :] = ((g * _sigmoid(g)) * o).astype(BF16)
        return carry

    lax.fori_loop(0, nc_tot, finish, 0)


def _retention(proj, log_gamma, gn, *, b, l_tot, lc, d_model):
    dk = d_model // RET_HEADS
    dv = 2 * d_model // RET_HEADS
    proj3 = proj.reshape(b, l_tot, proj.shape[-1])
    k_blk0 = d_model // dk
    v_blk0 = 2 * d_model // dv
    g_blk0 = 4 * d_model // dv
    body = functools.partial(_ret_body, nc_ctx=lc // RET_CHUNK, nc_tot=l_tot // RET_CHUNK)
    return pl.pallas_call(
        body,
        grid=(b, RET_HEADS),
        in_specs=[pl.BlockSpec(memory_space=pltpu.SMEM),
                  pl.BlockSpec((1, l_tot, dk), lambda bi, hi: (bi, 0, hi)),
                  pl.BlockSpec((1, l_tot, dk), lambda bi, hi: (bi, 0, k_blk0 + hi)),
                  pl.BlockSpec((1, l_tot, dv), lambda bi, hi: (bi, 0, v_blk0 + hi)),
                  pl.BlockSpec((1, l_tot, dv), lambda bi, hi: (bi, 0, g_blk0 + hi)),
                  pl.BlockSpec((1, dv), lambda bi, hi: (0, hi))],
        out_specs=pl.BlockSpec((1, l_tot, dv), lambda bi, hi: (bi, 0, hi)),
        out_shape=jax.ShapeDtypeStruct((b, l_tot, RET_HEADS * dv), BF16),
        scratch_shapes=[pltpu.VMEM((l_tot, dv), F32), pltpu.VMEM((dk, dv), F32), pltpu.VMEM((dk, dv), F32)],
        compiler_params=_params(2),
        name="retention",
    )(log_gamma, proj3, proj3, proj3, proj3, gn.reshape(1, RET_HEADS * dv))


def _retention_layer(x, h, cv, lv, w_in, decay, gn, w_out, cos, sin, *, b, l_tot, lc):
    m, d = x.shape
    proj = _ret_in_proj(h, w_in, cos, sin, l_tot=l_tot, d_model=d)
    log_gamma = jax.nn.log_sigmoid(decay.astype(F32))
    o = _retention(proj, log_gamma, gn, b=b, l_tot=l_tot, lc=lc, d_model=d).reshape(m, 2 * d)
    return _residual_matmul(o, w_out, jnp.zeros((d,), F32), x, cv, lv, l_tot=l_tot, lc=lc, r_gate=2,
                            tm=_row_tile(l_tot, 1088), tn=512, name="ret_out_proj")


def _mla_in_proj(h, w_in_pad, qa_gain, kva_gain, *, l_tot):
    m = h.shape[0]
    n = w_in_pad.shape[1]
    tm = _row_tile(l_tot, 1088)
    r = MLA_RANK

    def epilogue(acc, extra, outs):
        qg_ref, kg_ref = extra
        ql = acc[:, :r]
        kl = acc[:, r:2 * r]
        outs[0][...] = (ql * lax.rsqrt(jnp.mean(ql * ql, axis=-1, keepdims=True) + EPS)
                        * qg_ref[...]).astype(BF16)
        outs[1][...] = (kl * lax.rsqrt(jnp.mean(kl * kl, axis=-1, keepdims=True) + EPS)
                        * kg_ref[...]).astype(BF16)
        outs[2][...] = acc[:, 2 * r:]

    gspec = pl.BlockSpec((1, r), lambda i, j: (0, 0))
    return _matmul(h, w_in_pad, [qa_gain.reshape(1, r), kva_gain.reshape(1, r)], [gspec, gspec],
                   [jax.ShapeDtypeStruct((m, r), BF16), jax.ShapeDtypeStruct((m, r), BF16),
                    jax.ShapeDtypeStruct((m, LANES), F32)],
                   [pl.BlockSpec((tm, r), lambda i, j: (i, 0)),
                    pl.BlockSpec((tm, r), lambda i, j: (i, 0)),
                    pl.BlockSpec((tm, LANES), lambda i, j: (i, 0))],
                   epilogue, tm=tm, tn=n, name="mla_in_proj")


def _mla_q_proj(qn, wq_pad, q_gain_pad, tabs, *, l_tot):
    m = qn.shape[0]
    n = wq_pad.shape[1]
    tm = _row_tile(l_tot, 1088)
    tn = 1024
    tpb = l_tot // tm
    hw = 2 * LANES
    width = MLA_NOPE + MLA_ROPE
    scale = width ** -0.5 * LOG2_E

    def epilogue(acc, extra, outs):
        g_ref, c_ref, s1_ref, s2_ref = extra
        c, s1, s2 = c_ref[...], s1_ref[...], s2_ref[...]
        for hh in range(tn // hw):
            xs = acc[:, hh * hw:(hh + 1) * hw]
            inv = lax.rsqrt(jnp.sum(xs * xs, axis=-1, keepdims=True) * (1.0 / width) + EPS)
            y = xs * inv * g_ref[...] * scale
            outs[0][:, hh * hw:hh * hw + LANES] = y[:, :LANES].astype(BF16)
            outs[0][:, hh * hw + LANES:(hh + 1) * hw] = _rope3(y[:, LANES:], c, s1, s2).astype(BF16)

    tspec = pl.BlockSpec((tm, LANES), lambda i, j: (lax.rem(i, tpb), 0))
    return _matmul(qn, wq_pad, [q_gain_pad] + list(tabs),
                   [pl.BlockSpec((1, hw), lambda i, j: (0, 0)), tspec, tspec, tspec],
                   [jax.ShapeDtypeStruct((m, n), BF16)],
                   [pl.BlockSpec((tm, tn), lambda i, j: (i, j))],
                   epilogue, tm=tm, tn=tn, name="mla_q_proj")[0]


def _mla_kv_proj(kvn, w_kv, kr, k_gain_pad, tabs, *, l_tot):
    m = kvn.shape[0]
    n = w_kv.shape[1]
    tm = _row_tile(l_tot, 1088)
    tn = 1024
    tpb = l_tot // tm
    hw = MLA_NOPE + MLA_V
    width = MLA_NOPE + MLA_ROPE

    def epilogue(acc, extra, outs):
        kr_ref, g_ref, c_ref, s1_ref, s2_ref = extra
        kr_raw = kr_ref[...]
        ss_r = jnp.sum(kr_raw * kr_raw, axis=-1, keepdims=True)
        kr_rot = _rope3(kr_raw * g_ref[:, LANES:], c_ref[...], s1_ref[...], s2_ref[...])
        g_nope = g_ref[:, :LANES]
        for hh in range(tn // hw):
            kn = acc[:, hh * hw:hh * hw + MLA_NOPE]
            inv = lax.rsqrt((jnp.sum(kn * kn, axis=-1, keepdims=True) + ss_r) * (1.0 / width) + EPS)
            outs[0][:, hh * hw:hh * hw + LANES] = (kn * inv * g_nope).astype(BF16)
            outs[0][:, hh * hw + LANES:(hh + 1) * hw] = (kr_rot * inv).astype(BF16)
            outs[1][:, hh * MLA_V:(hh + 1) * MLA_V] = acc[:, hh * hw + MLA_NOPE:(hh + 1) * hw].astype(BF16)

    tspec = pl.BlockSpec((tm, LANES), lambda i, j: (lax.rem(i, tpb), 0))
    return _matmul(kvn, w_kv, [kr, k_gain_pad] + list(tabs),
                   [pl.BlockSpec((tm, LANES), lambda i, j: (i, 0)),
                    pl.BlockSpec((1, 2 * LANES), lambda i, j: (0, 0)), tspec, tspec, tspec],
                   [jax.ShapeDtypeStruct((m, n), BF16), jax.ShapeDtypeStruct((m, n // 2), BF16)],
                   [pl.BlockSpec((tm, tn), lambda i, j: (i, j)),
                    pl.BlockSpec((tm, tn // 2), lambda i, j: (i, j))],
                   epilogue, tm=tm, tn=tn, name="mla_kv_proj")


def _softmax_pv(q, k, v):
    s = lax.dot_general(q, k, (((1,), (1,)), ((), ())), preferred_element_type=F32)
    p = jnp.exp2(s - jnp.max(s, axis=-1, keepdims=True))
    den = jnp.sum(p, axis=-1, keepdims=True)
    return jnp.dot(p.astype(BF16), v, preferred_element_type=F32) / den


def _mla_attn_body(q_ref, k_ref, v_ref, o_ref, *, lc, tq, n_q):
    o_ref[0, 0:lc, :] = _softmax_pv(q_ref[0, 0:lc, :], k_ref[0, 0:lc, :], v_ref[0, 0:lc, :]).astype(BF16)

    def step(t, carry):
        rows = pl.ds(pl.multiple_of(lc + t * tq, LANES), tq)
        o_ref[0, rows, :] = _softmax_pv(q_ref[0, rows, :], k_ref[0], v_ref[0]).astype(BF16)
        return carry

    lax.fori_loop(0, n_q, step, 0, unroll=8)


def _mla_attention(q, k, v, *, b, l_tot, lc):
    hw = 2 * LANES
    tq = 256
    assert lc % LANES == 0 and (l_tot - lc) % tq == 0
    q3 = q.reshape(b, l_tot, MLA_HEADS * hw)
    k3 = k.reshape(b, l_tot, MLA_HEADS * hw)
    v3 = v.reshape(b, l_tot, MLA_HEADS * MLA_V)
    body = functools.partial(_mla_attn_body, lc=lc, tq=tq, n_q=(l_tot - lc) // tq)
    return pl.pallas_call(
        body,
        grid=(b, MLA_HEADS),
        in_specs=[pl.BlockSpec((1, l_tot, hw), lambda bi, hi: (bi, 0, hi)),
                  pl.BlockSpec((1, l_tot, hw), lambda bi, hi: (bi, 0, hi)),
                  pl.BlockSpec((1, l_tot, MLA_V), lambda bi, hi: (bi, 0, hi))],
        out_specs=pl.BlockSpec((1, l_tot, MLA_V), lambda bi, hi: (bi, 0, hi)),
        out_shape=jax.ShapeDtypeStruct((b, l_tot, MLA_HEADS * MLA_V), BF16),
        compiler_params=_params(2),
        name="mla_attention",
    )(q3, k3, v3)


def _mla_layer(x, h, cv, lv, w_in, qa_gain, kva_gain, w_q_b, w_kv_b, q_gain, k_gain, w_out, tabs,
               *, b, l_tot, lc):
    m, d = x.shape
    width = MLA_NOPE + MLA_ROPE
    hw = 2 * LANES
    w_in_pad = jnp.pad(w_in, ((0, 0), (0, 2 * MLA_RANK + LANES - w_in.shape[1])))
    wq_pad = jnp.pad(w_q_b.reshape(MLA_RANK, MLA_HEADS, width),
                     ((0, 0), (0, 0), (0, hw - width))).reshape(MLA_RANK, MLA_HEADS * hw)
    q_gain_pad = jnp.pad(q_gain, (0, hw - width)).reshape(1, hw)
    k_gain_pad = jnp.pad(k_gain, (0, hw - width)).reshape(1, hw)
    qn, kvn, kr = _mla_in_proj(h, w_in_pad, qa_gain, kva_gain, l_tot=l_tot)
    q = _mla_q_proj(qn, wq_pad, q_gain_pad, tabs, l_tot=l_tot)
    k, v = _mla_kv_proj(kvn, w_kv_b, kr, k_gain_pad, tabs, l_tot=l_tot)
    o = _mla_attention(q, k, v, b=b, l_tot=l_tot, lc=lc).reshape(m, MLA_HEADS * MLA_V)
    return _residual_matmul(o, w_out, jnp.zeros((d,), F32), x, cv, lv, l_tot=l_tot, lc=lc,
                            r_gate=2, tm=_row_tile(l_tot, 2176), tn=512, name="mla_out_proj")


def _conv_gate_body(bg_ref, cg_ref, u_ref, w_ref, cb_ref, o_ref, z_ref, *, l_tot, lc):
    pad = 8
    tn = o_ref.shape[-1]
    z_ref[0:pad, :] = jnp.zeros((pad, tn), F32)
    z_ref[pad + l_tot:, :] = jnp.zeros((pad, tn), F32)
    z_ref[pad:pad + l_tot, :] = cg_ref[0].astype(F32) * u_ref[0].astype(F32)
    rows = lax.broadcasted_iota(jnp.int32, (l_tot, 1), 0)
    z_prev = jnp.where(rows == lc, 0.0, z_ref[pad - 1:pad - 1 + l_tot, :])
    z_next = jnp.where(rows == lc - 1, 0.0, z_ref[pad + 1:pad + 1 + l_tot, :])
    z_mid = z_ref[pad:pad + l_tot, :]
    conv = z_prev * w_ref[0:1, :] + z_mid * w_ref[1:2, :] + z_next * w_ref[2:3, :] + cb_ref[...]
    o_ref[0] = (bg_ref[0].astype(F32) * conv).astype(BF16)


def _conv_gate(proj, conv_w, conv_b, *, b, l_tot, lc, d):
    tn = 256
    nb = d // tn
    proj3 = proj.reshape(b, l_tot, 3 * d)
    body = functools.partial(_conv_gate_body, l_tot=l_tot, lc=lc)
    return pl.pallas_call(
        body,
        grid=(b, nb),
        in_specs=[pl.BlockSpec((1, l_tot, tn), lambda bi, j: (bi, 0, j)),
                  pl.BlockSpec((1, l_tot, tn), lambda bi, j: (bi, 0, nb + j)),
                  pl.BlockSpec((1, l_tot, tn), lambda bi, j: (bi, 0, 2 * nb + j)),
                  pl.BlockSpec((3, tn), lambda bi, j: (0, j)),
                  pl.BlockSpec((1, tn), lambda bi, j: (0, j))],
        out_specs=pl.BlockSpec((1, l_tot, tn), lambda bi, j: (bi, 0, j)),
        out_shape=jax.ShapeDtypeStruct((b, l_tot, d), BF16),
        scratch_shapes=[pltpu.VMEM((l_tot + 16, tn), F32)],
        compiler_params=_params(2),
        name="conv_gate",
    )(proj3, proj3, proj3, conv_w, conv_b.reshape(1, d))


def _shortconv_layer(x, h, cv, lv, w_in, conv_w, conv_b, w_out, *, b, l_tot, lc):
    m, d = x.shape
    tm = _row_tile(l_tot, 2176)
    proj = _matmul(h, w_in, [], [], [jax.ShapeDtypeStruct((m, 3 * d), BF16)],
                   [pl.BlockSpec((tm, 512), lambda i, j: (i, j))],
                   _plain_epilogue(BF16), tm=tm, tn=512, name="sc_in_proj")[0]
    a = _conv_gate(proj, conv_w, conv_b, b=b, l_tot=l_tot, lc=lc, d=d).reshape(m, d)
    return _residual_matmul(a, w_out, jnp.zeros((d,), F32), x, cv, lv, l_tot=l_tot, lc=lc,
                            r_gate=2, tm=tm, tn=512, name="sc_out_proj")


def _split_hi_lo(x):
    hi = x.astype(BF16)
    lo = (x - hi.astype(F32)).astype(BF16)
    return hi, lo


def _head_rms_scale(x, seg_ref, segt_ref):
    hi, lo = _split_hi_lo(x * x)
    seg = seg_ref[...]
    ss = jnp.dot(hi, seg, preferred_element_type=F32) + jnp.dot(lo, seg, preferred_element_type=F32)
    inv = lax.rsqrt(ss * (1.0 / GQA_HEAD_DIM) + EPS)
    ihi, ilo = _split_hi_lo(inv)
    segt = segt_ref[...]
    return jnp.dot(ihi, segt, preferred_element_type=F32) + jnp.dot(ilo, segt, preferred_element_type=F32)


def _gqa_in_proj(h, w_in, b_in, qg_t, kg_t, seg, segt, tabs, *, l_tot):
    m = h.shape[0]
    n = w_in.shape[1]
    tm = _row_tile(l_tot, 1088)
    tn = 512
    tpb = l_tot // tm
    n_q_tiles = GQA_HEADS * GQA_HEAD_DIM // tn
    kv_w = GQA_KV_HEADS * GQA_HEAD_DIM
    assert n == n_q_tiles * tn + 2 * kv_w and 2 * kv_w == tn
    scale = GQA_HEAD_DIM ** -0.5 * LOG2_E

    def norm_rope(xs, g, extra, mult):
        seg_ref, segt_ref, c_ref, s1_ref, s2_ref = extra[3:8]
        y = xs * _head_rms_scale(xs, seg_ref, segt_ref) * g
        c, s1, s2 = c_ref[...], s1_ref[...], s2_ref[...]
        return [_rope3(y[:, a * LANES:(a + 1) * LANES], c, s1, s2) * mult
                for a in range(xs.shape[1] // LANES)]

    def dup_heads(slab, lo_mask):
        swapped = pltpu.roll(slab, GQA_HEAD_DIM, axis=1)
        return jnp.where(lo_mask, slab, swapped), jnp.where(lo_mask, swapped, slab)

    def epilogue(acc, extra, outs):
        b_ref, qg_ref, kg_ref = extra[0:3]
        j = pl.program_id(1)
        acc = acc + b_ref[...]

        @pl.when(j < n_q_tiles)
        def _():
            for a, slab in enumerate(norm_rope(acc, qg_ref[...], extra, scale)):
                outs[0][:, a * LANES:(a + 1) * LANES] = slab.astype(BF16)

        @pl.when(j == n_q_tiles)
        def _():
            lo_mask = lax.broadcasted_iota(jnp.int32, (tm, LANES), 1) < GQA_HEAD_DIM
            kpad = jnp.concatenate([acc[:, :kv_w], acc[:, :kv_w]], axis=1)
            k_slabs = norm_rope(kpad, kg_ref[...], extra, 1.0)[:kv_w // LANES]
            for a in range(kv_w // LANES):
                ka, kb = dup_heads(k_slabs[a], lo_mask)
                outs[1][:, (2 * a) * LANES:(2 * a + 1) * LANES] = ka.astype(BF16)
                outs[1][:, (2 * a + 1) * LANES:(2 * a + 2) * LANES] = kb.astype(BF16)
                va, vb = dup_heads(acc[:, kv_w + a * LANES:kv_w + (a + 1) * LANES], lo_mask)
                outs[2][:, (2 * a) * LANES:(2 * a + 1) * LANES] = va.astype(BF16)
                outs[2][:, (2 * a + 1) * LANES:(2 * a + 2) * LANES] = vb.astype(BF16)

    tspec = pl.BlockSpec((tm, LANES), lambda i, j: (lax.rem(i, tpb), 0))
    full = lambda shape: pl.BlockSpec(shape, lambda i, j: (0, 0))
    return _matmul(
        h, w_in, [b_in.reshape(1, n), qg_t, kg_t, seg, segt] + list(tabs),
        [pl.BlockSpec((1, tn), lambda i, j: (0, j)), full((1, tn)), full((1, tn)),
         full((tn, LANES)), full((LANES, tn)), tspec, tspec, tspec],
        [jax.ShapeDtypeStruct((m, n_q_tiles * tn), BF16),
         jax.ShapeDtypeStruct((m, GQA_KV_HEADS * LANES), BF16),
         jax.ShapeDtypeStruct((m, GQA_KV_HEADS * LANES), BF16)],
        [pl.BlockSpec((tm, tn), lambda i, j: (i, jnp.minimum(j, n_q_tiles - 1))),
         pl.BlockSpec((tm, GQA_KV_HEADS * LANES), lambda i, j: (i, 0)),
         pl.BlockSpec((tm, GQA_KV_HEADS * LANES), lambda i, j: (i, 0))],
        epilogue, tm=tm, tn=tn, name="gqa_in_proj")


def _gqa_attn_body(sink_ref, q_ref, k_ref, v_ref, o_ref, *, lc, nb):
    g = pl.program_id(1)
    r = WINDOW
    n_slab = GQA_GROUP // 2
    rows_all = GQA_GROUP * r
    row_head = lax.broadcasted_iota(jnp.int32, (rows_all, 1), 0) // r
    sink = jnp.zeros((rows_all, 1), F32)
    for hh in range(GQA_GROUP):
        sink = jnp.where(row_head == hh, sink_ref[g * GQA_GROUP + hh] * LOG2_E, sink)
    lo_mask = lax.broadcasted_iota(jnp.int32, (r, LANES), 1) < GQA_HEAD_DIM
    ii = lax.rem(lax.broadcasted_iota(jnp.int32, (rows_all, r), 0), r)
    jj = lax.broadcasted_iota(jnp.int32, (rows_all, r), 1)
    k_ctx = k_ref[0, 0:lc, :]
    v_ctx = v_ref[0, 0:lc, :]

    def stack_q(rows):
        parts = []
        for a in range(n_slab):
            qa = q_ref[0, rows, a * LANES:(a + 1) * LANES]
            zero = jnp.zeros_like(qa)
            parts += [jnp.where(lo_mask, qa, zero), jnp.where(lo_mask, zero, qa)]
        return jnp.concatenate(parts, axis=0)

    def attend(rows, blocks):
        qs = stack_q(rows)
        scores = []
        for kb, _, mask in blocks:
            s = lax.dot_general(qs, kb, (((1,), (1,)), ((), ())), preferred_element_type=F32)
            scores.append(s if mask is None else jnp.where(mask, s, NEG_INF))
        slabs = lambda a: [a[:, c * LANES:(c + 1) * LANES] for c in range(a.shape[1] // LANES)]
        mx = functools.reduce(jnp.maximum, [sl for s in scores for sl in slabs(s)])
        mx = jnp.maximum(sink, jnp.max(mx, axis=-1, keepdims=True))
        psum = jnp.zeros((rows_all, LANES), F32)
        acc = jnp.zeros((rows_all, LANES), F32)
        for s, (_, vb, _) in zip(scores, blocks):
            p = jnp.exp2(s - mx)
            psum = functools.reduce(jnp.add, slabs(p), psum)
            acc = acc + jnp.dot(p.astype(BF16), vb, preferred_element_type=F32)
        o = acc / (jnp.exp2(sink - mx) + jnp.sum(psum, axis=-1, keepdims=True))
        for a in range(n_slab):
            oa = jnp.where(lo_mask, o[(2 * a) * r:(2 * a + 1) * r], o[(2 * a + 1) * r:(2 * a + 2) * r])
            o_ref[0, rows, a * LANES:(a + 1) * LANES] = oa.astype(BF16)

    for cb in range(lc // r):
        attend(pl.ds(cb * r, r), [(k_ctx, v_ctx, None)])

    def step(t, carry):
        r_cur = pl.multiple_of(lc + t * r, r)
        r_prev = pl.multiple_of(lc + jnp.maximum(t - 1, 0) * r, r)
        r_next = pl.multiple_of(lc + jnp.minimum(t + 1, nb - 1) * r, r)
        blk = lambda r0: (k_ref[0, pl.ds(r0, r), :], v_ref[0, pl.ds(r0, r), :])
        m_prev = jnp.logical_and(jj >= ii, t > 0)
        m_next = jnp.logical_and(jj <= ii, t < nb - 1)
        attend(pl.ds(r_cur, r), [blk(r_prev) + (m_prev,), blk(r_cur) + (None,),
                                 blk(r_next) + (m_next,), (k_ctx, v_ctx, None)])
        return carry

    lax.fori_loop(0, nb, step, 0, unroll=2)


def _gqa_attention(q, k, v, sink, *, b, l_tot, lc):
    gw = GQA_GROUP * GQA_HEAD_DIM
    q3 = q.reshape(b, l_tot, GQA_HEADS * GQA_HEAD_DIM)
    k3 = k.reshape(b, l_tot, GQA_KV_HEADS * LANES)
    v3 = v.reshape(b, l_tot, GQA_KV_HEADS * LANES)
    body = functools.partial(_gqa_attn_body, lc=lc, nb=(l_tot - lc) // WINDOW)
    return pl.pallas_call(
        body,
        grid=(b, GQA_KV_HEADS),
        in_specs=[pl.BlockSpec(memory_space=pltpu.SMEM),
                  pl.BlockSpec((1, l_tot, gw), lambda bi, gi: (bi, 0, gi)),
                  pl.BlockSpec((1, l_tot, LANES), lambda bi, gi: (bi, 0, gi)),
                  pl.BlockSpec((1, l_tot, LANES), lambda bi, gi: (bi, 0, gi))],
        out_specs=pl.BlockSpec((1, l_tot, gw), lambda bi, gi: (bi, 0, gi)),
        out_shape=jax.ShapeDtypeStruct((b, l_tot, GQA_HEADS * GQA_HEAD_DIM), BF16),
        compiler_params=_params(2),
        name="gqa_attention",
    )(sink, q3, k3, v3)


def _gqa_layer(x, h, cv, lv, w_in, b_in, q_gain, k_gain, sink, w_out, b_out, tabs, *, b, l_tot, lc):
    m, d = x.shape
    tn = 512
    heads_per_tile = tn // GQA_HEAD_DIM
    col = jnp.arange(tn)
    seg = (col[:, None] // GQA_HEAD_DIM == jnp.arange(LANES)[None, :]).astype(BF16)
    qg_t = jnp.tile(q_gain, heads_per_tile).reshape(1, tn)
    kg_t = jnp.tile(k_gain, heads_per_tile).reshape(1, tn)
    q, k, v = _gqa_in_proj(h, w_in, b_in, qg_t, kg_t, seg, seg.T, tabs, l_tot=l_tot)
    o = _gqa_attention(q, k, v, sink.astype(F32), b=b, l_tot=l_tot, lc=lc).reshape(m, d)
    return _residual_matmul(o, w_out, b_out, x, cv, lv, l_tot=l_tot, lc=lc, r_gate=2,
                            tm=_row_tile(l_tot, 2176), tn=512, name="gqa_out_proj")


def _pack_halves(yb):
    w = yb.shape[1] // 2
    lo = lax.bitcast_convert_type(yb[:, :w].astype(F32), jnp.uint32)
    hi = lax.bitcast_convert_type(yb[:, w:].astype(F32), jnp.uint32)
    return lax.shift_right_logical(lo, jnp.uint32(16)) | hi


def _store_row_tiles(ref, words):
    n = words.shape[0]
    for s in range(ROW_TILE):
        ref[pl.ds(s, n, stride=ROW_TILE), :] = words[:, s * LANES:(s + 1) * LANES]


def _load_row_tiles(ref, n):
    return jnp.concatenate([ref[pl.ds(s, n, stride=ROW_TILE), :] for s in range(ROW_TILE)], axis=1)


def _tile_of_row(r):
    return pl.ds(pl.multiple_of(r * ROW_TILE, ROW_TILE), ROW_TILE)


def _unpack_halves_f32(words):
    lo = lax.bitcast_convert_type(lax.shift_left(words, jnp.uint32(16)), F32)
    hi = lax.bitcast_convert_type(words & jnp.uint32(0xFFFF0000), F32)
    return lo, hi


def _unpack_halves(words):
    lo, hi = _unpack_halves_f32(words)
    return lo.astype(BF16), hi.astype(BF16)


def _ffn_route_body(x_ref, g_ref, cv_ref, lv_ref, rw_ref, rb_ref, fpk_ref, ri_ref, rg_ref, cnt_ref,
                    carry_ref, *, tm, tpb, lc, route_ctx, n_e):
    i = pl.program_id(0)

    @pl.when(i == 0)
    def _():
        carry_ref[...] = jnp.zeros_like(carry_ref)

    x = x_ref[...]
    y = x * lax.rsqrt(jnp.mean(x * x, axis=-1, keepdims=True) + EPS) * g_ref[...]
    is_ctx = _is_ctx_rows(i, tm, tpb, lc)
    y = y * (1.0 + _pick_mod(is_ctx, cv_ref, lv_ref, 4)) + _pick_mod(is_ctx, cv_ref, lv_ref, 3)
    yb = y.astype(BF16)
    _store_row_tiles(fpk_ref, _pack_halves(yb))

    lane = lax.broadcasted_iota(jnp.int32, (tm, LANES), 1)
    logits = jnp.dot(yb, rw_ref[...].astype(BF16), preferred_element_type=F32) + rb_ref[...]
    logits = jnp.where(lane < n_e, logits, NEG_INF)
    vals, idxs, hots = [], [], []
    for _ in range(TOP_K):
        mk = jnp.max(logits, axis=-1, keepdims=True)
        ik = jnp.min(jnp.where(logits == mk, lane, LANES), axis=-1, keepdims=True)
        hot = lane == ik
        vals.append(mk)
        idxs.append(ik)
        hots.append(hot)
        logits = jnp.where(hot, NEG_INF, logits)
    exps = [jnp.exp(v - vals[0]) for v in vals]
    den = exps[0] + exps[1] + exps[2] + exps[3]

    routed = jnp.ones_like(is_ctx) if route_ctx else jnp.logical_not(is_ctx)
    chosen = jnp.zeros((tm, LANES), F32)
    for hot in hots:
        chosen = chosen + jnp.where(jnp.logical_and(hot, routed), 1.0, 0.0)
    before = (lax.broadcasted_iota(jnp.int32, (tm, tm), 0)
              > lax.broadcasted_iota(jnp.int32, (tm, tm), 1)).astype(BF16)
    arrived = jnp.dot(before, chosen.astype(BF16), preferred_element_type=F32) + carry_ref[...]
    carry_ref[...] = carry_ref[...] + jnp.sum(chosen, axis=0, keepdims=True)
    cnt_ref[...] = carry_ref[...]

    ri = jnp.zeros((tm, LANES), jnp.int32)
    rg = jnp.zeros((tm, LANES), F32)
    for k in range(TOP_K):
        rank = jnp.sum(jnp.where(hots[k], arrived, 0.0), axis=-1, keepdims=True)
        ri = jnp.where(lane == k, idxs[k], ri)
        ri = jnp.where(lane == TOP_K + k, rank.astype(jnp.int32), ri)
        rg = jnp.where(lane == k, exps[k] / den, rg)
    ri_ref[...] = ri
    rg_ref[...] = rg


def _ffn_route(x, gain, cv, lv, router_w_pad, router_b_pad, layer, *, l_tot, lc, route_ctx, n_e):
    m, d = x.shape
    tm = _row_tile(l_tot, 544)
    tpb = l_tot // tm
    body = functools.partial(_ffn_route_body, tm=tm, tpb=tpb, lc=lc, route_ctx=route_ctx, n_e=n_e)
    return pl.pallas_call(
        body,
        grid=(m // tm,),
        in_specs=[pl.BlockSpec((tm, d), lambda i: (i, 0)),
                  pl.BlockSpec((1, d), lambda i: (0, 0)),
                  pl.BlockSpec((MOD_ROWS, d), lambda i: (0, 0)),
                  pl.BlockSpec((1, MOD_ROWS, d), lambda i: (i // tpb, 0, 0)),
                  pl.BlockSpec((None, d, LANES), lambda i: (layer, 0, 0)),
                  pl.BlockSpec((None, 1, LANES), lambda i: (layer, 0, 0))],
        out_specs=[pl.BlockSpec((tm * ROW_TILE, LANES), lambda i: (i, 0)),
                   pl.BlockSpec((tm, LANES), lambda i: (i, 0)),
                   pl.BlockSpec((tm, LANES), lambda i: (i, 0)),
                   pl.BlockSpec((1, LANES), lambda i: (0, 0))],
        out_shape=[jax.ShapeDtypeStruct((m * ROW_TILE, LANES), jnp.uint32),
                   jax.ShapeDtypeStruct((m, LANES), jnp.int32),
                   jax.ShapeDtypeStruct((m, LANES), F32),
                   jax.ShapeDtypeStruct((1, LANES), F32)],
        scratch_shapes=[pltpu.VMEM((1, LANES), F32)],
        compiler_params=_params(1),
        name="ffn_route",
    )(x, gain.reshape(1, d), cv, lv, router_w_pad, router_b_pad)


def _dispatch_body(dest_ref, zb_ref, nz_ref, f_ref, xr_hbm, sem, zbuf, zsem, *, tm, tpb, ctx_tiles, route_ctx):
    i = pl.program_id(0)

    @pl.when(i == 0)
    def _():
        zbuf[...] = jnp.zeros_like(zbuf)
        blk_rows = MOE_TM * ROW_TILE

        def zero_copy(j):
            rows = pl.ds(pl.multiple_of(zb_ref[j] * blk_rows, blk_rows), blk_rows)
            return pltpu.make_async_copy(zbuf, xr_hbm.at[rows], zsem)

        def zstart(j, carry):
            zero_copy(j).start()
            return carry

        def zwait(j, carry):
            zero_copy(j).wait()
            return carry

        lax.fori_loop(0, nz_ref[0], zstart, 0)
        lax.fori_loop(0, nz_ref[0], zwait, 0)

    def scatter_rows():
        def start(r, carry):
            a0 = (i * tm + r) * TOP_K
            for k in range(TOP_K):
                pltpu.make_async_copy(f_ref.at[_tile_of_row(r)], xr_hbm.at[_tile_of_row(dest_ref[a0 + k])],
                                      sem).start(priority=k % 2)
            return carry

        lax.fori_loop(0, tm, start, 0, unroll=ISSUE_UNROLL)

        def wait(r, carry):
            for k in range(TOP_K):
                pltpu.make_async_copy(f_ref.at[_tile_of_row(0)], xr_hbm.at[_tile_of_row(0)], sem).wait()
            return carry

        lax.fori_loop(0, tm, wait, 0, unroll=ISSUE_UNROLL)

    if route_ctx:
        scatter_rows()
    else:
        pl.when(lax.rem(i, tpb) >= ctx_tiles)(scatter_rows)


def _dispatch(fpk, dest, zero_blocks, n_zero, n_rows, *, l_tot, lc, route_ctx):
    m = fpk.shape[0] // ROW_TILE
    tm = 256
    assert lc % tm == 0 and l_tot % tm == 0
    body = functools.partial(_dispatch_body, tm=tm, tpb=l_tot // tm, ctx_tiles=lc // tm, route_ctx=route_ctx)
    return pl.pallas_call(
        body,
        grid_spec=pltpu.PrefetchScalarGridSpec(
            num_scalar_prefetch=3,
            grid=(m // tm,),
            in_specs=[pl.BlockSpec((tm * ROW_TILE, LANES), lambda i, dst, zb, nz: (i, 0))],
            out_specs=pl.BlockSpec(memory_space=pl.ANY),
            scratch_shapes=[pltpu.SemaphoreType.DMA(()),
                            pltpu.VMEM((MOE_TM * ROW_TILE, LANES), jnp.uint32),
                            pltpu.SemaphoreType.DMA(())]),
        out_shape=jax.ShapeDtypeStruct((n_rows * ROW_TILE, LANES), jnp.uint32),
        compiler_params=_row_copy_params(),
        name="moe_dispatch",
    )(dest, zero_blocks, n_zero, fpk)


def _moe_gu_body(be_ref, nu_ref, x_ref, wgu_ref, bgu_ref, o_ref, wgb_ref, wlb_ref):
    blk = pl.program_id(0)
    ff = wgb_ref.shape[1]
    prev = be_ref[jnp.maximum(blk - 1, 0)]

    @pl.when(jnp.logical_or(blk == 0, be_ref[blk] != prev))
    def _():
        wgb_ref[...] = wgu_ref[:, :ff].astype(BF16)
        wlb_ref[...] = wgu_ref[:, ff:].astype(BF16)

    @pl.when(blk < nu_ref[0])
    def _():
        x_lo, x_hi = _unpack_halves(_load_row_tiles(x_ref, MOE_TM))
        w = x_lo.shape[1]
        h_glu = (jnp.dot(x_lo, wgb_ref[0:w, :], preferred_element_type=F32)
                 + jnp.dot(x_hi, wgb_ref[w:, :], preferred_element_type=F32) + bgu_ref[:, :ff])
        h_lin = (jnp.dot(x_lo, wlb_ref[0:w, :], preferred_element_type=F32)
                 + jnp.dot(x_hi, wlb_ref[w:, :], preferred_element_type=F32) + bgu_ref[:, ff:])
        h_glu = jnp.minimum(h_glu, SWIGLU_LIMIT)
        h_lin = jnp.clip(h_lin, -SWIGLU_LIMIT, SWIGLU_LIMIT)
        o_ref[...] = (h_glu * _sigmoid(SWIGLU_ALPHA * h_glu) * (h_lin + 1.0)).astype(BF16)

    @pl.when(blk >= nu_ref[0])
    def _():
        o_ref[...] = jnp.zeros_like(o_ref)


def _moe_down_body(be_ref, nu_ref, a_ref, w_ref, b_ref, o_ref, wb_ref):
    blk = pl.program_id(0)
    prev = be_ref[jnp.maximum(blk - 1, 0)]

    @pl.when(jnp.logical_or(blk == 0, be_ref[blk] != prev))
    def _():
        wb_ref[...] = w_ref[...].astype(BF16)

    @pl.when(blk < nu_ref[0])
    def _():
        y = jnp.dot(a_ref[...], wb_ref[...], preferred_element_type=F32) + b_ref[...]
        _store_row_tiles(o_ref, _pack_halves(y.astype(BF16)))

    @pl.when(blk >= nu_ref[0])
    def _():
        o_ref[...] = jnp.zeros_like(o_ref)


def _moe_experts(x_rows, block_expert, n_used, w_gu, b_gu, w_down, b_down, layer):
    n_rows = x_rows.shape[0] // ROW_TILE
    n_blocks = n_rows // MOE_TM
    tile_rows = pl.BlockSpec((MOE_TM * ROW_TILE, LANES), lambda i, be, nu: (i, 0))
    depth, n_e, ff, d = w_down.shape
    act = pl.pallas_call(
        _moe_gu_body,
        grid_spec=pltpu.PrefetchScalarGridSpec(
            num_scalar_prefetch=2,
            grid=(n_blocks,),
            in_specs=[tile_rows,
                      pl.BlockSpec((None, None, d, 2 * ff), lambda i, be, nu: (layer, be[i], 0, 0)),
                      pl.BlockSpec((None, None, 1, 2 * ff), lambda i, be, nu: (layer, be[i], 0, 0))],
            out_specs=pl.BlockSpec((MOE_TM, ff), lambda i, be, nu: (i, 0)),
            scratch_shapes=[pltpu.VMEM((d, ff), BF16), pltpu.VMEM((d, ff), BF16)]),
        out_shape=jax.ShapeDtypeStruct((n_rows, ff), BF16),
        compiler_params=_params(1),
        name="moe_gate_up",
    )(block_expert, n_used, x_rows, w_gu, b_gu.reshape(depth, n_e, 1, 2 * ff))
    return pl.pallas_call(
        _moe_down_body,
        grid_spec=pltpu.PrefetchScalarGridSpec(
            num_scalar_prefetch=2,
            grid=(n_blocks,),
            in_specs=[pl.BlockSpec((MOE_TM, ff), lambda i, be, nu: (i, 0)),
                      pl.BlockSpec((None, None, ff, d), lambda i, be, nu: (layer, be[i], 0, 0)),
                      pl.BlockSpec((None, None, 1, d), lambda i, be, nu: (layer, be[i], 0, 0))],
            out_specs=tile_rows,
            scratch_shapes=[pltpu.VMEM((ff, d), BF16)]),
        out_shape=jax.ShapeDtypeStruct((n_rows * ROW_TILE, LANES), jnp.uint32),
        compiler_params=_params(1),
        name="moe_down",
    )(block_expert, n_used, act, w_down, b_down.reshape(depth, n_e, 1, d))


def _combine_body(dest_ref, x_ref, rg_ref, cv_ref, lv_ref, y_hbm, o_ref, buf, sem,
                  *, tt, n_tiles, tpb, ctx_tiles, lc, route_ctx):
    i = pl.program_id(0)

    def routed(tile):
        return (tile >= 0) if route_ctx else (lax.rem(tile, tpb) >= ctx_tiles)

    def row_copy(src, slot, k, r):
        return pltpu.make_async_copy(y_hbm.at[_tile_of_row(src)], buf.at[slot, k, _tile_of_row(r)], sem.at[slot])

    def issue(tile, slot):
        def start(r, carry):
            a0 = (tile * tt + r) * TOP_K
            for k in range(TOP_K):
                row_copy(dest_ref[a0 + k], slot, k, r).start(priority=k % 2)
            return carry

        lax.fori_loop(0, tt, start, 0, unroll=ISSUE_UNROLL)

    @pl.when(jnp.logical_and(i == 0, routed(i)))
    def _():
        issue(i, 0)

    nxt = jnp.minimum(i + 1, n_tiles - 1)

    @pl.when(jnp.logical_and(i + 1 < n_tiles, routed(nxt)))
    def _():
        issue(nxt, lax.rem(i + 1, 2))

    slot = lax.rem(i, 2)

    @pl.when(routed(i))
    def _():
        def wait(r, carry):
            for k in range(TOP_K):
                row_copy(0, slot, k, 0).wait()
            return carry

        lax.fori_loop(0, tt, wait, 0, unroll=ISSUE_UNROLL)
        rg = rg_ref[...]
        w = ROW_TILE * LANES
        acc_lo = jnp.zeros((tt, w), F32)
        acc_hi = jnp.zeros((tt, w), F32)
        for k in range(TOP_K):
            y_lo, y_hi = _unpack_halves_f32(_load_row_tiles(buf.at[slot, k], tt))
            acc_lo = acc_lo + rg[:, k:k + 1] * y_lo
            acc_hi = acc_hi + rg[:, k:k + 1] * y_hi
        gate = _pick_mod(_is_ctx_rows(i, tt, tpb, lc), cv_ref, lv_ref, 5)
        o_ref[:, :w] = x_ref[:, :w] + gate[:, :w] * acc_lo
        o_ref[:, w:] = x_ref[:, w:] + gate[:, w:] * acc_hi

    @pl.when(jnp.logical_not(routed(i)))
    def _():
        o_ref[...] = x_ref[...]


def _combine(x, y_rows, rg, dest, cv, lv, *, l_tot, lc, route_ctx):
    m, d = x.shape
    tt = 128
    assert lc % tt == 0 and l_tot % tt == 0
    tpb = l_tot // tt
    n_tiles = m // tt
    body = functools.partial(_combine_body, tt=tt, n_tiles=n_tiles, tpb=tpb, ctx_tiles=lc // tt, lc=lc,
                             route_ctx=route_ctx)
    return pl.pallas_call(
        body,
        grid_spec=pltpu.PrefetchScalarGridSpec(
            num_scalar_prefetch=1,
            grid=(n_tiles,),
            in_specs=[pl.BlockSpec((tt, d), lambda i, dst: (i, 0)),
                      pl.BlockSpec((tt, LANES), lambda i, dst: (i, 0)),
                      pl.BlockSpec((MOD_ROWS, d), lambda i, dst: (0, 0)),
                      pl.BlockSpec((1, MOD_ROWS, d), lambda i, dst: (i // tpb, 0, 0)),
                      pl.BlockSpec(memory_space=pl.ANY)],
            out_specs=pl.BlockSpec((tt, d), lambda i, dst: (i, 0)),
            scratch_shapes=[pltpu.VMEM((2, TOP_K, tt * ROW_TILE, LANES), jnp.uint32),
                            pltpu.SemaphoreType.DMA((2,))]),
        out_shape=jax.ShapeDtypeStruct((m, d), F32),
        compiler_params=_row_copy_params(),
        name="moe_combine",
    )(dest, x, rg, cv, lv, y_rows)


def _moe_layer(x, gain, cv, lv, router_w_pad, router_b_pad, w_gu, b_gu, w_down, b_down, layer,
               *, l_tot, lc, route_ctx):
    m, d = x.shape
    n_e = w_gu.shape[1]
    fpk, ri, rg, cnt = _ffn_route(x, gain, cv, lv, router_w_pad, router_b_pad, layer,
                                  l_tot=l_tot, lc=lc, route_ctx=route_ctx, n_e=n_e)
    counts = cnt[0, :n_e].astype(jnp.int32)
    padded = (counts + MOE_TM - 1) // MOE_TM * MOE_TM
    pend = jnp.cumsum(padded)
    pstart = (pend - padded).astype(jnp.int32)
    n_blocks = -(-m * TOP_K // MOE_TM) + n_e
    n_rows = n_blocks * MOE_TM
    block_start = jnp.arange(n_blocks, dtype=jnp.int32) * MOE_TM
    block_expert = jnp.minimum(jnp.sum(block_start[:, None] >= pend[None, :], axis=1), n_e - 1).astype(jnp.int32)
    n_used = (pend[-1] // MOE_TM).astype(jnp.int32).reshape(1)
    dest = (pstart[ri[:, :TOP_K]] + ri[:, TOP_K:2 * TOP_K]).reshape(m * TOP_K)
    needs_zero = jnp.logical_or(block_start >= pend[-1],
                                jnp.any((block_start + MOE_TM)[:, None] == pend[None, :], axis=1))
    zero_blocks = jnp.nonzero(needs_zero, size=n_blocks, fill_value=0)[0].astype(jnp.int32)
    n_zero = jnp.sum(needs_zero).astype(jnp.int32).reshape(1)
    x_rows = _dispatch(fpk, dest, zero_blocks, n_zero, n_rows, l_tot=l_tot, lc=lc, route_ctx=route_ctx)
    y_rows = _moe_experts(x_rows, block_expert, n_used, w_gu, b_gu, w_down, b_down, layer)
    return _combine(x, y_rows, rg, dest, cv, lv, l_tot=l_tot, lc=lc, route_ctx=route_ctx)


def _rope_1d_tables(l_tot, half):
    inv = ROPE_BASE ** (-jnp.arange(half, dtype=F32) / half)
    ang = jnp.arange(l_tot, dtype=F32)[:, None] * inv
    return jnp.cos(ang), jnp.sin(ang)


def _axial_tables(l_tot, lc):
    n_freq = GQA_HEAD_DIM // 4
    t = jnp.maximum(jnp.arange(l_tot) - lc, 0)
    pos = jnp.stack([(t // GRID_W).astype(F32), (t % GRID_W).astype(F32)], axis=-1)
    inv_freq = ROPE_BASE ** (-jnp.arange(n_freq, dtype=F32) / n_freq)
    ang = pos[:, :, None] * inv_freq
    lane = jnp.arange(LANES)
    axis = (lane % GQA_HEAD_DIM) // (2 * n_freq)
    freq = lane % n_freq
    first_half = (lane % (2 * n_freq)) < n_freq
    is_lat = (jnp.arange(l_tot) >= lc)[:, None]
    cos = jnp.where(is_lat, jnp.cos(ang)[:, axis, freq], 1.0)
    sin = jnp.where(is_lat, jnp.sin(ang)[:, axis, freq], 0.0)
    return cos, jnp.where(first_half, -sin, 0.0), jnp.where(first_half, 0.0, sin)


def kernel(x, c, ctx, c_ctx, norm_mix, norm_ffn, mod_w, mod_b, ret_w_in, ret_decay, ret_gn, ret_w_out, mla_w_in, mla_q_a_norm, mla_kv_a_norm, mla_w_q_b, mla_w_kv_b, mla_q_norm, mla_k_norm, mla_w_out, sc_w_in, sc_conv_w, sc_conv_b, sc_w_out, gqa_w_in, gqa_b_in, gqa_q_norm, gqa_k_norm, gqa_sink, gqa_w_out, gqa_b_out, moe_router_w, moe_router_b, moe_w_gu, moe_b_gu, moe_w_down, moe_b_down):
    b, n_lat, d = x.shape
    lc = ctx.shape[1]
    l_tot = lc + n_lat
    m = b * l_tot
    depth = mod_w.shape[0]
    assert b + 1 <= MOD_ROWS and lc % 256 == 0 and n_lat % 256 == 0

    xs = jnp.concatenate([ctx, x], axis=1).reshape(m, d)
    cvec = jnp.zeros((MOD_ROWS, d), F32).at[:b].set(c).at[b].set(c_ctx)
    mod = _modulation(cvec, mod_w, mod_b).reshape(depth, MOD_ROWS, 6, d)
    pad_rows = ((0, 0), (0, MOD_ROWS - 6), (0, 0))
    ret_cos, ret_sin = _rope_1d_tables(l_tot, d // RET_HEADS // 2)
    tabs = _axial_tables(l_tot, lc)
    n_e = moe_router_w.shape[-1]
    router_w_pad = jnp.pad(moe_router_w, ((0, 0), (0, 0), (0, LANES - n_e)))
    router_b_pad = jnp.pad(moe_router_b, ((0, 0), (0, LANES - n_e))).reshape(depth, 1, LANES)

    for i in range(depth):
        kind, j = i % N_MIXERS, i // N_MIXERS
        cv = jnp.pad(mod[i, b], pad_rows[1:])
        lv = jnp.pad(mod[i, :b], pad_rows)
        lay = dict(b=b, l_tot=l_tot, lc=lc)
        h = _norm_mod(xs, norm_mix[i], cv, lv, l_tot=l_tot, lc=lc, r_shift=0, r_scale=1)
        if kind == 0:
            xs = _retention_layer(xs, h, cv, lv, ret_w_in[j], ret_decay[j], ret_gn[j], ret_w_out[j],
                                  ret_cos, ret_sin, **lay)
        elif kind == 1:
            xs = _mla_layer(xs, h, cv, lv, mla_w_in[j], mla_q_a_norm[j], mla_kv_a_norm[j], mla_w_q_b[j],
                            mla_w_kv_b[j], mla_q_norm[j], mla_k_norm[j], mla_w_out[j], tabs, **lay)
        elif kind == 2:
            xs = _shortconv_layer(xs, h, cv, lv, sc_w_in[j], sc_conv_w[j], sc_conv_b[j], sc_w_out[j], **lay)
        else:
            xs = _gqa_layer(xs, h, cv, lv, gqa_w_in[j], gqa_b_in[j], gqa_q_norm[j], gqa_k_norm[j],
                            gqa_sink[j], gqa_w_out[j], gqa_b_out[j], tabs, **lay)
        xs = _moe_layer(xs, norm_ffn[i], cv, lv, router_w_pad, router_b_pad, moe_w_gu, moe_b_gu,
                        moe_w_down, moe_b_down, i, l_tot=l_tot, lc=lc, route_ctx=i < depth - 1)
    return xs.reshape(b, l_tot, d)[:, lc:, :]
```
